```python
import math
import jax
import jax.numpy as jnp
from jax import lax
import numpy as np

D_MODEL = 1024
BATCH = 8
SEQ = 4096
DEPTH = 2
DEC_BATCH = 16
DEC_SEQ = 32
PAST_LEN = 2048

CHUNK = 64
EPS = 1e-6
N_BRANCH = 4
N_STATE = 7
POOL_WINDOWS = (2, 4, 8, 16)
N_POOL = 4
A_WIDTH = 512
A_GROUP = A_WIDTH // N_POOL
POOL_BUF = 15
B_HEADS = 4
B_DK = 128
B_DV = 128
B_QK = B_HEADS * B_DK
B_V = B_HEADS * B_DV
CONV_W = 4
B_CONV_CH = 2 * B_QK + B_V
C_HEADS = 4
C_DK = 64
C_DV = 128
C_QK = C_HEADS * C_DK
C_V = C_HEADS * C_DV
GLA_RANK = 16
GLA_NORMALIZER = 16.0
D_HEADS = 4
D_DK = 64
D_DV = 128
D_QK = D_HEADS * D_DK
D_V = D_HEADS * D_DV
BR_WIDTH = 512
FFN_HIDDEN = ((-(-8 * D_MODEL // 3) + 255) // 256) * 256
IN_SPLITS = (A_WIDTH,
             B_QK, B_QK, B_V, B_V, B_HEADS, B_HEADS,
             C_QK, C_QK, C_V, C_V, GLA_RANK,
             D_QK, D_QK, D_V, D_V, D_HEADS, D_HEADS)
IN_WIDTH = sum(IN_SPLITS)

kernel_name = 'hybrid_streaming_pool_gdn_gla_mlstm_step'


def _rmsnorm(x, g):
    xf = x.astype(jnp.float32)
    y = xf * lax.rsqrt(jnp.mean(xf * xf, axis=-1, keepdims=True) + EPS)
    return (y * g.astype(jnp.float32)).astype(x.dtype)


def _l2norm(x):
    return x * lax.rsqrt(jnp.sum(x * x, axis=-1, keepdims=True) + EPS)


def _to_chunks(t, chunk):
    b, t_len, h = t.shape[:3]
    t = t.reshape((b, t_len // chunk, chunk, h) + t.shape[3:])
    return jnp.moveaxis(t, (1, 3), (0, 2))


def _from_chunks(t):
    n, b, h, c, d = t.shape
    return jnp.moveaxis(t, (0, 2), (1, 3)).reshape(b, n * c, h, d)


def _pool_mixer(u, buf, pos0, w_group, scale):
    b, t_len, _ = u.shape
    full = jnp.concatenate([buf.astype(u.dtype), u], axis=1)
    cs = jnp.pad(jnp.cumsum(full.astype(jnp.float32), axis=1), ((0, 0), (1, 0), (0, 0)))
    pos = pos0 + jnp.arange(t_len)
    means = []
    for gi, w in enumerate(POOL_WINDOWS):
        cols = slice(gi * A_GROUP, (gi + 1) * A_GROUP)
        hi = cs[:, POOL_BUF + 1:POOL_BUF + 1 + t_len, cols]
        lo = cs[:, POOL_BUF + 1 - w:POOL_BUF + 1 - w + t_len, cols]
        cnt = jnp.minimum(pos + 1, w).astype(jnp.float32)
        means.append((hi - lo) / cnt[None, :, None])
    diff = jnp.concatenate(means, axis=-1) - u.astype(jnp.float32)
    y = jnp.einsum('btgc,gcd->btgd', diff.reshape(b, t_len, N_POOL, A_GROUP),
                   w_group.astype(jnp.float32)).reshape(b, t_len, A_WIDTH)
    return (y * scale.astype(jnp.float32)).astype(u.dtype), full[:, -POOL_BUF:]


def _causal_conv(u, buf, w):
    t_len = u.shape[1]
    full = jnp.concatenate([buf.astype(u.dtype), u], axis=1)
    y = full[:, 0:t_len] * w[0]
    for j in range(1, CONV_W):
        y = y + full[:, j:j + t_len] * w[j]
    return jax.nn.silu(y), full[:, -(CONV_W - 1):]


def _gated_delta(q, k, v, log_alpha, beta, s0, chunk):
    dv = v.shape[-1]
    qc, kc, vc, gc, bc = (_to_chunks(t, chunk) for t in (q, k, v, log_alpha, beta))
    g = jnp.cumsum(gc, axis=-1)
    causal = jnp.tril(jnp.ones((chunk, chunk), dtype=bool))
    strict = jnp.tril(jnp.ones((chunk, chunk), dtype=bool), -1)
    decay = jnp.exp(jnp.where(causal, g[..., :, None] - g[..., None, :], -jnp.inf))
    kb = kc * bc[..., None]
    lower = jnp.where(strict, jnp.einsum('nbhik,nbhjk->nbhij', kb, kc) * decay, 0.0)
    rhs = jnp.concatenate([vc * bc[..., None], kb * jnp.exp(g)[..., None]], axis=-1)
    sol = lax.linalg.triangular_solve(lower + jnp.eye(chunk, dtype=lower.dtype), rhs,
                                      left_side=True, lower=True, unit_diagonal=True)
    u_val, w_dec = sol[..., :dv], sol[..., dv:]
    attn = jnp.where(causal, jnp.einsum('nbhik,nbhjk->nbhij', qc, kc) * decay, 0.0)
    q_dec = qc * jnp.exp(g)[..., None]
    k_tail = kc * jnp.exp(g[..., -1:] - g)[..., None]
    g_last = jnp.exp(g[..., -1])

    def step(s, inp):
        q_i, u_i, w_i, a_i, k_i, gl_i = inp
        v_new = u_i - jnp.einsum('bhck,bhkv->bhcv', w_i, s)
        o = jnp.einsum('bhck,bhkv->bhcv', q_i, s) + jnp.einsum('bhij,bhjv->bhiv', a_i, v_new)
        s = s * gl_i[..., None, None] + jnp.einsum('bhck,bhcv->bhkv', k_i, v_new)
        return s, o

    s_final, o = lax.scan(step, s0, (q_dec, u_val, w_dec, attn, k_tail, g_last))
    return _from_chunks(o), s_final


def _gla(q, k, v, log_a, s0, chunk):
    qc, kc, vc, gc = (_to_chunks(t, chunk) for t in (q, k, v, log_a))
    bcum = jnp.cumsum(gc, axis=-2)
    ref = bcum[..., chunk // 2:chunk // 2 + 1, :]
    causal = jnp.tril(jnp.ones((chunk, chunk), dtype=bool))
    attn = jnp.einsum('nbhik,nbhjk->nbhij', qc * jnp.exp(bcum - ref), kc * jnp.exp(ref - bcum))
    o_intra = jnp.einsum('nbhij,nbhjv->nbhiv', jnp.where(causal, attn, 0.0), vc)
    q_inter = qc * jnp.exp(bcum)
    k_tail = kc * jnp.exp(bcum[..., -1:, :] - bcum)
    decay_last = jnp.exp(bcum[..., -1, :])

    def step(s, inp):
        q_i, v_i, k_i, d_i = inp
        o = jnp.einsum('bhck,bhkv->bhcv', q_i, s)
        s = s * d_i[..., None] + jnp.einsum('bhck,bhcv->bhkv', k_i, v_i)
        return s, o

    s_final, o_inter = lax.scan(step, s0, (q_inter, vc, k_tail, decay_last))
    return _from_chunks(o_inter + o_intra), s_final


def _mlstm(q, k, v, i_pre, log_f, c0, n0, m0, chunk):
    qc, kc, vc, ic, fc = (_to_chunks(t, chunk) for t in (q, k, v, i_pre, log_f))
    bcum = jnp.cumsum(fc, axis=-1)
    causal = jnp.tril(jnp.ones((chunk, chunk), dtype=bool))
    log_w = jnp.where(causal, bcum[..., :, None] - bcum[..., None, :] + ic[..., None, :], -jnp.inf)
    log_w_max = jnp.max(log_w, axis=-1)
    qk = jnp.einsum('nbhik,nbhjk->nbhij', qc, kc)

    def step(carry, inp):
        c_s, n_s, m_s = carry
        q_i, k_i, v_i, b_i, i_i, lw_i, lwm_i, qk_i = inp
        log_inter = b_i + m_s[..., None]
        m_t = jnp.maximum(log_inter, lwm_i)
        w_inter = jnp.exp(log_inter - m_t)
        p_w = jnp.exp(lw_i - m_t[..., None]) * qk_i
        num = w_inter[..., None] * jnp.einsum('bhck,bhkv->bhcv', q_i, c_s) + jnp.einsum('bhij,bhjv->bhiv', p_w, v_i)
        den = w_inter * jnp.einsum('bhck,bhk->bhc', q_i, n_s) + jnp.sum(p_w, axis=-1)
        h = num / jnp.maximum(jnp.abs(den), jnp.exp(-m_t))[..., None]
        m_new = m_t[..., -1]
        w_state = jnp.exp(b_i[..., -1] + m_s - m_new)
        w_k = jnp.exp(b_i[..., -1:] - b_i + i_i - m_new[..., None])
        c_s = w_state[..., None, None] * c_s + jnp.einsum('bhc,bhck,bhcv->bhkv', w_k, k_i, v_i)
        n_s = w_state[..., None] * n_s + jnp.einsum('bhc,bhck->bhk', w_k, k_i)
        return (c_s, n_s, m_new), h

    (c_f, n_f, m_f), h = lax.scan(step, (c0, n0, m0), (qc, kc, vc, bcum, ic, log_w, log_w_max, qk))
    return _from_chunks(h), c_f, n_f, m_f


def _layer(x, st, p, pos0, chunk):
    b, t_len, _ = x.shape
    dt = x.dtype
    f32 = jnp.float32
    a_buf, conv_buf, gdn_s, gla_s, ml_c, ml_n, ml_m = st
    h = _rmsnorm(x, p['norm_mix'])
    z = h @ p['w_in']
    split_at = [int(s) for s in np.cumsum(IN_SPLITS)[:-1]]
    (a_in, b_q, b_k, b_v, b_g, b_a, b_b, c_q, c_k, c_v, c_g, c_lr,
     d_q, d_k, d_v, d_o, d_i, d_f) = jnp.split(z, split_at, axis=-1)

    y_a, a_buf_new = _pool_mixer(a_in, a_buf, pos0, p['pool_w'], p['pool_scale'])

    qkv, conv_buf_new = _causal_conv(jnp.concatenate([b_q, b_k, b_v], axis=-1), conv_buf, p['gdn_conv_w'])
    qkv = qkv.astype(f32)
    bq = _l2norm(qkv[..., :B_QK].reshape(b, t_len, B_HEADS, B_DK)) * (B_DK ** -0.5)
    bk = _l2norm(qkv[..., B_QK:2 * B_QK].reshape(b, t_len, B_HEADS, B_DK))
    bv = qkv[..., 2 * B_QK:].reshape(b, t_len, B_HEADS, B_DV)
    log_alpha = -jnp.exp(p['gdn_a_log'].astype(f32)) * jax.nn.softplus(b_a.astype(f32) + p['gdn_dt_bias'].astype(f32))
    beta = jax.nn.sigmoid(b_b.astype(f32))
    o_b, gdn_s_new = _gated_delta(bq, bk, bv, log_alpha, beta, gdn_s.astype(f32), chunk)
    y_b = (_rmsnorm(o_b, p['gdn_norm']) * jax.nn.silu(b_g.astype(f32)).reshape(b, t_len, B_HEADS, B_DV))
    y_b = y_b.reshape(b, t_len, B_V).astype(dt)

    cq = c_q.astype(f32).reshape(b, t_len, C_HEADS, C_DK) * (C_DK ** -0.5)
    ck = c_k.astype(f32).reshape(b, t_len, C_HEADS, C_DK)
    cv = c_v.astype(f32).reshape(b, t_len, C_HEADS, C_DV)
    log_a = jax.nn.log_sigmoid(c_lr.astype(f32) @ p['gla_w_up'].astype(f32) + p['gla_b_up'].astype(f32)) / GLA_NORMALIZER
    o_c, gla_s_new = _gla(cq, ck, cv, log_a.reshape(b, t_len, C_HEADS, C_DK), gla_s.astype(f32), chunk)
    y_c = (_rmsnorm(o_c, p['gla_norm']) * jax.nn.silu(c_g.astype(f32)).reshape(b, t_len, C_HEADS, C_DV))
    y_c = y_c.reshape(b, t_len, C_V).astype(dt)

    dq = d_q.astype(f32).reshape(b, t_len, D_HEADS, D_DK) * (D_DK ** -0.5)
    dk = d_k.astype(f32).reshape(b, t_len, D_HEADS, D_DK)
    dv = d_v.astype(f32).reshape(b, t_len, D_HEADS, D_DV)
    i_pre = d_i.astype(f32) + p['mlstm_b_i'].astype(f32)
    log_f = jax.nn.log_sigmoid(d_f.astype(f32) + p['mlstm_b_f'].astype(f32))
    o_d, ml_c_new, ml_n_new, ml_m_new = _mlstm(dq, dk, dv, i_pre, log_f, ml_c.astype(f32), ml_n.astype(f32),
                                               ml_m.astype(f32), chunk)
    y_d = jax.nn.sigmoid(d_o.astype(f32)).reshape(b, t_len, D_HEADS, D_DV) * _rmsnorm(o_d, p['mlstm_norm'])
    y_d = y_d.reshape(b, t_len, D_V).astype(dt)

    merged = None
    for i, y_i in enumerate((y_a, y_b, y_c, y_d)):
        gate = jax.nn.sigmoid(h @ p['w_gate'][i] + p['b_gate'][i])
        term = gate * (y_i @ p['w_branch'][i])
        merged = term if merged is None else merged + term
    x = x + merged @ p['w_out']

    h2 = _rmsnorm(x, p['norm_ffn'])
    x = x + (jax.nn.silu(h2 @ p['w_ffn_gate']) * (h2 @ p['w_ffn_up'])) @ p['w_ffn_down']
    return x, (a_buf_new, conv_buf_new, gdn_s_new, gla_s_new, ml_c_new, ml_n_new, ml_m_new)


def setup_inputs(seed: int = 0) -> dict:
    key = jax.random.key(seed)
    keys = jax.random.split(key, 32)
    f32 = jnp.float32

    def nrm(i, shape, scale):
        return jax.random.normal(keys[i], shape, f32) * scale

    def gain(i, shape):
        return 1.0 + nrm(i, shape, 0.02)

    dt_init = jnp.exp(jax.random.uniform(keys[15], (DEPTH, B_HEADS), f32, math.log(1e-3), math.log(1e-1)))
    return {
        'x_prompt': nrm(0, (BATCH, SEQ, D_MODEL), 1.0),
        'x_sample': nrm(1, (DEC_BATCH, DEC_SEQ, D_MODEL), 1.0),
        'state_a_pool': nrm(2, (DEPTH, DEC_BATCH, POOL_BUF, A_WIDTH), 1.0),
        'state_b_conv': nrm(3, (DEPTH, DEC_BATCH, CONV_W - 1, B_CONV_CH), 1.0),
        'state_b_S': nrm(4, (DEPTH, DEC_BATCH, B_HEADS, B_DK, B_DV), 0.5),
        'state_c_S': nrm(5, (DEPTH, DEC_BATCH, C_HEADS, C_DK, C_DV), 2.0),
        'state_d_C': nrm(6, (DEPTH, DEC_BATCH, D_HEADS, D_DK, D_DV), 1.0),
        'state_d_n': nrm(7, (DEPTH, DEC_BATCH, D_HEADS, D_DK), 1.0),
        'state_d_m': nrm(8, (DEPTH, DEC_BATCH, D_HEADS), 1.0),
        'norm_mix': gain(9, (DEPTH, D_MODEL)),
        'w_in': nrm(10, (DEPTH, D_MODEL, IN_WIDTH), D_MODEL ** -0.5),
        'pool_w': nrm(11, (DEPTH, N_POOL, A_GROUP, A_GROUP), A_GROUP ** -0.5),
        'pool_scale': gain(12, (DEPTH, A_WIDTH)),
        'gdn_conv_w': nrm(13, (DEPTH, CONV_W, B_CONV_CH), CONV_W ** -0.5),
        'gdn_a_log': jnp.log(jax.random.uniform(keys[14], (DEPTH, B_HEADS), f32, 1.0, 16.0)),
        'gdn_dt_bias': dt_init + jnp.log(-jnp.expm1(-dt_init)),
        'gdn_norm': gain(16, (DEPTH, B_DV)),
        'gla_w_up': nrm(17, (DEPTH, GLA_RANK, C_QK), GLA_RANK ** -0.5),
        'gla_b_up': nrm(18, (DEPTH, C_QK), 0.1),
        'gla_norm': gain(19, (DEPTH, C_DV)),
        'mlstm_b_i': nrm(20, (DEPTH, D_HEADS), 0.1),
        'mlstm_b_f': 3.0 + jax.random.uniform(keys[21], (DEPTH, D_HEADS), f32, 0.0, 3.0),
        'mlstm_norm': gain(22, (DEPTH, D_DV)),
        'w_branch': nrm(23, (DEPTH, N_BRANCH, BR_WIDTH, D_MODEL), BR_WIDTH ** -0.5),
        'w_gate': nrm(24, (DEPTH, N_BRANCH, D_MODEL, D_MODEL), D_MODEL ** -0.5),
        'b_gate': nrm(25, (DEPTH, N_BRANCH, D_MODEL), 0.1),
        'w_out': nrm(26, (DEPTH, D_MODEL, D_MODEL), D_MODEL ** -0.5),
        'norm_ffn': gain(27, (DEPTH, D_MODEL)),
        'w_ffn_gate': nrm(28, (DEPTH, D_MODEL, FFN_HIDDEN), D_MODEL ** -0.5),
        'w_ffn_up': nrm(29, (DEPTH, D_MODEL, FFN_HIDDEN), D_MODEL ** -0.5),
        'w_ffn_down': nrm(30, (DEPTH, FFN_HIDDEN, D_MODEL), FFN_HIDDEN ** -0.5),
        'norm_final': gain(31, (D_MODEL,)),
    }


def reference(x_prompt, x_sample, state_a_pool, state_b_conv, state_b_S, state_c_S, state_d_C, state_d_n, state_d_m,
              norm_mix, w_in, pool_w, pool_scale, gdn_conv_w, gdn_a_log, gdn_dt_bias, gdn_norm,
              gla_w_up, gla_b_up, gla_norm, mlstm_b_i, mlstm_b_f, mlstm_norm,
              w_branch, w_gate, b_gate, w_out, norm_ffn, w_ffn_gate, w_ffn_up, w_ffn_down, norm_final):
    f32 = jnp.float32
    bp = x_prompt.shape[0]
    zero_p = (jnp.zeros((bp, POOL_BUF, A_WIDTH), x_prompt.dtype),
              jnp.zeros((bp, CONV_W - 1, B_CONV_CH), x_prompt.dtype),
              jnp.zeros((bp, B_HEADS, B_DK, B_DV), f32),
              jnp.zeros((bp, C_HEADS, C_DK, C_DV), f32),
              jnp.zeros((bp, D_HEADS, D_DK, D_DV), f32),
              jnp.zeros((bp, D_HEADS, D_DK), f32),
              jnp.zeros((bp, D_HEADS), f32))
    yp, ys = x_prompt, x_sample
    new_p, new_s = [], []
    for l in range(DEPTH):
        lp = dict(norm_mix=norm_mix[l], w_in=w_in[l], pool_w=pool_w[l], pool_scale=pool_scale[l],
                  gdn_conv_w=gdn_conv_w[l], gdn_a_log=gdn_a_log[l], gdn_dt_bias=gdn_dt_bias[l], gdn_norm=gdn_norm[l],
                  gla_w_up=gla_w_up[l], gla_b_up=gla_b_up[l], gla_norm=gla_norm[l],
                  mlstm_b_i=mlstm_b_i[l], mlstm_b_f=mlstm_b_f[l], mlstm_norm=mlstm_norm[l],
                  w_branch=w_branch[l], w_gate=w_gate[l], b_gate=b_gate[l], w_out=w_out[l],
                  norm_ffn=norm_ffn[l], w_ffn_gate=w_ffn_gate[l], w_ffn_up=w_ffn_up[l], w_ffn_down=w_ffn_down[l])
        yp, sp = _layer(yp, zero_p, lp, 0, CHUNK)
        cache_l = (state_a_pool[l], state_b_conv[l], state_b_S[l], state_c_S[l],
                   state_d_C[l], state_d_n[l], state_d_m[l])
        ys, ss = _layer(ys, cache_l, lp, PAST_LEN, x_sample.shape[1])
        new_p.append(sp)
        new_s.append(ss)
    y_prompt = _rmsnorm(yp, norm_final)
    y_sample = _rmsnorm(ys, norm_final)
    a_pool_p, b_conv_p, b_S_p, c_S_p, d_C_p, d_n_p, d_m_p = [jnp.stack([s[i] for s in new_p]) for i in range(N_STATE)]
    a_pool_s, b_conv_s, b_S_s, c_S_s, d_C_s, d_n_s, d_m_s = [jnp.stack([s[i] for s in new_s]) for i in range(N_STATE)]
    return (y_prompt, y_sample,
            a_pool_p, b_conv_p, b_S_p, c_S_p, d_C_p, d_n_p, d_m_p,
            a_pool_s, b_conv_s, b_S_s, c_S_s, d_C_s, d_n_s, d_m_s)
```

```python
import functools
import math

import jax
import jax.numpy as jnp
from jax import lax
from jax.experimental import pallas as pl
from jax.experimental.pallas import tpu as pltpu

F32 = jnp.float32
BF16 = jnp.bfloat16

D_MODEL = 1024
DEPTH = 2
PAST_LEN = 2048
CHUNK = 64
EPS = 1e-6
POOL_WINDOWS = (2, 4, 8, 16)
A_WIDTH = 512
A_GROUP = 128
POOL_BUF = 15
B_HEADS, B_DK, B_DV = 4, 128, 128
B_QK = B_HEADS * B_DK
B_V = B_HEADS * B_DV
CONV_W = 4
B_CONV_CH = 2 * B_QK + B_V
C_HEADS, C_DK, C_DV = 4, 64, 128
C_QK = C_HEADS * C_DK
C_V = C_HEADS * C_DV
GLA_RANK = 16
GLA_NORMALIZER = 16.0
D_HEADS, D_DK, D_DV = 4, 64, 128
D_QK = D_HEADS * D_DK
D_V = D_HEADS * D_DV
BR_WIDTH = 512
FFN_HIDDEN = 2816
IN_SPLITS = (A_WIDTH, B_QK, B_QK, B_V, B_V, B_HEADS, B_HEADS, C_QK, C_QK, C_V, C_V, GLA_RANK,
             D_QK, D_QK, D_V, D_V, D_HEADS, D_HEADS)

LANES = 128
Z_WIDTH = 5632
Z_A, Z_BQ, Z_BK, Z_BV, Z_BG = 0, 512, 1024, 1536, 2048
Z_CQ, Z_CK, Z_CV, Z_CG = 2560, 2816, 3072, 3584
Z_DQ, Z_DK, Z_DV, Z_DO = 4096, 4352, 4608, 5120
S_BA, S_BB, S_DI, S_DF, S_LR = 0, 4, 8, 12, 16

ROW_TILE = 256
TIME_TILE = 256
FFN_CHUNK = 1408
VMEM_LIMIT = 56 * 1024 * 1024


def _bdot(a, b):
    return jnp.dot(a.astype(BF16), b.astype(BF16), preferred_element_type=F32)


def _bdot_nt(a, b):
    return lax.dot_general(a.astype(BF16), b.astype(BF16), (((1,), (1,)), ((), ())), preferred_element_type=F32)


def _bdot_tn(a, b):
    return lax.dot_general(a.astype(BF16), b.astype(BF16), (((0,), (0,)), ((), ())), preferred_element_type=F32)


def _hdot(a, b):
    return jnp.dot(a, b, precision=lax.Precision.HIGHEST, preferred_element_type=F32)


def _hdot_tn(a, b):
    return lax.dot_general(a, b, (((0,), (0,)), ((), ())), precision=lax.Precision.HIGHEST,
                           preferred_element_type=F32)


def _sigmoid(x):
    return 1.0 / (1.0 + jnp.exp(-x))


def _softplus(x):
    return jnp.maximum(x, 0.0) + jnp.log(1.0 + jnp.exp(-jnp.abs(x)))


def _rms(x, gain):
    return x * lax.rsqrt(jnp.mean(x * x, axis=-1, keepdims=True) + EPS) * gain


def _tri(c, kind):
    r = lax.broadcasted_iota(jnp.int32, (c, c), 0)
    k = lax.broadcasted_iota(jnp.int32, (c, c), 1)
    if kind == "lower":
        return r >= k
    if kind == "strict":
        return r > k
    if kind == "upper":
        return r <= k
    return r == k


def _unit_lower_inverse(low, c):
    eye = _tri(c, "eye").astype(F32)
    p = eye - low
    m = low
    for _ in range(int(math.log2(c)) - 1):
        m = _hdot(m, m)
        p = p + _hdot(p, m)
    return p


def _resident(shape):
    nd = len(shape)
    return pl.BlockSpec(shape, lambda *_: (0,) * nd, pipeline_mode=pl.Buffered(1))


def _proj_kernel(x_ref, g_ref, wz_ref, ws_ref, z_ref, s_ref):
    hb = _rms(x_ref[...], g_ref[...]).astype(BF16)
    step = 512
    for c0 in range(0, Z_WIDTH, step):
        z_ref[:, c0:c0 + step] = jnp.dot(hb, wz_ref[:, c0:c0 + step], preferred_element_type=F32).astype(BF16)
    s_ref[...] = jnp.dot(hb, ws_ref[...], preferred_element_type=F32)


def _project(x2d, gain, wz, ws):
    n = x2d.shape[0]
    r = min(ROW_TILE, n)
    return pl.pallas_call(
        _proj_kernel,
        grid=(n // r,),
        in_specs=[pl.BlockSpec((r, D_MODEL), lambda i: (i, 0)),
                  _resident((1, D_MODEL)), _resident((D_MODEL, Z_WIDTH)), _resident((D_MODEL, LANES))],
        out_specs=[pl.BlockSpec((r, Z_WIDTH), lambda i: (i, 0)), pl.BlockSpec((r, LANES), lambda i: (i, 0))],
        out_shape=[jax.ShapeDtypeStruct((n, Z_WIDTH), BF16), jax.ShapeDtypeStruct((n, LANES), F32)],
        compiler_params=pltpu.CompilerParams(dimension_semantics=("parallel",), vmem_limit_bytes=VMEM_LIMIT),
        name="proj",
    )(x2d, gain, wz, ws)


def _pool_kernel(u_ref, buf_ref, pw_ref, sc_ref, y_ref, nbuf_ref, full_ref, *, tt, pos0):
    t = pl.program_id(1)
    hdr = 16

    @pl.when(t == 0)
    def _():
        full_ref[1:hdr, :] = buf_ref[0]

    @pl.when(t > 0)
    def _():
        full_ref[1:hdr, :] = full_ref[tt + 1:tt + hdr, :]

    full_ref[hdr:hdr + tt, :] = u_ref[0].astype(F32)
    nbuf_ref[0] = full_ref[tt + 1:tt + hdr, :]
    pos = pos0 + t * tt + lax.broadcasted_iota(jnp.int32, (tt, 1), 0)
    for gi, w in enumerate(POOL_WINDOWS):
        cols = slice(gi * A_GROUP, (gi + 1) * A_GROUP)
        u = full_ref[hdr:hdr + tt, cols]
        acc = u
        for j in range(1, w):
            acc = acc + full_ref[hdr - j:hdr - j + tt, cols]
        cnt = jnp.minimum(pos + 1, w).astype(F32)
        diff = acc / cnt - u
        y = _bdot(diff, pw_ref[gi]) * sc_ref[:, cols]
        y_ref[0, :, cols] = y.astype(BF16)


def _pool_mixer(z3, buf, pool_w, scale, pos0):
    b, t_len, _ = z3.shape
    tt = min(TIME_TILE, t_len)
    return pl.pallas_call(
        functools.partial(_pool_kernel, tt=tt, pos0=pos0),
        grid=(b, t_len // tt),
        in_specs=[pl.BlockSpec((1, tt, A_WIDTH), lambda i, j: (i, j, Z_A // A_WIDTH)),
                  pl.BlockSpec((1, POOL_BUF, A_WIDTH), lambda i, j: (i, 0, 0)),
                  _resident((4, A_GROUP, A_GROUP)), _resident((1, A_WIDTH))],
        out_specs=[pl.BlockSpec((1, tt, A_WIDTH), lambda i, j: (i, j, 0)),
                   pl.BlockSpec((1, POOL_BUF, A_WIDTH), lambda i, j: (i, 0, 0))],
        out_shape=[jax.ShapeDtypeStruct((b, t_len, A_WIDTH), BF16),
                   jax.ShapeDtypeStruct((b, POOL_BUF, A_WIDTH), F32)],
        scratch_shapes=[pltpu.VMEM((16 + tt, A_WIDTH), F32)],
        compiler_params=pltpu.CompilerParams(dimension_semantics=("parallel", "arbitrary"),
                                             vmem_limit_bytes=VMEM_LIMIT),
        name="pool",
    )(z3, buf, pool_w, scale)


def _gdn_kernel(q_ref, k_ref, v_ref, g_ref, sm_ref, cbuf_ref, s_in_ref, cw_ref, alog_ref, dt_ref, nrm_ref,
                y_ref, ncbuf_ref, s_ref, full_ref, *, tt, chunk):
    t = pl.program_id(1)
    hdr = 8

    @pl.when(t == 0)
    def _():
        full_ref[hdr - 3:hdr, :] = cbuf_ref[0]
        s_ref[0] = s_in_ref[0]

    @pl.when(t > 0)
    def _():
        full_ref[hdr - 3:hdr, :] = full_ref[tt + hdr - 3:tt + hdr, :]

    full_ref[hdr:hdr + tt, 0:B_QK] = q_ref[0].astype(F32)
    full_ref[hdr:hdr + tt, B_QK:2 * B_QK] = k_ref[0].astype(F32)
    full_ref[hdr:hdr + tt, 2 * B_QK:] = v_ref[0].astype(F32)
    ncbuf_ref[0] = full_ref[tt + hdr - 3:tt + hdr, :]

    causal = _tri(chunk, "lower")
    strict = _tri(chunk, "strict")
    tril = causal.astype(F32)
    triu = _tri(chunk, "upper").astype(F32)

    def conv(r0, c0):
        acc = full_ref[r0 + hdr - 3:r0 + hdr - 3 + chunk, c0:c0 + LANES] * cw_ref[0:1, c0:c0 + LANES]
        for j in range(1, CONV_W):
            acc = acc + full_ref[r0 + hdr - 3 + j:r0 + hdr - 3 + j + chunk, c0:c0 + LANES] * cw_ref[j:j + 1, c0:c0 + LANES]
        return acc * _sigmoid(acc)

    for c in range(tt // chunk):
        r0 = c * chunk
        sm = sm_ref[0, r0:r0 + chunk, :]
        log_alpha = alog_ref[...] * _softplus(sm + dt_ref[...])
        beta = _sigmoid(sm)
        g_col = _hdot(tril, log_alpha)
        g_row = _hdot_tn(log_alpha, triu)
        for h in range(B_HEADS):
            q = conv(r0, h * B_DK)
            k = conv(r0, B_QK + h * B_DK)
            v = conv(r0, 2 * B_QK + h * B_DV)
            q = q * lax.rsqrt(jnp.sum(q * q, axis=-1, keepdims=True) + EPS) * (B_DK ** -0.5)
            k = k * lax.rsqrt(jnp.sum(k * k, axis=-1, keepdims=True) + EPS)
            gi = g_col[:, S_BA + h:S_BA + h + 1]
            gj = g_row[S_BA + h:S_BA + h + 1, :]
            bt = beta[:, S_BB + h:S_BB + h + 1]
            decay = jnp.exp(jnp.where(causal, gi - gj, -jnp.inf))
            kb = k * bt
            low = jnp.where(strict, _bdot_nt(kb, k) * decay, 0.0)
            tinv = _unit_lower_inverse(low, chunk)
            eg = jnp.exp(gi)
            sol = _hdot(tinv, jnp.concatenate([v * bt, kb * eg], axis=-1))
            u_val, w_dec = sol[:, :B_DV], sol[:, B_DV:]
            attn = jnp.where(causal, _bdot_nt(q, k) * decay, 0.0)
            g_last = gi[chunk - 1:chunk, :]
            s = s_ref[0, h]
            v_new = u_val - _bdot(w_dec, s)
            o = _bdot(q * eg, s) + _bdot(attn, v_new)
            s_ref[0, h] = s * jnp.exp(g_last) + _bdot_tn(k * jnp.exp(g_last - gi), v_new)
            gate = g_ref[0, r0:r0 + chunk, h * B_DV:(h + 1) * B_DV].astype(F32)
            y = _rms(o, nrm_ref[...]) * (gate * _sigmoid(gate))
            y_ref[0, r0:r0 + chunk, h * B_DV:(h + 1) * B_DV] = y.astype(BF16)


def _gdn_mixer(z3, sm3, cbuf, s0, conv_w, alog_row, dt_row, norm_row, chunk):
    b, t_len, _ = z3.shape
    tt = min(TIME_TILE, t_len)
    wide = lambda off: pl.BlockSpec((1, tt, 512), lambda i, j: (i, j, off // 512))
    return pl.pallas_call(
        functools.partial(_gdn_kernel, tt=tt, chunk=chunk),
        grid=(b, t_len // tt),
        in_specs=[wide(Z_BQ), wide(Z_BK), wide(Z_BV), wide(Z_BG),
                  pl.BlockSpec((1, tt, LANES), lambda i, j: (i, j, 0)),
                  pl.BlockSpec((1, CONV_W - 1, B_CONV_CH), lambda i, j: (i, 0, 0)),
                  pl.BlockSpec((1, B_HEADS, B_DK, B_DV), lambda i, j: (i, 0, 0, 0)),
                  _resident((CONV_W, B_CONV_CH)), _resident((1, LANES)), _resident((1, LANES)),
                  _resident((1, B_DV))],
        out_specs=[pl.BlockSpec((1, tt, B_V), lambda i, j: (i, j, 0)),
                   pl.BlockSpec((1, CONV_W - 1, B_CONV_CH), lambda i, j: (i, 0, 0)),
                   pl.BlockSpec((1, B_HEADS, B_DK, B_DV), lambda i, j: (i, 0, 0, 0))],
        out_shape=[jax.ShapeDtypeStruct((b, t_len, B_V), BF16),
                   jax.ShapeDtypeStruct((b, CONV_W - 1, B_CONV_CH), F32),
                   jax.ShapeDtypeStruct((b, B_HEADS, B_DK, B_DV), F32)],
        scratch_shapes=[pltpu.VMEM((8 + tt, B_CONV_CH), F32)],
        compiler_params=pltpu.CompilerParams(dimension_semantics=("parallel", "arbitrary"),
                                             vmem_limit_bytes=VMEM_LIMIT),
        name="gdn",
    )(z3, z3, z3, z3, sm3, cbuf, s0, conv_w, alog_row, dt_row, norm_row)


def _gla_kernel(q_ref, k_ref, v_ref, g_ref, sm_ref, s_in_ref, wup_ref, bup_ref, nrm_ref,
                y_ref, s_ref, *, tt, chunk):
    t = pl.program_id(1)

    @pl.when(t == 0)
    def _():
        s_ref[0] = s_in_ref[0]

    causal = _tri(chunk, "lower")
    tril = causal.astype(F32)
    ones = jnp.ones((chunk, LANES), F32)
    for c in range(tt // chunk):
        r0 = c * chunk
        sm = sm_ref[0, r0:r0 + chunk, :]
        lr = _hdot(sm, wup_ref[...]) + bup_ref[...]
        log_a = -_softplus(-lr) / GLA_NORMALIZER
        bcum = _hdot(tril, log_a)
        mid = bcum[chunk // 2:chunk // 2 + 1, :]
        last = bcum[chunk - 1:chunk, :]
        qf = q_ref[0, r0:r0 + chunk, :].astype(F32) * (C_DK ** -0.5)
        kf = k_ref[0, r0:r0 + chunk, :].astype(F32)
        q_in = qf * jnp.exp(bcum - mid)
        k_in = kf * jnp.exp(mid - bcum)
        q_x = qf * jnp.exp(bcum)
        k_t = kf * jnp.exp(last - bcum)
        d_col = jnp.exp(_hdot_tn(log_a, ones))
        for h in range(C_HEADS):
            ks = slice(h * C_DK, (h + 1) * C_DK)
            vs = slice(h * C_DV, (h + 1) * C_DV)
            v = v_ref[0, r0:r0 + chunk, vs]
            attn = jnp.where(causal, _bdot_nt(q_in[:, ks], k_in[:, ks]), 0.0)
            s = s_ref[0, h]
            o = _bdot(q_x[:, ks], s) + _bdot(attn, v)
            s_ref[0, h] = s * d_col[ks, :] + _bdot_tn(k_t[:, ks], v)
            gate = g_ref[0, r0:r0 + chunk, vs].astype(F32)
            y = _rms(o, nrm_ref[...]) * (gate * _sigmoid(gate))
            y_ref[0, r0:r0 + chunk, vs] = y.astype(BF16)


def _gla_mixer(z3, sm3, s0, wup_pad, bup_row, norm_row, chunk):
    b, t_len, _ = z3.shape
    tt = min(TIME_TILE, t_len)
    blk = lambda w, off: pl.BlockSpec((1, tt, w), lambda i, j: (i, j, off // w))
    return pl.pallas_call(
        functools.partial(_gla_kernel, tt=tt, chunk=chunk),
        grid=(b, t_len // tt),
        in_specs=[blk(C_QK, Z_CQ), blk(C_QK, Z_CK), blk(C_V, Z_CV), blk(C_V, Z_CG),
                  pl.BlockSpec((1, tt, LANES), lambda i, j: (i, j, 0)),
                  pl.BlockSpec((1, C_HEADS, C_DK, C_DV), lambda i, j: (i, 0, 0, 0)),
                  _resident((LANES, C_QK)), _resident((1, C_QK)), _resident((1, C_DV))],
        out_specs=[pl.BlockSpec((1, tt, C_V), lambda i, j: (i, j, 0)),
                   pl.BlockSpec((1, C_HEADS, C_DK, C_DV), lambda i, j: (i, 0, 0, 0))],
        out_shape=[jax.ShapeDtypeStruct((b, t_len, C_V), BF16),
                   jax.ShapeDtypeStruct((b, C_HEADS, C_DK, C_DV), F32)],
        compiler_params=pltpu.CompilerParams(dimension_semantics=("parallel", "arbitrary"),
                                             vmem_limit_bytes=VMEM_LIMIT),
        name="gla",
    )(z3, z3, z3, z3, sm3, s0, wup_pad, bup_row, norm_row)


def _mlstm_kernel(q_ref, k_ref, v_ref, og_ref, sm_ref, cn_in_ref, m_in_ref, bi_ref, bf_ref, nrm_ref,
                  y_ref, cn_ref, m_ref, *, tt, chunk):
    t = pl.program_id(1)

    @pl.when(t == 0)
    def _():
        cn_ref[0] = cn_in_ref[0]
        m_ref[0] = m_in_ref[0]

    causal = _tri(chunk, "lower")
    tril = causal.astype(F32)
    triu = _tri(chunk, "upper").astype(F32)
    eye = _tri(chunk, "eye").astype(F32)
    ones = jnp.ones((chunk, LANES), F32)
    for c in range(tt // chunk):
        r0 = c * chunk
        sm = sm_ref[0, r0:r0 + chunk, :]
        i_pre = sm + bi_ref[...]
        log_f = -_softplus(-(sm + bf_ref[...]))
        b_col = _hdot(tril, log_f)
        b_row = _hdot_tn(log_f, triu)
        i_row = _hdot_tn(i_pre, eye)
        qf = q_ref[0, r0:r0 + chunk, :].astype(F32) * (D_DK ** -0.5)
        kf = k_ref[0, r0:r0 + chunk, :].astype(F32)
        for h in range(D_HEADS):
            ks = slice(h * D_DK, (h + 1) * D_DK)
            vs = slice(h * D_DV, (h + 1) * D_DV)
            q, k = qf[:, ks], kf[:, ks]
            v = v_ref[0, r0:r0 + chunk, vs].astype(F32)
            bi = b_col[:, S_DF + h:S_DF + h + 1]
            bj = b_row[S_DF + h:S_DF + h + 1, :]
            ij = i_row[S_DI + h:S_DI + h + 1, :]
            ii = i_pre[:, S_DI + h:S_DI + h + 1]
            log_w = jnp.where(causal, bi - bj + ij, -jnp.inf)
            lw_max = jnp.max(log_w, axis=-1, keepdims=True)
            m_s = m_ref[0, h:h + 1, 0:1]
            log_inter = bi + m_s
            m_t = jnp.maximum(log_inter, lw_max)
            w_inter = jnp.exp(log_inter - m_t)
            p_w = jnp.exp(log_w - m_t) * _bdot_nt(q, k)
            cn = cn_ref[0, h]
            qcn = _bdot(q, cn)
            num = w_inter * qcn[:, :D_DV] + _bdot(p_w, v)
            den = w_inter * qcn[:, D_DV:D_DV + 1] + jnp.sum(p_w, axis=-1, keepdims=True)
            hh = num / jnp.maximum(jnp.abs(den), jnp.exp(-m_t))
            m_new = m_t[chunk - 1:chunk, :]
            b_last = bi[chunk - 1:chunk, :]
            w_state = jnp.exp(b_last + m_s - m_new)
            w_k = jnp.exp(b_last - bi + ii - m_new)
            cn_ref[0, h] = w_state * cn + _bdot_tn(k * w_k, jnp.concatenate([v, ones], axis=-1))
            m_ref[0, h:h + 1, :] = jnp.broadcast_to(m_new, (1, LANES))
            gate = og_ref[0, r0:r0 + chunk, vs].astype(F32)
            y = _sigmoid(gate) * _rms(hh, nrm_ref[...])
            y_ref[0, r0:r0 + chunk, vs] = y.astype(BF16)


def _mlstm_mixer(z3, sm3, cn0, m0, bi_row, bf_row, norm_row, chunk):
    b, t_len, _ = z3.shape
    tt = min(TIME_TILE, t_len)
    blk = lambda w, off: pl.BlockSpec((1, tt, w), lambda i, j: (i, j, off // w))
    return pl.pallas_call(
        functools.partial(_mlstm_kernel, tt=tt, chunk=chunk),
        grid=(b, t_len // tt),
        in_specs=[blk(D_QK, Z_DQ), blk(D_QK, Z_DK), blk(D_V, Z_DV), blk(D_V, Z_DO),
                  pl.BlockSpec((1, tt, LANES), lambda i, j: (i, j, 0)),
                  pl.BlockSpec((1, D_HEADS, D_DK, 2 * D_DV), lambda i, j: (i, 0, 0, 0)),
                  pl.BlockSpec((1, 8, LANES), lambda i, j: (i, 0, 0)),
                  _resident((1, LANES)), _resident((1, LANES)), _resident((1, D_DV))],
        out_specs=[pl.BlockSpec((1, tt, D_V), lambda i, j: (i, j, 0)),
                   pl.BlockSpec((1, D_HEADS, D_DK, 2 * D_DV), lambda i, j: (i, 0, 0, 0)),
                   pl.BlockSpec((1, 8, LANES), lambda i, j: (i, 0, 0))],
        out_shape=[jax.ShapeDtypeStruct((b, t_len, D_V), BF16),
                   jax.ShapeDtypeStruct((b, D_HEADS, D_DK, 2 * D_DV), F32),
                   jax.ShapeDtypeStruct((b, 8, LANES), F32)],
        compiler_params=pltpu.CompilerParams(dimension_semantics=("parallel", "arbitrary"),
                                             vmem_limit_bytes=VMEM_LIMIT),
        name="mlstm",
    )(z3, z3, z3, z3, sm3, cn0, m0, bi_row, bf_row, norm_row)


def _merge_ffn_kernel(x_ref, ya_ref, yb_ref, yc_ref, yd_ref, gmix_ref, wgate_ref, bgate_ref, wbr_ref, wout_ref,
                      gffn_ref, wfg_ref, wfu_ref, wfd_ref, gfin_ref, o_ref, *, final):
    x = x_ref[...]
    hb = _rms(x, gmix_ref[...]).astype(BF16)
    merged = None
    for i, y_ref in enumerate((ya_ref, yb_ref, yc_ref, yd_ref)):
        gate = _sigmoid(jnp.dot(hb, wgate_ref[i], preferred_element_type=F32) + bgate_ref[i])
        term = gate * jnp.dot(y_ref[...], wbr_ref[i], preferred_element_type=F32)
        merged = term if merged is None else merged + term
    x = x + jnp.dot(merged.astype(BF16), wout_ref[...], preferred_element_type=F32)
    h2 = _rms(x, gffn_ref[...]).astype(BF16)
    for c0 in range(0, FFN_HIDDEN, FFN_CHUNK):
        a = jnp.dot(h2, wfg_ref[:, c0:c0 + FFN_CHUNK], preferred_element_type=F32)
        u = jnp.dot(h2, wfu_ref[:, c0:c0 + FFN_CHUNK], preferred_element_type=F32)
        f = (a * _sigmoid(a) * u).astype(BF16)
        x = x + jnp.dot(f, wfd_ref[c0:c0 + FFN_CHUNK, :], preferred_element_type=F32)
    if final:
        x = _rms(x, gfin_ref[...])
    o_ref[...] = x


def _merge_ffn(x2d, ys, gmix, wgate, bgate, wbr, wout, gffn, wfg, wfu, wfd, gfin, final):
    n = x2d.shape[0]
    r = min(ROW_TILE, n)
    row = lambda w: pl.BlockSpec((r, w), lambda i: (i, 0))
    return pl.pallas_call(
        functools.partial(_merge_ffn_kernel, final=final),
        grid=(n // r,),
        in_specs=[row(D_MODEL), row(BR_WIDTH), row(BR_WIDTH), row(BR_WIDTH), row(BR_WIDTH),
                  _resident((1, D_MODEL)), _resident((4, D_MODEL, D_MODEL)), _resident((4, 1, D_MODEL)),
                  _resident((4, BR_WIDTH, D_MODEL)), _resident((D_MODEL, D_MODEL)), _resident((1, D_MODEL)),
                  _resident((D_MODEL, FFN_HIDDEN)), _resident((D_MODEL, FFN_HIDDEN)),
                  _resident((FFN_HIDDEN, D_MODEL)), _resident((1, D_MODEL))],
        out_specs=row(D_MODEL),
        out_shape=jax.ShapeDtypeStruct((n, D_MODEL), F32),
        compiler_params=pltpu.CompilerParams(dimension_semantics=("parallel",), vmem_limit_bytes=VMEM_LIMIT),
        name="merge_ffn",
    )(x2d, *ys, gmix, wgate, bgate, wbr, wout, gffn, wfg, wfu, wfd, gfin)


def _lane_row(vec, off):
    return jnp.zeros((1, LANES), F32).at[0, off:off + vec.shape[0]].set(vec.astype(F32))


def _layer_params(l, p):
    w_in = p["w_in"][l]
    offs = [0]
    for s in IN_SPLITS:
        offs.append(offs[-1] + s)
    piece = lambda i: w_in[:, offs[i]:offs[i + 1]]
    wide_ids = (0, 1, 2, 3, 4, 7, 8, 9, 10, 12, 13, 14, 15)
    wz = jnp.concatenate([piece(i) for i in wide_ids], axis=1).astype(BF16)
    small = jnp.concatenate([piece(5), piece(6), piece(16), piece(17), piece(11)], axis=1)
    ws = jnp.zeros((D_MODEL, LANES), F32).at[:, :small.shape[1]].set(small).astype(BF16)
    return dict(
        gmix=p["norm_mix"][l].reshape(1, D_MODEL), wz=wz, ws=ws,
        pool_w=p["pool_w"][l].astype(BF16), pool_scale=p["pool_scale"][l].reshape(1, A_WIDTH),
        conv_w=p["gdn_conv_w"][l],
        alog_row=_lane_row(-jnp.exp(p["gdn_a_log"][l].astype(F32)), S_BA),
        dt_row=_lane_row(p["gdn_dt_bias"][l], S_BA),
        gdn_norm=p["gdn_norm"][l].reshape(1, B_DV),
        wup=jnp.zeros((LANES, C_QK), F32).at[S_LR:S_LR + GLA_RANK].set(p["gla_w_up"][l].astype(F32)),
        bup=p["gla_b_up"][l].reshape(1, C_QK), gla_norm=p["gla_norm"][l].reshape(1, C_DV),
        bi_row=_lane_row(p["mlstm_b_i"][l], S_DI), bf_row=_lane_row(p["mlstm_b_f"][l], S_DF),
        mlstm_norm=p["mlstm_norm"][l].reshape(1, D_DV),
        wgate=p["w_gate"][l].astype(BF16), bgate=p["b_gate"][l].reshape(4, 1, D_MODEL),
        wbr=p["w_branch"][l].astype(BF16), wout=p["w_out"][l].astype(BF16),
        gffn=p["norm_ffn"][l].reshape(1, D_MODEL),
        wfg=p["w_ffn_gate"][l].astype(BF16), wfu=p["w_ffn_up"][l].astype(BF16), wfd=p["w_ffn_down"][l].astype(BF16),
    )


def _layer(x, st, lp, gfin, pos0, chunk, final):
    b, t_len, _ = x.shape
    a_buf, conv_buf, gdn_s, gla_s, ml_c, ml_n, ml_m = st
    x2d = x.reshape(b * t_len, D_MODEL)
    z, sm = _project(x2d, lp["gmix"], lp["wz"], lp["ws"])
    z3 = z.reshape(b, t_len, Z_WIDTH)
    sm3 = sm.reshape(b, t_len, LANES)
    y_a, a_new = _pool_mixer(z3, a_buf, lp["pool_w"], lp["pool_scale"], pos0)
    y_b, conv_new, gdn_new = _gdn_mixer(z3, sm3, conv_buf, gdn_s, lp["conv_w"], lp["alog_row"], lp["dt_row"],
                                        lp["gdn_norm"], chunk)
    y_c, gla_new = _gla_mixer(z3, sm3, gla_s, lp["wup"], lp["bup"], lp["gla_norm"], chunk)
    cn0 = jnp.concatenate([ml_c, jnp.broadcast_to(ml_n[..., None], ml_c.shape)], axis=-1)
    m0 = jnp.zeros((b, 8, LANES), F32).at[:, :D_HEADS, :].set(jnp.broadcast_to(ml_m[..., None], (b, D_HEADS, LANES)))
    y_d, cn_new, m_new = _mlstm_mixer(z3, sm3, cn0, m0, lp["bi_row"], lp["bf_row"], lp["mlstm_norm"], chunk)
    ys = [y.reshape(b * t_len, BR_WIDTH) for y in (y_a, y_b, y_c, y_d)]
    x_new = _merge_ffn(x2d, ys, lp["gmix"], lp["wgate"], lp["bgate"], lp["wbr"], lp["wout"], lp["gffn"],
                       lp["wfg"], lp["wfu"], lp["wfd"], gfin, final)
    new_st = (a_new, conv_new, gdn_new, gla_new, cn_new[..., :D_DV], cn_new[..., D_DV], m_new[:, :D_HEADS, 0])
    return x_new.reshape(b, t_len, D_MODEL), new_st


def kernel(x_prompt, x_sample, state_a_pool, state_b_conv, state_b_S, state_c_S, state_d_C, state_d_n, state_d_m,
           norm_mix, w_in, pool_w, pool_scale, gdn_conv_w, gdn_a_log, gdn_dt_bias, gdn_norm,
           gla_w_up, gla_b_up, gla_norm, mlstm_b_i, mlstm_b_f, mlstm_norm,
           w_branch, w_gate, b_gate, w_out, norm_ffn, w_ffn_gate, w_ffn_up, w_ffn_down, norm_final):
    p = dict(norm_mix=norm_mix, w_in=w_in, pool_w=pool_w, pool_scale=pool_scale, gdn_conv_w=gdn_conv_w,
             gdn_a_log=gdn_a_log, gdn_dt_bias=gdn_dt_bias, gdn_norm=gdn_norm, gla_w_up=gla_w_up, gla_b_up=gla_b_up,
             gla_norm=gla_norm, mlstm_b_i=mlstm_b_i, mlstm_b_f=mlstm_b_f, mlstm_norm=mlstm_norm,
             w_branch=w_branch, w_gate=w_gate, b_gate=b_gate, w_out=w_out, norm_ffn=norm_ffn,
             w_ffn_gate=w_ffn_gate, w_ffn_up=w_ffn_up, w_ffn_down=w_ffn_down)
    bp = x_prompt.shape[0]
    zero_p = (jnp.zeros((bp, POOL_BUF, A_WIDTH), F32), jnp.zeros((bp, CONV_W - 1, B_CONV_CH), F32),
              jnp.zeros((bp, B_HEADS, B_DK, B_DV), F32), jnp.zeros((bp, C_HEADS, C_DK, C_DV), F32),
              jnp.zeros((bp, D_HEADS, D_DK, D_DV), F32), jnp.zeros((bp, D_HEADS, D_DK), F32),
              jnp.zeros((bp, D_HEADS), F32))
    gfin = norm_final.reshape(1, D_MODEL)
    yp, ys = x_prompt, x_sample
    new_p, new_s = [], []
    for l in range(DEPTH):
        lp = _layer_params(l, p)
        final = l == DEPTH - 1
        yp, sp = _layer(yp, zero_p, lp, gfin, 0, CHUNK, final)
        cache_l = (state_a_pool[l], state_b_conv[l], state_b_S[l], state_c_S[l],
                   state_d_C[l], state_d_n[l], state_d_m[l])
        ys, ss = _layer(ys, cache_l, lp, gfin, PAST_LEN, x_sample.shape[1], final)
        new_p.append(sp)
        new_s.append(ss)
    outs_p = [jnp.stack([s[i] for s in new_p]) for i in range(7)]
    outs_s = [jnp.stack([s[i] for s in new_s]) for i in range(7)]
    return (yp, ys, *outs_p, *outs_s)
```

```python
import functools
import math

import jax
import jax.numpy as jnp
from jax import lax
from jax.experimental import pallas as pl
from jax.experimental.pallas import tpu as pltpu

F32 = jnp.float32
BF16 = jnp.bfloat16

D_MODEL = 1024
DEPTH = 2
PAST_LEN = 2048
CHUNK = 64
EPS = 1e-6
POOL_WINDOWS = (2, 4, 8, 16)
A_WIDTH = 512
A_GROUP = 128
POOL_BUF = 15
B_HEADS, B_DK, B_DV = 4, 128, 128
B_QK = B_HEADS * B_DK
B_V = B_HEADS * B_DV
CONV_W = 4
B_CONV_CH = 2 * B_QK + B_V
C_HEADS, C_DK, C_DV = 4, 64, 128
C_QK = C_HEADS * C_DK
C_V = C_HEADS * C_DV
GLA_RANK = 16
GLA_NORMALIZER = 16.0
D_HEADS, D_DK, D_DV = 4, 64, 128
D_QK = D_HEADS * D_DK
D_V = D_HEADS * D_DV
BR_WIDTH = 512
FFN_HIDDEN = 2816
IN_SPLITS = (A_WIDTH, B_QK, B_QK, B_V, B_V, B_HEADS, B_HEADS, C_QK, C_QK, C_V, C_V, GLA_RANK,
             D_QK, D_QK, D_V, D_V, D_HEADS, D_HEADS)

LANES = 128
Z_WIDTH = 5632
Z_A, Z_BQ, Z_BK, Z_BV, Z_BG = 0, 512, 1024, 1536, 2048
Z_CQ, Z_CK, Z_CV, Z_CG = 2560, 2816, 3072, 3584
Z_DQ, Z_DK, Z_DV, Z_DO = 4096, 4352, 4608, 5120
S_BA, S_BB, S_DI, S_DF, S_LR = 0, 4, 8, 12, 16

ROW_TILE = 256
TIME_TILE = 256
FFN_CHUNK = 1408
VMEM_LIMIT = 56 * 1024 * 1024


def _bdot(a, b):
    return jnp.dot(a.astype(BF16), b.astype(BF16), preferred_element_type=F32)


def _bdot_nt(a, b):
    return lax.dot_general(a.astype(BF16), b.astype(BF16), (((1,), (1,)), ((), ())), preferred_element_type=F32)


def _bdot_tn(a, b):
    return lax.dot_general(a.astype(BF16), b.astype(BF16), (((0,), (0,)), ((), ())), preferred_element_type=F32)


def _sigmoid(x):
    return 1.0 / (1.0 + jnp.exp(-x))


def _softplus(x):
    return jnp.maximum(x, 0.0) + jnp.log(1.0 + jnp.exp(-jnp.abs(x)))


def _rms(x, gain):
    return x * lax.rsqrt(jnp.mean(x * x, axis=-1, keepdims=True) + EPS) * gain


def _tri(c, kind):
    r = lax.broadcasted_iota(jnp.int32, (c, c), 0)
    k = lax.broadcasted_iota(jnp.int32, (c, c), 1)
    if kind == "lower":
        return r >= k
    if kind == "strict":
        return r > k
    if kind == "upper":
        return r <= k
    return r == k


def _split_f32(x):
    hi = x.astype(BF16).astype(F32)
    return hi, x - hi


def _packed_consts(chunk):
    rep = LANES // chunk
    row = lax.broadcasted_iota(jnp.int32, (chunk, LANES), 0)
    lane = lax.broadcasted_iota(jnp.int32, (chunk, LANES), 1)
    col = lane & (chunk - 1)
    even = ((lane // chunk) & 1) == 0
    return rep, row, col, even


def _pack_lhs(hi, lo, even, rep):
    xx = jnp.where(even, hi, lo).astype(BF16)
    return xx if rep == 4 else jnp.concatenate([xx, xx], axis=1)


def _pack_rhs(hi, lo):
    hi, lo = hi.astype(BF16), lo.astype(BF16)
    return jnp.concatenate([hi, hi, lo, lo], axis=0)


def _resident(shape):
    nd = len(shape)
    return pl.BlockSpec(shape, lambda *_: (0,) * nd, pipeline_mode=pl.Buffered(1))


def _proj_kernel(x_ref, g_ref, wz_ref, ws_ref, z_ref, s_ref):
    hb = _rms(x_ref[...], g_ref[...]).astype(BF16)
    step = 512
    for c0 in range(0, Z_WIDTH, step):
        z_ref[:, c0:c0 + step] = jnp.dot(hb, wz_ref[:, c0:c0 + step], preferred_element_type=F32).astype(BF16)
    s_ref[...] = jnp.dot(hb, ws_ref[...], preferred_element_type=F32)


def _project(x2d, gain, wz, ws):
    n = x2d.shape[0]
    r = min(ROW_TILE, n)
    return pl.pallas_call(
        _proj_kernel,
        grid=(n // r,),
        in_specs=[pl.BlockSpec((r, D_MODEL), lambda i: (i, 0)),
                  _resident((1, D_MODEL)), _resident((D_MODEL, Z_WIDTH)), _resident((D_MODEL, LANES))],
        out_specs=[pl.BlockSpec((r, Z_WIDTH), lambda i: (i, 0)), pl.BlockSpec((r, LANES), lambda i: (i, 0))],
        out_shape=[jax.ShapeDtypeStruct((n, Z_WIDTH), BF16), jax.ShapeDtypeStruct((n, LANES), F32)],
        compiler_params=pltpu.CompilerParams(dimension_semantics=("parallel",), vmem_limit_bytes=VMEM_LIMIT),
        name="proj",
    )(x2d, gain, wz, ws)


def _pool_kernel(u_ref, buf_ref, pw_ref, sc_ref, y_ref, nbuf_ref, full_ref, *, tt, pos0):
    t = pl.program_id(1)
    hdr = 16

    @pl.when(t == 0)
    def _():
        full_ref[1:hdr, :] = buf_ref[0]

    @pl.when(t > 0)
    def _():
        full_ref[1:hdr, :] = full_ref[tt + 1:tt + hdr, :]

    full_ref[hdr:hdr + tt, :] = u_ref[0].astype(F32)
    nbuf_ref[0] = full_ref[tt + 1:tt + hdr, :]
    pos = pos0 + t * tt + lax.broadcasted_iota(jnp.int32, (tt, 1), 0)
    for gi, w in enumerate(POOL_WINDOWS):
        cols = slice(gi * A_GROUP, (gi + 1) * A_GROUP)
        u = full_ref[hdr:hdr + tt, cols]
        acc = u
        for j in range(1, w):
            acc = acc + full_ref[hdr - j:hdr - j + tt, cols]
        cnt = jnp.minimum(pos + 1, w).astype(F32)
        diff = acc / cnt - u
        y = _bdot(diff, pw_ref[gi]) * sc_ref[:, cols]
        y_ref[0, :, cols] = y.astype(BF16)


def _pool_mixer(z3, buf, pool_w, scale, pos0):
    b, t_len, _ = z3.shape
    tt = min(TIME_TILE, t_len)
    return pl.pallas_call(
        functools.partial(_pool_kernel, tt=tt, pos0=pos0),
        grid=(b, t_len // tt),
        in_specs=[pl.BlockSpec((1, tt, A_WIDTH), lambda i, j: (i, j, Z_A // A_WIDTH)),
                  pl.BlockSpec((1, POOL_BUF, A_WIDTH), lambda i, j: (i, 0, 0)),
                  _resident((4, A_GROUP, A_GROUP)), _resident((1, A_WIDTH))],
        out_specs=[pl.BlockSpec((1, tt, A_WIDTH), lambda i, j: (i, j, 0)),
                   pl.BlockSpec((1, POOL_BUF, A_WIDTH), lambda i, j: (i, 0, 0))],
        out_shape=[jax.ShapeDtypeStruct((b, t_len, A_WIDTH), BF16),
                   jax.ShapeDtypeStruct((b, POOL_BUF, A_WIDTH), F32)],
        scratch_shapes=[pltpu.VMEM((16 + tt, A_WIDTH), F32)],
        compiler_params=pltpu.CompilerParams(dimension_semantics=("parallel", "arbitrary"),
                                             vmem_limit_bytes=VMEM_LIMIT),
        name="pool",
    )(z3, buf, pool_w, scale)


def _gdn_kernel(q_ref, k_ref, v_ref, g_ref, sm_ref, cbuf_ref, s_in_ref, cw_ref, alog_ref, dt_ref, nrm_ref,
                y_ref, ncbuf_ref, s_ref, full_ref, *, tt, chunk):
    t = pl.program_id(1)
    hdr = 8

    @pl.when(t == 0)
    def _():
        full_ref[hdr - 3:hdr, :] = cbuf_ref[0]
        s_ref[0] = s_in_ref[0]

    @pl.when(t > 0)
    def _():
        full_ref[hdr - 3:hdr, :] = full_ref[tt + hdr - 3:tt + hdr, :]

    full_ref[hdr:hdr + tt, 0:B_QK] = q_ref[0].astype(F32)
    full_ref[hdr:hdr + tt, B_QK:2 * B_QK] = k_ref[0].astype(F32)
    full_ref[hdr:hdr + tt, 2 * B_QK:] = v_ref[0].astype(F32)
    ncbuf_ref[0] = full_ref[tt + hdr - 3:tt + hdr, :]

    rep, row, col, even = _packed_consts(chunk)
    causal = row >= col
    strict = row > col
    eye = (row == col).astype(F32)
    tril_pack = jnp.concatenate([causal[:, :chunk].astype(F32).astype(BF16)] * 2, axis=1)
    triu_pack = jnp.concatenate([(row <= col).astype(F32).astype(BF16)] * 2, axis=0)
    heads = range(B_HEADS)
    n_steps = int(math.log2(chunk)) - 1

    def conv(r0, c0):
        acc = full_ref[r0 + hdr - 3:r0 + hdr - 3 + chunk, c0:c0 + LANES] * cw_ref[0:1, c0:c0 + LANES]
        for j in range(1, CONV_W):
            acc = acc + full_ref[r0 + hdr - 3 + j:r0 + hdr - 3 + j + chunk, c0:c0 + LANES] * cw_ref[j:j + 1, c0:c0 + LANES]
        return acc * _sigmoid(acc)

    for c in range(tt // chunk):
        r0 = c * chunk
        sm = sm_ref[0, r0:r0 + chunk, :]
        log_alpha = alog_ref[...] * _softplus(sm + dt_ref[...])
        beta = _sigmoid(sm)
        la_hi, la_lo = _split_f32(log_alpha)
        la_pack = jnp.concatenate([la_hi.astype(BF16), la_lo.astype(BF16)], axis=0)
        g_col = jnp.dot(tril_pack, la_pack, preferred_element_type=F32)
        g_row = lax.dot_general(la_pack, triu_pack, (((0,), (0,)), ((), ())),
                                preferred_element_type=F32)
        q = [conv(r0, h * B_DK) for h in heads]
        k = [conv(r0, B_QK + h * B_DK) for h in heads]
        v = [conv(r0, 2 * B_QK + h * B_DV) for h in heads]
        q = [x * lax.rsqrt(jnp.sum(x * x, axis=-1, keepdims=True) + EPS) * (B_DK ** -0.5) for x in q]
        k = [x * lax.rsqrt(jnp.sum(x * x, axis=-1, keepdims=True) + EPS) for x in k]
        gi = [g_col[:, S_BA + h:S_BA + h + 1] for h in heads]
        gj = [g_row[S_BA + h:S_BA + h + 1, :] for h in heads]
        bt = [beta[:, S_BB + h:S_BB + h + 1] for h in heads]
        decay = [jnp.exp(jnp.where(causal, gi[h] - gj[h], -jnp.inf)) for h in heads]
        kb = [k[h] * bt[h] for h in heads]
        eg = [jnp.exp(gi[h]) for h in heads]
        low = [jnp.where(strict, _bdot_nt(kb[h], jnp.concatenate([k[h]] * rep, axis=0)) * decay[h], 0.0)
               for h in heads]
        lsp = [_split_f32(low[h]) for h in heads]
        lhs = [_pack_lhs(lsp[h][0], lsp[h][1], even, rep) for h in heads]
        m = [jnp.dot(lhs[h], _pack_rhs(*lsp[h]), preferred_element_type=F32) for h in heads]
        tinv = [eye - low[h] for h in heads]
        for step in range(n_steps):
            last = step == n_steps - 1
            msp = [_split_f32(m[h]) for h in heads]
            tsp = [_split_f32(tinv[h]) for h in heads]
            lhs = [_pack_lhs(msp[h][0], msp[h][1], even, rep) for h in heads]
            if last:
                rhs = [_pack_rhs(*tsp[h]) for h in heads]
            else:
                rhs = [_pack_rhs(jnp.concatenate([msp[h][0], tsp[h][0]], axis=1),
                                 jnp.concatenate([msp[h][1], tsp[h][1]], axis=1)) for h in heads]
            out = [jnp.dot(lhs[h], rhs[h], preferred_element_type=F32) for h in heads]
            if last:
                tinv = [tinv[h] + out[h] for h in heads]
            else:
                m = [out[h][:, :LANES] for h in heads]
                tinv = [tinv[h] + out[h][:, LANES:] for h in heads]
        tsp = [_split_f32(tinv[h]) for h in heads]
        lhs = [_pack_lhs(tsp[h][0], tsp[h][1], even, rep) for h in heads]
        rsp = [_split_f32(jnp.concatenate([v[h] * bt[h], kb[h] * eg[h]], axis=-1)) for h in heads]
        sol = [jnp.dot(lhs[h], _pack_rhs(*rsp[h]), preferred_element_type=F32) for h in heads]
        attn = [jnp.where(causal[:, :chunk], _bdot_nt(q[h], k[h]) * decay[h][:, :chunk], 0.0) for h in heads]
        g_last = [gi[h][chunk - 1:chunk, :] for h in heads]
        k_tail = [k[h] * jnp.exp(g_last[h] - gi[h]) for h in heads]
        q_dec = [q[h] * eg[h] for h in heads]
        s = [s_ref[0, h] for h in heads]
        v_new = [sol[h][:, :B_DV] - _bdot(sol[h][:, B_DV:], s[h]) for h in heads]
        o = [_bdot(q_dec[h], s[h]) + _bdot(attn[h], v_new[h]) for h in heads]
        for h in heads:
            s_ref[0, h] = s[h] * jnp.exp(g_last[h]) + _bdot_tn(k_tail[h], v_new[h])
        for h in heads:
            gate = g_ref[0, r0:r0 + chunk, h * B_DV:(h + 1) * B_DV].astype(F32)
            y = _rms(o[h], nrm_ref[...]) * (gate * _sigmoid(gate))
            y_ref[0, r0:r0 + chunk, h * B_DV:(h + 1) * B_DV] = y.astype(BF16)


def _gdn_mixer(z3, sm3, cbuf, s0, conv_w, alog_row, dt_row, norm_row, chunk):
    b, t_len, _ = z3.shape
    tt = min(TIME_TILE, t_len)
    wide = lambda off: pl.BlockSpec((1, tt, 512), lambda i, j: (i, j, off // 512))
    return pl.pallas_call(
        functools.partial(_gdn_kernel, tt=tt, chunk=chunk),
        grid=(b, t_len // tt),
        in_specs=[wide(Z_BQ), wide(Z_BK), wide(Z_BV), wide(Z_BG),
                  pl.BlockSpec((1, tt, LANES), lambda i, j: (i, j, 0)),
                  pl.BlockSpec((1, CONV_W - 1, B_CONV_CH), lambda i, j: (i, 0, 0)),
                  pl.BlockSpec((1, B_HEADS, B_DK, B_DV), lambda i, j: (i, 0, 0, 0)),
                  _resident((CONV_W, B_CONV_CH)), _resident((1, LANES)), _resident((1, LANES)),
                  _resident((1, B_DV))],
        out_specs=[pl.BlockSpec((1, tt, B_V), lambda i, j: (i, j, 0)),
                   pl.BlockSpec((1, CONV_W - 1, B_CONV_CH), lambda i, j: (i, 0, 0)),
                   pl.BlockSpec((1, B_HEADS, B_DK, B_DV), lambda i, j: (i, 0, 0, 0))],
        out_shape=[jax.ShapeDtypeStruct((b, t_len, B_V), BF16),
                   jax.ShapeDtypeStruct((b, CONV_W - 1, B_CONV_CH), F32),
                   jax.ShapeDtypeStruct((b, B_HEADS, B_DK, B_DV), F32)],
        scratch_shapes=[pltpu.VMEM((8 + tt, B_CONV_CH), F32)],
        compiler_params=pltpu.CompilerParams(dimension_semantics=("parallel", "arbitrary"),
                                             vmem_limit_bytes=VMEM_LIMIT),
        name="gdn",
    )(z3, z3, z3, z3, sm3, cbuf, s0, conv_w, alog_row, dt_row, norm_row)


def _gla_kernel(q_ref, k_ref, v_ref, g_ref, sm_ref, s_in_ref, wup_ref, bup_ref, nrm_ref,
                y_ref, s_ref, *, tt, chunk):
    t = pl.program_id(1)

    @pl.when(t == 0)
    def _():
        s_ref[0] = s_in_ref[0]

    causal = _tri(chunk, "lower")
    tril_pack = jnp.concatenate([causal.astype(F32).astype(BF16)] * 2, axis=1)
    ones = jnp.ones((2 * chunk, LANES), BF16)
    heads = range(C_HEADS)
    ks = [slice(h * C_DK, (h + 1) * C_DK) for h in heads]
    vs = [slice(h * C_DV, (h + 1) * C_DV) for h in heads]
    chunks = range(tt // chunk)
    rows = [slice(c * chunk, (c + 1) * chunk) for c in chunks]
    sm_sp = [_split_f32(sm_ref[0, rows[c], :]) for c in chunks]
    lr = [jnp.dot(jnp.concatenate([sm_sp[c][0], sm_sp[c][1], sm_sp[c][0]], axis=1).astype(BF16), wup_ref[...],
                  preferred_element_type=F32) + bup_ref[...] for c in chunks]
    log_a = [-_softplus(-lr[c]) / GLA_NORMALIZER for c in chunks]
    la_sp = [_split_f32(log_a[c]) for c in chunks]
    la_pack = [jnp.concatenate([la_sp[c][0].astype(BF16), la_sp[c][1].astype(BF16)], axis=0) for c in chunks]
    bcum = [jnp.dot(tril_pack, la_pack[c], preferred_element_type=F32) for c in chunks]
    d_col = [jnp.exp(lax.dot_general(la_pack[c], ones, (((0,), (0,)), ((), ())), preferred_element_type=F32))
             for c in chunks]
    mid = [bcum[c][chunk // 2:chunk // 2 + 1, :] for c in chunks]
    last = [bcum[c][chunk - 1:chunk, :] for c in chunks]
    qf = [q_ref[0, rows[c], :].astype(F32) * (C_DK ** -0.5) for c in chunks]
    kf = [k_ref[0, rows[c], :].astype(F32) for c in chunks]
    q_in = [qf[c] * jnp.exp(bcum[c] - mid[c]) for c in chunks]
    k_in = [kf[c] * jnp.exp(mid[c] - bcum[c]) for c in chunks]
    q_x = [qf[c] * jnp.exp(bcum[c]) for c in chunks]
    k_t = [kf[c] * jnp.exp(last[c] - bcum[c]) for c in chunks]
    v = [[v_ref[0, rows[c], vs[h]] for h in heads] for c in chunks]
    attn = [[jnp.where(causal, _bdot_nt(q_in[c][:, ks[h]], k_in[c][:, ks[h]]), 0.0) for h in heads] for c in chunks]
    o_in = [[_bdot(attn[c][h], v[c][h]) for h in heads] for c in chunks]
    upd = [[_bdot_tn(k_t[c][:, ks[h]], v[c][h]) for h in heads] for c in chunks]
    s = [[s_ref[0, h] for h in heads]]
    for c in chunks:
        s.append([s[c][h] * d_col[c][ks[h], :] + upd[c][h] for h in heads])
    for h in heads:
        s_ref[0, h] = s[-1][h]
    o = [[_bdot(q_x[c][:, ks[h]], s[c][h]) + o_in[c][h] for h in heads] for c in chunks]
    for c in chunks:
        for h in heads:
            gate = g_ref[0, rows[c], vs[h]].astype(F32)
            y = _rms(o[c][h], nrm_ref[...]) * (gate * _sigmoid(gate))
            y_ref[0, rows[c], vs[h]] = y.astype(BF16)


def _gla_mixer(z3, sm3, s0, wup_pad, bup_row, norm_row, chunk):
    b, t_len, _ = z3.shape
    tt = min(TIME_TILE, t_len)
    blk = lambda w, off: pl.BlockSpec((1, tt, w), lambda i, j: (i, j, off // w))
    return pl.pallas_call(
        functools.partial(_gla_kernel, tt=tt, chunk=chunk),
        grid=(b, t_len // tt),
        in_specs=[blk(C_QK, Z_CQ), blk(C_QK, Z_CK), blk(C_V, Z_CV), blk(C_V, Z_CG),
                  pl.BlockSpec((1, tt, LANES), lambda i, j: (i, j, 0)),
                  pl.BlockSpec((1, C_HEADS, C_DK, C_DV), lambda i, j: (i, 0, 0, 0)),
                  _resident((3 * LANES, C_QK)), _resident((1, C_QK)), _resident((1, C_DV))],
        out_specs=[pl.BlockSpec((1, tt, C_V), lambda i, j: (i, j, 0)),
                   pl.BlockSpec((1, C_HEADS, C_DK, C_DV), lambda i, j: (i, 0, 0, 0))],
        out_shape=[jax.ShapeDtypeStruct((b, t_len, C_V), BF16),
                   jax.ShapeDtypeStruct((b, C_HEADS, C_DK, C_DV), F32)],
        compiler_params=pltpu.CompilerParams(dimension_semantics=("parallel", "arbitrary"),
                                             vmem_limit_bytes=VMEM_LIMIT),
        name="gla",
    )(z3, z3, z3, z3, sm3, s0, wup_pad, bup_row, norm_row)


def _mlstm_kernel(q_ref, k_ref, v_ref, og_ref, sm_ref, cn_in_ref, m_in_ref, bi_ref, bf_ref, nrm_ref,
                  y_ref, cn_ref, m_ref, *, tt, chunk):
    t = pl.program_id(1)

    @pl.when(t == 0)
    def _():
        cn_ref[0] = cn_in_ref[0]
        m_ref[0] = m_in_ref[0]

    causal = _tri(chunk, "lower")
    tril_pack = jnp.concatenate([causal.astype(F32).astype(BF16)] * 2, axis=1)
    triu_pack = jnp.concatenate([_tri(chunk, "upper").astype(F32).astype(BF16)] * 2, axis=0)
    eye_pack = jnp.concatenate([_tri(chunk, "eye").astype(F32).astype(BF16)] * 2, axis=0)
    ones = jnp.ones((chunk, LANES), BF16)
    heads = range(D_HEADS)
    ks = [slice(h * D_DK, (h + 1) * D_DK) for h in heads]
    vs = [slice(h * D_DV, (h + 1) * D_DV) for h in heads]
    tn = (((0,), (0,)), ((), ()))
    chunks = range(tt // chunk)
    rows = [slice(c * chunk, (c + 1) * chunk) for c in chunks]
    sm = [sm_ref[0, rows[c], :] for c in chunks]
    i_pre = [sm[c] + bi_ref[...] for c in chunks]
    log_f = [-_softplus(-(sm[c] + bf_ref[...])) for c in chunks]
    lf_sp = [_split_f32(log_f[c]) for c in chunks]
    lf_pack = [jnp.concatenate([lf_sp[c][0].astype(BF16), lf_sp[c][1].astype(BF16)], axis=0) for c in chunks]
    ip_sp = [_split_f32(i_pre[c]) for c in chunks]
    ip_pack = [jnp.concatenate([ip_sp[c][0].astype(BF16), ip_sp[c][1].astype(BF16)], axis=0) for c in chunks]
    b_col = [jnp.dot(tril_pack, lf_pack[c], preferred_element_type=F32) for c in chunks]
    b_row = [lax.dot_general(lf_pack[c], triu_pack, tn, preferred_element_type=F32) for c in chunks]
    i_row = [lax.dot_general(ip_pack[c], eye_pack, tn, preferred_element_type=F32) for c in chunks]
    qf = [q_ref[0, rows[c], :].astype(F32) * (D_DK ** -0.5) for c in chunks]
    kf = [k_ref[0, rows[c], :].astype(F32) for c in chunks]
    q = [[qf[c][:, ks[h]] for h in heads] for c in chunks]
    k = [[kf[c][:, ks[h]] for h in heads] for c in chunks]
    v = [[v_ref[0, rows[c], vs[h]] for h in heads] for c in chunks]
    bi = [[b_col[c][:, S_DF + h:S_DF + h + 1] for h in heads] for c in chunks]
    ii = [[i_pre[c][:, S_DI + h:S_DI + h + 1] for h in heads] for c in chunks]
    log_w = [[jnp.where(causal, bi[c][h] - b_row[c][S_DF + h:S_DF + h + 1, :] + i_row[c][S_DI + h:S_DI + h + 1, :],
                        -jnp.inf) for h in heads] for c in chunks]
    lw_max = [[jnp.max(log_w[c][h], axis=-1, keepdims=True) for h in heads] for c in chunks]
    qk = [[_bdot_nt(q[c][h], k[c][h]) for h in heads] for c in chunks]
    b_last = [[bi[c][h][chunk - 1:chunk, :] for h in heads] for c in chunks]
    m_s = [[m_ref[0, h:h + 1, 0:1] for h in heads]]
    for c in chunks:
        m_s.append([jnp.maximum(b_last[c][h] + m_s[c][h], lw_max[c][h][chunk - 1:chunk, :]) for h in heads])
    for h in heads:
        m_ref[0, h:h + 1, :] = jnp.broadcast_to(m_s[-1][h], (1, LANES))
    log_inter = [[bi[c][h] + m_s[c][h] for h in heads] for c in chunks]
    m_t = [[jnp.maximum(log_inter[c][h], lw_max[c][h]) for h in heads] for c in chunks]
    w_inter = [[jnp.exp(log_inter[c][h] - m_t[c][h]) for h in heads] for c in chunks]
    p_w = [[jnp.exp(log_w[c][h] - m_t[c][h]) * qk[c][h] for h in heads] for c in chunks]
    pv = [[_bdot(p_w[c][h], v[c][h]) for h in heads] for c in chunks]
    w_k = [[jnp.exp(b_last[c][h] - bi[c][h] + ii[c][h] - m_s[c + 1][h]) for h in heads] for c in chunks]
    upd = [[_bdot_tn(k[c][h] * w_k[c][h], jnp.concatenate([v[c][h], ones], axis=-1)) for h in heads] for c in chunks]
    cn = [[cn_ref[0, h] for h in heads]]
    for c in chunks:
        cn.append([jnp.exp(b_last[c][h] + m_s[c][h] - m_s[c + 1][h]) * cn[c][h] + upd[c][h] for h in heads])
    for h in heads:
        cn_ref[0, h] = cn[-1][h]
    qcn = [[_bdot(q[c][h], cn[c][h]) for h in heads] for c in chunks]
    for c in chunks:
        for h in heads:
            num = w_inter[c][h] * qcn[c][h][:, :D_DV] + pv[c][h]
            den = w_inter[c][h] * qcn[c][h][:, D_DV:D_DV + 1] + jnp.sum(p_w[c][h], axis=-1, keepdims=True)
            hh = num / jnp.maximum(jnp.abs(den), jnp.exp(-m_t[c][h]))
            gate = og_ref[0, rows[c], vs[h]].astype(F32)
            y = _sigmoid(gate) * _rms(hh, nrm_ref[...])
            y_ref[0, rows[c], vs[h]] = y.astype(BF16)


def _mlstm_mixer(z3, sm3, cn0, m0, bi_row, bf_row, norm_row, chunk):
    b, t_len, _ = z3.shape
    tt = min(TIME_TILE, t_len)
    blk = lambda w, off: pl.BlockSpec((1, tt, w), lambda i, j: (i, j, off // w))
    return pl.pallas_call(
        functools.partial(_mlstm_kernel, tt=tt, chunk=chunk),
        grid=(b, t_len // tt),
        in_specs=[blk(D_QK, Z_DQ), blk(D_QK, Z_DK), blk(D_V, Z_DV), blk(D_V, Z_DO),
                  pl.BlockSpec((1, tt, LANES), lambda i, j: (i, j, 0)),
                  pl.BlockSpec((1, D_HEADS, D_DK, 2 * D_DV), lambda i, j: (i, 0, 0, 0)),
                  pl.BlockSpec((1, 8, LANES), lambda i, j: (i, 0, 0)),
                  _resident((1, LANES)), _resident((1, LANES)), _resident((1, D_DV))],
        out_specs=[pl.BlockSpec((1, tt, D_V), lambda i, j: (i, j, 0)),
                   pl.BlockSpec((1, D_HEADS, D_DK, 2 * D_DV), lambda i, j: (i, 0, 0, 0)),
                   pl.BlockSpec((1, 8, LANES), lambda i, j: (i, 0, 0))],
        out_shape=[jax.ShapeDtypeStruct((b, t_len, D_V), BF16),
                   jax.ShapeDtypeStruct((b, D_HEADS, D_DK, 2 * D_DV), F32),
                   jax.ShapeDtypeStruct((b, 8, LANES), F32)],
        compiler_params=pltpu.CompilerParams(dimension_semantics=("parallel", "arbitrary"),
                                             vmem_limit_bytes=VMEM_LIMIT),
        name="mlstm",
    )(z3, z3, z3, z3, sm3, cn0, m0, bi_row, bf_row, norm_row)


def _merge_ffn_kernel(x_ref, ya_ref, yb_ref, yc_ref, yd_ref, gmix_ref, wgate_ref, bgate_ref, wbr_ref, wout_ref,
                      gffn_ref, wfg_ref, wfu_ref, wfd_ref, gfin_ref, o_ref, *, final):
    x = x_ref[...]
    hb = _rms(x, gmix_ref[...]).astype(BF16)
    merged = None
    for i, y_ref in enumerate((ya_ref, yb_ref, yc_ref, yd_ref)):
        gate = _sigmoid(jnp.dot(hb, wgate_ref[i], preferred_element_type=F32) + bgate_ref[i])
        term = gate * jnp.dot(y_ref[...], wbr_ref[i], preferred_element_type=F32)
        merged = term if merged is None else merged + term
    x = x + jnp.dot(merged.astype(BF16), wout_ref[...], preferred_element_type=F32)
    h2 = _rms(x, gffn_ref[...]).astype(BF16)
    for c0 in range(0, FFN_HIDDEN, FFN_CHUNK):
        a = jnp.dot(h2, wfg_ref[:, c0:c0 + FFN_CHUNK], preferred_element_type=F32)
        u = jnp.dot(h2, wfu_ref[:, c0:c0 + FFN_CHUNK], preferred_element_type=F32)
        f = (a * _sigmoid(a) * u).astype(BF16)
        x = x + jnp.dot(f, wfd_ref[c0:c0 + FFN_CHUNK, :], preferred_element_type=F32)
    if final:
        x = _rms(x, gfin_ref[...])
    o_ref[...] = x


def _merge_ffn(x2d, ys, gmix, wgate, bgate, wbr, wout, gffn, wfg, wfu, wfd, gfin, final):
    n = x2d.shape[0]
    r = min(ROW_TILE, n)
    row = lambda w: pl.BlockSpec((r, w), lambda i: (i, 0))
    return pl.pallas_call(
        functools.partial(_merge_ffn_kernel, final=final),
        grid=(n // r,),
        in_specs=[row(D_MODEL), row(BR_WIDTH), row(BR_WIDTH), row(BR_WIDTH), row(BR_WIDTH),
                  _resident((1, D_MODEL)), _resident((4, D_MODEL, D_MODEL)), _resident((4, 1, D_MODEL)),
                  _resident((4, BR_WIDTH, D_MODEL)), _resident((D_MODEL, D_MODEL)), _resident((1, D_MODEL)),
                  _resident((D_MODEL, FFN_HIDDEN)), _resident((D_MODEL, FFN_HIDDEN)),
                  _resident((FFN_HIDDEN, D_MODEL)), _resident((1, D_MODEL))],
        out_specs=row(D_MODEL),
        out_shape=jax.ShapeDtypeStruct((n, D_MODEL), F32),
        compiler_params=pltpu.CompilerParams(dimension_semantics=("parallel",), vmem_limit_bytes=VMEM_LIMIT),
        name="merge_ffn",
    )(x2d, *ys, gmix, wgate, bgate, wbr, wout, gffn, wfg, wfu, wfd, gfin)


def _lane_row(vec, off):
    return jnp.zeros((1, LANES), F32).at[0, off:off + vec.shape[0]].set(vec.astype(F32))


def _layer_params(l, p):
    w_in = p["w_in"][l]
    offs = [0]
    for s in IN_SPLITS:
        offs.append(offs[-1] + s)
    piece = lambda i: w_in[:, offs[i]:offs[i + 1]]
    wide_ids = (0, 1, 2, 3, 4, 7, 8, 9, 10, 12, 13, 14, 15)
    wz = jnp.concatenate([piece(i) for i in wide_ids], axis=1).astype(BF16)
    small = jnp.concatenate([piece(5), piece(6), piece(16), piece(17), piece(11)], axis=1)
    ws = jnp.zeros((D_MODEL, LANES), F32).at[:, :small.shape[1]].set(small).astype(BF16)
    wup = jnp.zeros((LANES, C_QK), F32).at[S_LR:S_LR + GLA_RANK].set(p["gla_w_up"][l].astype(F32))
    wup_hi = wup.astype(BF16)
    wup_lo = (wup - wup_hi.astype(F32)).astype(BF16)
    return dict(
        gmix=p["norm_mix"][l].reshape(1, D_MODEL), wz=wz, ws=ws,
        pool_w=p["pool_w"][l].astype(BF16), pool_scale=p["pool_scale"][l].reshape(1, A_WIDTH),
        conv_w=p["gdn_conv_w"][l],
        alog_row=_lane_row(-jnp.exp(p["gdn_a_log"][l].astype(F32)), S_BA),
        dt_row=_lane_row(p["gdn_dt_bias"][l], S_BA),
        gdn_norm=p["gdn_norm"][l].reshape(1, B_DV),
        wup=jnp.concatenate([wup_hi, wup_hi, wup_lo], axis=0),
        bup=p["gla_b_up"][l].reshape(1, C_QK), gla_norm=p["gla_norm"][l].reshape(1, C_DV),
        bi_row=_lane_row(p["mlstm_b_i"][l], S_DI), bf_row=_lane_row(p["mlstm_b_f"][l], S_DF),
        mlstm_norm=p["mlstm_norm"][l].reshape(1, D_DV),
        wgate=p["w_gate"][l].astype(BF16), bgate=p["b_gate"][l].reshape(4, 1, D_MODEL),
        wbr=p["w_branch"][l].astype(BF16), wout=p["w_out"][l].astype(BF16),
        gffn=p["norm_ffn"][l].reshape(1, D_MODEL),
        wfg=p["w_ffn_gate"][l].astype(BF16), wfu=p["w_ffn_up"][l].astype(BF16), wfd=p["w_ffn_down"][l].astype(BF16),
    )


def _layer(x, st, lp, gfin, pos0, chunk, final):
    b, t_len, _ = x.shape
    a_buf, conv_buf, gdn_s, gla_s, ml_c, ml_n, ml_m = st
    x2d = x.reshape(b * t_len, D_MODEL)
    z, sm = _project(x2d, lp["gmix"], lp["wz"], lp["ws"])
    z3 = z.reshape(b, t_len, Z_WIDTH)
    sm3 = sm.reshape(b, t_len, LANES)
    y_a, a_new = _pool_mixer(z3, a_buf, lp["pool_w"], lp["pool_scale"], pos0)
    y_b, conv_new, gdn_new = _gdn_mixer(z3, sm3, conv_buf, gdn_s, lp["conv_w"], lp["alog_row"], lp["dt_row"],
                                        lp["gdn_norm"], chunk)
    y_c, gla_new = _gla_mixer(z3, sm3, gla_s, lp["wup"], lp["bup"], lp["gla_norm"], chunk)
    cn0 = jnp.concatenate([ml_c, jnp.broadcast_to(ml_n[..., None], ml_c.shape)], axis=-1)
    m0 = jnp.zeros((b, 8, LANES), F32).at[:, :D_HEADS, :].set(jnp.broadcast_to(ml_m[..., None], (b, D_HEADS, LANES)))
    y_d, cn_new, m_new = _mlstm_mixer(z3, sm3, cn0, m0, lp["bi_row"], lp["bf_row"], lp["mlstm_norm"], chunk)
    ys = [y.reshape(b * t_len, BR_WIDTH) for y in (y_a, y_b, y_c, y_d)]
    x_new = _merge_ffn(x2d, ys, lp["gmix"], lp["wgate"], lp["bgate"], lp["wbr"], lp["wout"], lp["gffn"],
                       lp["wfg"], lp["wfu"], lp["wfd"], gfin, final)
    new_st = (a_new, conv_new, gdn_new, gla_new, cn_new[..., :D_DV], cn_new[..., D_DV], m_new[:, :D_HEADS, 0])
    return x_new.reshape(b, t_len, D_MODEL), new_st


def kernel(x_prompt, x_sample, state_a_pool, state_b_conv, state_b_S, state_c_S, state_d_C, state_d_n, state_d_m,
           norm_mix, w_in, pool_w, pool_scale, gdn_conv_w, gdn_a_log, gdn_dt_bias, gdn_norm,
           gla_w_up, gla_b_up, gla_norm, mlstm_b_i, mlstm_b_f, mlstm_norm,
           w_branch, w_gate, b_gate, w_out, norm_ffn, w_ffn_gate, w_ffn_up, w_ffn_down, norm_final):
    p = dict(norm_mix=norm_mix, w_in=w_in, pool_w=pool_w, pool_scale=pool_scale, gdn_conv_w=gdn_conv_w,
             gdn_a_log=gdn_a_log, gdn_dt_bias=gdn_dt_bias, gdn_norm=gdn_norm, gla_w_up=gla_w_up, gla_b_up=gla_b_up,
             gla_norm=gla_norm, mlstm_b_i=mlstm_b_i, mlstm_b_f=mlstm_b_f, mlstm_norm=mlstm_norm,
             w_branch=w_branch, w_gate=w_gate, b_gate=b_gate, w_out=w_out, norm_ffn=norm_ffn,
             w_ffn_gate=w_ffn_gate, w_ffn_up=w_ffn_up, w_ffn_down=w_ffn_down)
    bp = x_prompt.shape[0]
    zero_p = (jnp.zeros((bp, POOL_BUF, A_WIDTH), F32), jnp.zeros((bp, CONV_W - 1, B_CONV_CH), F32),
              jnp.zeros((bp, B_HEADS, B_DK, B_DV), F32), jnp.zeros((bp, C_HEADS, C_DK, C_DV), F32),
              jnp.zeros((bp, D_HEADS, D_DK, D_DV), F32), jnp.zeros((bp, D_HEADS, D_DK), F32),
              jnp.zeros((bp, D_HEADS), F32))
    gfin = norm_final.reshape(1, D_MODEL)
    yp, ys = x_prompt, x_sample
    new_p, new_s = [], []
    for l in range(DEPTH):
        lp = _layer_params(l, p)
        final = l == DEPTH - 1
        yp, sp = _layer(yp, zero_p, lp, gfin, 0, CHUNK, final)
        cache_l = (state_a_pool[l], state_b_conv[l], state_b_S[l], state_c_S[l],
                   state_d_C[l], state_d_n[l], state_d_m[l])
        ys, ss = _layer(ys, cache_l, lp, gfin, PAST_LEN, x_sample.shape[1], final)
        new_p.append(sp)
        new_s.append(ss)
    outs_p = [jnp.stack([s[i] for s in new_p]) for i in range(7)]
    outs_s = [jnp.stack([s[i] for s in new_s]) for i in range(7)]
    return (yp, ys, *outs_p, *outs_s)
```

```python
import functools
import math

import jax
import jax.numpy as jnp
from jax import lax
from jax.experimental import pallas as pl
from jax.experimental.pallas import tpu as pltpu

F32 = jnp.float32
BF16 = jnp.bfloat16

D_MODEL = 1024
DEPTH = 2
PAST_LEN = 2048
CHUNK = 64
EPS = 1e-6
POOL_WINDOWS = (2, 4, 8, 16)
A_WIDTH = 512
A_GROUP = 128
POOL_BUF = 15
B_HEADS, B_DK, B_DV = 4, 128, 128
B_QK = B_HEADS * B_DK
B_V = B_HEADS * B_DV
CONV_W = 4
B_CONV_CH = 2 * B_QK + B_V
C_HEADS, C_DK, C_DV = 4, 64, 128
C_QK = C_HEADS * C_DK
C_V = C_HEADS * C_DV
GLA_RANK = 16
GLA_NORMALIZER = 16.0
D_HEADS, D_DK, D_DV = 4, 64, 128
D_QK = D_HEADS * D_DK
D_V = D_HEADS * D_DV
BR_WIDTH = 512
FFN_HIDDEN = 2816
IN_SPLITS = (A_WIDTH, B_QK, B_QK, B_V, B_V, B_HEADS, B_HEADS, C_QK, C_QK, C_V, C_V, GLA_RANK,
             D_QK, D_QK, D_V, D_V, D_HEADS, D_HEADS)

LANES = 128
Z_WIDTH = 5632
Z_A, Z_BQ, Z_BK, Z_BV, Z_BG = 0, 512, 1024, 1536, 2048
Z_CQ, Z_CK, Z_CV, Z_CG = 2560, 2816, 3072, 3584
Z_DQ, Z_DK, Z_DV, Z_DO = 4096, 4352, 4608, 5120
S_BA, S_BB, S_DI, S_DF, S_LR = 0, 4, 8, 12, 16

ROW_TILE = 256
TIME_TILE = 256
FFN_CHUNK = 1408
VMEM_LIMIT = 56 * 1024 * 1024


def _bdot(a, b):
    return jnp.dot(a.astype(BF16), b.astype(BF16), preferred_element_type=F32)


def _bdot_nt(a, b):
    return lax.dot_general(a.astype(BF16), b.astype(BF16), (((1,), (1,)), ((), ())), preferred_element_type=F32)


def _bdot_tn(a, b):
    return lax.dot_general(a.astype(BF16), b.astype(BF16), (((0,), (0,)), ((), ())), preferred_element_type=F32)


def _sigmoid(x):
    return 1.0 / (1.0 + jnp.exp(-x))


def _softplus(x):
    return jnp.maximum(x, 0.0) + jnp.log(1.0 + jnp.exp(-jnp.abs(x)))


def _rms(x, gain):
    return x * lax.rsqrt(jnp.mean(x * x, axis=-1, keepdims=True) + EPS) * gain


def _tri(c, kind):
    r = lax.broadcasted_iota(jnp.int32, (c, c), 0)
    k = lax.broadcasted_iota(jnp.int32, (c, c), 1)
    if kind == "lower":
        return r >= k
    if kind == "strict":
        return r > k
    if kind == "upper":
        return r <= k
    return r == k


def _split_f32(x):
    hi = x.astype(BF16).astype(F32)
    return hi, x - hi


def _packed_consts(chunk):
    rep = LANES // chunk
    row = lax.broadcasted_iota(jnp.int32, (chunk, LANES), 0)
    lane = lax.broadcasted_iota(jnp.int32, (chunk, LANES), 1)
    col = lane & (chunk - 1)
    even = ((lane // chunk) & 1) == 0
    return rep, row, col, even


def _pack_lhs(hi, lo, even, rep):
    xx = jnp.where(even, hi, lo).astype(BF16)
    return xx if rep == 4 else jnp.concatenate([xx, xx], axis=1)


def _pack_rhs(hi, lo):
    hi, lo = hi.astype(BF16), lo.astype(BF16)
    return jnp.concatenate([hi, hi, lo, lo], axis=0)


def _resident(shape):
    nd = len(shape)
    return pl.BlockSpec(shape, lambda *_: (0,) * nd, pipeline_mode=pl.Buffered(1))


def _proj_kernel(x_ref, g_ref, wz_ref, ws_ref, z_ref, s_ref):
    hb = _rms(x_ref[...], g_ref[...]).astype(BF16)
    step = 512
    for c0 in range(0, Z_WIDTH, step):
        z_ref[:, c0:c0 + step] = jnp.dot(hb, wz_ref[:, c0:c0 + step], preferred_element_type=F32).astype(BF16)
    s_ref[...] = jnp.dot(hb, ws_ref[...], preferred_element_type=F32)


def _project(x2d, gain, wz, ws):
    n = x2d.shape[0]
    r = min(ROW_TILE, n)
    return pl.pallas_call(
        _proj_kernel,
        grid=(n // r,),
        in_specs=[pl.BlockSpec((r, D_MODEL), lambda i: (i, 0)),
                  _resident((1, D_MODEL)), _resident((D_MODEL, Z_WIDTH)), _resident((D_MODEL, LANES))],
        out_specs=[pl.BlockSpec((r, Z_WIDTH), lambda i: (i, 0)), pl.BlockSpec((r, LANES), lambda i: (i, 0))],
        out_shape=[jax.ShapeDtypeStruct((n, Z_WIDTH), BF16), jax.ShapeDtypeStruct((n, LANES), F32)],
        compiler_params=pltpu.CompilerParams(dimension_semantics=("parallel",), vmem_limit_bytes=VMEM_LIMIT),
        name="proj",
    )(x2d, gain, wz, ws)


def _pool_kernel(u_ref, buf_ref, pw_ref, sc_ref, y_ref, nbuf_ref, full_ref, *, tt, pos0):
    t = pl.program_id(1)
    hdr = 16

    @pl.when(t == 0)
    def _():
        full_ref[1:hdr, :] = buf_ref[0]

    @pl.when(t > 0)
    def _():
        full_ref[1:hdr, :] = full_ref[tt + 1:tt + hdr, :]

    full_ref[hdr:hdr + tt, :] = u_ref[0].astype(F32)
    nbuf_ref[0] = full_ref[tt + 1:tt + hdr, :]
    pos = pos0 + t * tt + lax.broadcasted_iota(jnp.int32, (tt, 1), 0)
    for gi, w in enumerate(POOL_WINDOWS):
        cols = slice(gi * A_GROUP, (gi + 1) * A_GROUP)
        u = full_ref[hdr:hdr + tt, cols]
        acc = u
        for j in range(1, w):
            acc = acc + full_ref[hdr - j:hdr - j + tt, cols]
        cnt = jnp.minimum(pos + 1, w).astype(F32)
        diff = acc / cnt - u
        y = _bdot(diff, pw_ref[gi]) * sc_ref[:, cols]
        y_ref[0, :, cols] = y.astype(BF16)


def _pool_mixer(z3, buf, pool_w, scale, pos0):
    b, t_len, _ = z3.shape
    tt = min(TIME_TILE, t_len)
    return pl.pallas_call(
        functools.partial(_pool_kernel, tt=tt, pos0=pos0),
        grid=(b, t_len // tt),
        in_specs=[pl.BlockSpec((1, tt, A_WIDTH), lambda i, j: (i, j, Z_A // A_WIDTH)),
                  pl.BlockSpec((1, POOL_BUF, A_WIDTH), lambda i, j: (i, 0, 0)),
                  _resident((4, A_GROUP, A_GROUP)), _resident((1, A_WIDTH))],
        out_specs=[pl.BlockSpec((1, tt, A_WIDTH), lambda i, j: (i, j, 0)),
                   pl.BlockSpec((1, POOL_BUF, A_WIDTH), lambda i, j: (i, 0, 0))],
        out_shape=[jax.ShapeDtypeStruct((b, t_len, A_WIDTH), BF16),
                   jax.ShapeDtypeStruct((b, POOL_BUF, A_WIDTH), F32)],
        scratch_shapes=[pltpu.VMEM((16 + tt, A_WIDTH), F32)],
        compiler_params=pltpu.CompilerParams(dimension_semantics=("parallel", "arbitrary"),
                                             vmem_limit_bytes=VMEM_LIMIT),
        name="pool",
    )(z3, buf, pool_w, scale)


def _gdn_kernel(q_ref, k_ref, v_ref, g_ref, sm_ref, cbuf_ref, s_in_ref, cw_ref, alog_ref, dt_ref, nrm_ref,
                selg_ref, selb_ref, y_ref, ncbuf_ref, s_ref, full_ref, *, tt, chunk):
    t = pl.program_id(1)
    hdr = 8

    @pl.when(t == 0)
    def _():
        full_ref[hdr - 3:hdr, :] = cbuf_ref[0]
        s_ref[0] = s_in_ref[0]

    @pl.when(t > 0)
    def _():
        full_ref[hdr - 3:hdr, :] = full_ref[tt + hdr - 3:tt + hdr, :]

    full_ref[hdr:hdr + tt, 0:B_QK] = q_ref[0].astype(F32)
    full_ref[hdr:hdr + tt, B_QK:2 * B_QK] = k_ref[0].astype(F32)
    full_ref[hdr:hdr + tt, 2 * B_QK:] = v_ref[0].astype(F32)
    ncbuf_ref[0] = full_ref[tt + hdr - 3:tt + hdr, :]

    rep, row, col, even = _packed_consts(chunk)
    causal = row >= col
    strict = row > col
    eye = (row == col).astype(F32)
    tril_pack = jnp.concatenate([causal[:, :chunk].astype(F32).astype(BF16)] * 2, axis=1)
    triu_pack = jnp.concatenate([(row <= col).astype(F32).astype(BF16)] * 2, axis=0)
    heads = range(B_HEADS)
    n_steps = int(math.log2(chunk)) - 1

    def conv(r0, c0):
        acc = full_ref[r0 + hdr - 3:r0 + hdr - 3 + chunk, c0:c0 + LANES] * cw_ref[0:1, c0:c0 + LANES]
        for j in range(1, CONV_W):
            acc = acc + full_ref[r0 + hdr - 3 + j:r0 + hdr - 3 + j + chunk, c0:c0 + LANES] * cw_ref[j:j + 1, c0:c0 + LANES]
        return acc * _sigmoid(acc)

    chunks = range(tt // chunk)
    ch = [(c, h) for c in chunks for h in heads]
    rows = [slice(c * chunk, (c + 1) * chunk) for c in chunks]
    ones = jnp.ones((LANES, LANES), BF16)
    sm = [sm_ref[0, rows[c], :] for c in chunks]
    log_alpha = [alog_ref[...] * _softplus(sm[c] + dt_ref[...]) for c in chunks]
    beta = [_sigmoid(sm[c]) for c in chunks]
    la_sp = [_split_f32(log_alpha[c]) for c in chunks]
    la_pack = [jnp.concatenate([la_sp[c][0].astype(BF16), la_sp[c][1].astype(BF16)], axis=0) for c in chunks]
    g_col = [jnp.dot(tril_pack, la_pack[c], preferred_element_type=F32) for c in chunks]
    g_row = [lax.dot_general(la_pack[c], triu_pack, (((0,), (0,)), ((), ())), preferred_element_type=F32)
             for c in chunks]
    g_hi, g_lo = _split_f32(jnp.concatenate(g_col, axis=0))
    g_b = jnp.dot(jnp.concatenate([g_hi, g_lo], axis=1).astype(BF16), selg_ref[...], preferred_element_type=F32)
    b_hi, b_lo = _split_f32(jnp.concatenate(beta, axis=0))
    b_b = jnp.dot(jnp.concatenate([b_hi, b_lo], axis=1).astype(BF16), selb_ref[...], preferred_element_type=F32)
    gi = {(c, h): g_b[rows[c], h * LANES:(h + 1) * LANES] for c, h in ch}
    bt = {(c, h): b_b[rows[c], h * LANES:(h + 1) * LANES] for c, h in ch}
    gj = {(c, h): g_row[c][S_BA + h:S_BA + h + 1, :] for c, h in ch}
    q = {(c, h): conv(c * chunk, h * B_DK) for c, h in ch}
    k = {(c, h): conv(c * chunk, B_QK + h * B_DK) for c, h in ch}
    v = {(c, h): conv(c * chunk, 2 * B_QK + h * B_DV) for c, h in ch}
    for c in chunks:
        sq = jnp.concatenate([q[c, h] * q[c, h] for h in heads] + [k[c, h] * k[c, h] for h in heads], axis=0)
        ssq = jnp.dot(sq.astype(BF16), ones, preferred_element_type=F32)
        for h in heads:
            q[c, h] = q[c, h] * (lax.rsqrt(ssq[h * chunk:(h + 1) * chunk, :] + EPS) * (B_DK ** -0.5))
            k[c, h] = k[c, h] * lax.rsqrt(ssq[(B_HEADS + h) * chunk:(B_HEADS + h + 1) * chunk, :] + EPS)
    decay = {i: jnp.exp(jnp.where(causal, gi[i] - gj[i], -jnp.inf)) for i in ch}
    kb = {i: k[i] * bt[i] for i in ch}
    eg = {i: jnp.exp(gi[i]) for i in ch}
    low = {i: jnp.where(strict, _bdot_nt(kb[i], jnp.concatenate([k[i]] * rep, axis=0)) * decay[i], 0.0) for i in ch}
    lb = {i: low[i].astype(BF16) for i in ch}
    m = {i: jnp.dot(lb[i][:, :chunk], lb[i], preferred_element_type=F32) for i in ch}
    tinv = {i: eye - low[i] for i in ch}
    for step in range(n_steps):
        mb = {i: m[i].astype(BF16) for i in ch}
        tb = {i: tinv[i].astype(BF16) for i in ch}
        if step == n_steps - 1:
            tinv = {i: tinv[i] + jnp.dot(mb[i][:, :chunk], tb[i], preferred_element_type=F32) for i in ch}
        else:
            out = {i: jnp.dot(mb[i][:, :chunk], jnp.concatenate([mb[i], tb[i]], axis=1),
                              preferred_element_type=F32) for i in ch}
            m = {i: out[i][:, :LANES] for i in ch}
            tinv = {i: tinv[i] + out[i][:, LANES:] for i in ch}
    sol = {i: _bdot(tinv[i][:, :chunk], jnp.concatenate([v[i] * bt[i], kb[i] * eg[i]], axis=-1)) for i in ch}
    attn = {i: jnp.where(causal[:, :chunk], _bdot_nt(q[i], k[i]) * decay[i][:, :chunk], 0.0) for i in ch}
    g_last = {i: gi[i][chunk - 1:chunk, :] for i in ch}
    k_tail = {i: k[i] * jnp.exp(g_last[i] - gi[i]) for i in ch}
    q_dec = {i: q[i] * eg[i] for i in ch}
    s = [s_ref[0, h] for h in heads]
    o = {}
    for c in chunks:
        v_new = [sol[c, h][:, :B_DV] - _bdot(sol[c, h][:, B_DV:], s[h]) for h in heads]
        for h in heads:
            o[c, h] = _bdot(q_dec[c, h], s[h]) + _bdot(attn[c, h], v_new[h])
        s = [s[h] * jnp.exp(g_last[c, h]) + _bdot_tn(k_tail[c, h], v_new[h]) for h in heads]
    for h in heads:
        s_ref[0, h] = s[h]
    for c in chunks:
        oo = jnp.concatenate([o[c, h] * o[c, h] for h in heads], axis=0)
        ms = jnp.dot(oo.astype(BF16), ones, preferred_element_type=F32) * (1.0 / B_DV)
        for h in heads:
            gate = g_ref[0, rows[c], h * B_DV:(h + 1) * B_DV].astype(F32)
            y = o[c, h] * lax.rsqrt(ms[h * chunk:(h + 1) * chunk, :] + EPS) * nrm_ref[...] * (gate * _sigmoid(gate))
            y_ref[0, rows[c], h * B_DV:(h + 1) * B_DV] = y.astype(BF16)


def _head_select(off, n_heads):
    r = jnp.arange(2 * LANES)[:, None] % LANES
    c = jnp.arange(n_heads * LANES)[None, :] // LANES
    return (r == off + c).astype(BF16)


def _gdn_mixer(z3, sm3, cbuf, s0, conv_w, alog_row, dt_row, norm_row, chunk):
    b, t_len, _ = z3.shape
    tt = min(TIME_TILE, t_len)
    wide = lambda off: pl.BlockSpec((1, tt, 512), lambda i, j: (i, j, off // 512))
    return pl.pallas_call(
        functools.partial(_gdn_kernel, tt=tt, chunk=chunk),
        grid=(b, t_len // tt),
        in_specs=[wide(Z_BQ), wide(Z_BK), wide(Z_BV), wide(Z_BG),
                  pl.BlockSpec((1, tt, LANES), lambda i, j: (i, j, 0)),
                  pl.BlockSpec((1, CONV_W - 1, B_CONV_CH), lambda i, j: (i, 0, 0)),
                  pl.BlockSpec((1, B_HEADS, B_DK, B_DV), lambda i, j: (i, 0, 0, 0)),
                  _resident((CONV_W, B_CONV_CH)), _resident((1, LANES)), _resident((1, LANES)),
                  _resident((1, B_DV)), _resident((2 * LANES, B_HEADS * LANES)),
                  _resident((2 * LANES, B_HEADS * LANES))],
        out_specs=[pl.BlockSpec((1, tt, B_V), lambda i, j: (i, j, 0)),
                   pl.BlockSpec((1, CONV_W - 1, B_CONV_CH), lambda i, j: (i, 0, 0)),
                   pl.BlockSpec((1, B_HEADS, B_DK, B_DV), lambda i, j: (i, 0, 0, 0))],
        out_shape=[jax.ShapeDtypeStruct((b, t_len, B_V), BF16),
                   jax.ShapeDtypeStruct((b, CONV_W - 1, B_CONV_CH), F32),
                   jax.ShapeDtypeStruct((b, B_HEADS, B_DK, B_DV), F32)],
        scratch_shapes=[pltpu.VMEM((8 + tt, B_CONV_CH), F32)],
        compiler_params=pltpu.CompilerParams(dimension_semantics=("parallel", "arbitrary"),
                                             vmem_limit_bytes=VMEM_LIMIT),
        name="gdn",
    )(z3, z3, z3, z3, sm3, cbuf, s0, conv_w, alog_row, dt_row, norm_row,
      _head_select(S_BA, B_HEADS), _head_select(S_BB, B_HEADS))


def _gla_kernel(q_ref, k_ref, v_ref, g_ref, sm_ref, s_in_ref, wup_ref, bup_ref, nrm_ref,
                y_ref, s_ref, *, tt, chunk):
    t = pl.program_id(1)

    @pl.when(t == 0)
    def _():
        s_ref[0] = s_in_ref[0]

    causal = _tri(chunk, "lower")
    tril_pack = jnp.concatenate([causal.astype(F32).astype(BF16)] * 2, axis=1)
    ones = jnp.ones((2 * chunk, LANES), BF16)
    heads = range(C_HEADS)
    ks = [slice(h * C_DK, (h + 1) * C_DK) for h in heads]
    vs = [slice(h * C_DV, (h + 1) * C_DV) for h in heads]
    chunks = range(tt // chunk)
    rows = [slice(c * chunk, (c + 1) * chunk) for c in chunks]
    sm_sp = [_split_f32(sm_ref[0, rows[c], :]) for c in chunks]
    lr = [jnp.dot(jnp.concatenate([sm_sp[c][0], sm_sp[c][1], sm_sp[c][0]], axis=1).astype(BF16), wup_ref[...],
                  preferred_element_type=F32) + bup_ref[...] for c in chunks]
    log_a = [-_softplus(-lr[c]) / GLA_NORMALIZER for c in chunks]
    la_sp = [_split_f32(log_a[c]) for c in chunks]
    la_pack = [jnp.concatenate([la_sp[c][0].astype(BF16), la_sp[c][1].astype(BF16)], axis=0) for c in chunks]
    bcum = [jnp.dot(tril_pack, la_pack[c], preferred_element_type=F32) for c in chunks]
    d_col = [jnp.exp(lax.dot_general(la_pack[c], ones, (((0,), (0,)), ((), ())), preferred_element_type=F32))
             for c in chunks]
    mid = [bcum[c][chunk // 2:chunk // 2 + 1, :] for c in chunks]
    last = [bcum[c][chunk - 1:chunk, :] for c in chunks]
    qf = [q_ref[0, rows[c], :].astype(F32) * (C_DK ** -0.5) for c in chunks]
    kf = [k_ref[0, rows[c], :].astype(F32) for c in chunks]
    q_in = [qf[c] * jnp.exp(bcum[c] - mid[c]) for c in chunks]
    k_in = [kf[c] * jnp.exp(mid[c] - bcum[c]) for c in chunks]
    q_x = [qf[c] * jnp.exp(bcum[c]) for c in chunks]
    k_t = [kf[c] * jnp.exp(last[c] - bcum[c]) for c in chunks]
    lane = lax.broadcasted_iota(jnp.int32, (chunk, LANES), 1)
    half = [(lane < C_DK) if h % 2 == 0 else (lane >= C_DK) for h in heads]
    grp = [slice((h // 2) * LANES, (h // 2 + 1) * LANES) for h in heads]
    pairs = range(C_HEADS // 2)
    v = [[v_ref[0, rows[c], vs[h]] for h in heads] for c in chunks]
    attn = [[jnp.where(causal, _bdot_nt(jnp.where(half[h], q_in[c][:, grp[h]], 0.0), k_in[c][:, grp[h]]), 0.0)
             for h in heads] for c in chunks]
    o_in = [[_bdot(attn[c][h], v[c][h]) for h in heads] for c in chunks]
    upd = [[_bdot_tn(jnp.where(half[h], k_t[c][:, grp[h]], 0.0), v[c][h]) for h in heads] for c in chunks]
    s = [[s_ref[0, p] for p in pairs]]
    for c in chunks:
        s.append([s[c][p] * d_col[c][p * LANES:(p + 1) * LANES, :] + upd[c][2 * p] + upd[c][2 * p + 1]
                  for p in pairs])
    for p in pairs:
        s_ref[0, p] = s[-1][p]
    o = [[_bdot(jnp.where(half[h], q_x[c][:, grp[h]], 0.0), s[c][h // 2]) + o_in[c][h] for h in heads]
         for c in chunks]
    inv_dv = jnp.full((LANES, LANES), 1.0 / C_DV, BF16)
    for c in chunks:
        oo = jnp.concatenate([o[c][h] * o[c][h] for h in heads], axis=0)
        ms = jnp.dot(oo.astype(BF16), inv_dv, preferred_element_type=F32)
        for h in heads:
            gate = g_ref[0, rows[c], vs[h]].astype(F32)
            y = o[c][h] * lax.rsqrt(ms[h * chunk:(h + 1) * chunk, :] + EPS) * nrm_ref[...] * (gate * _sigmoid(gate))
            y_ref[0, rows[c], vs[h]] = y.astype(BF16)


def _gla_mixer(z3, sm3, s0, wup_pad, bup_row, norm_row, chunk):
    b, t_len, _ = z3.shape
    tt = min(TIME_TILE, t_len)
    blk = lambda w, off: pl.BlockSpec((1, tt, w), lambda i, j: (i, j, off // w))
    return pl.pallas_call(
        functools.partial(_gla_kernel, tt=tt, chunk=chunk),
        grid=(b, t_len // tt),
        in_specs=[blk(C_QK, Z_CQ), blk(C_QK, Z_CK), blk(C_V, Z_CV), blk(C_V, Z_CG),
                  pl.BlockSpec((1, tt, LANES), lambda i, j: (i, j, 0)),
                  pl.BlockSpec((1, C_HEADS // 2, 2 * C_DK, C_DV), lambda i, j: (i, 0, 0, 0)),
                  _resident((3 * LANES, C_QK)), _resident((1, C_QK)), _resident((1, C_DV))],
        out_specs=[pl.BlockSpec((1, tt, C_V), lambda i, j: (i, j, 0)),
                   pl.BlockSpec((1, C_HEADS // 2, 2 * C_DK, C_DV), lambda i, j: (i, 0, 0, 0))],
        out_shape=[jax.ShapeDtypeStruct((b, t_len, C_V), BF16),
                   jax.ShapeDtypeStruct((b, C_HEADS // 2, 2 * C_DK, C_DV), F32)],
        compiler_params=pltpu.CompilerParams(dimension_semantics=("parallel", "arbitrary"),
                                             vmem_limit_bytes=VMEM_LIMIT),
        name="gla",
    )(z3, z3, z3, z3, sm3, s0, wup_pad, bup_row, norm_row)


def _mlstm_kernel(q_ref, k_ref, v_ref, og_ref, sm_ref, cn_in_ref, m_in_ref, bi_ref, bf_ref, nrm_ref,
                  self_ref, seli_ref, y_ref, cn_ref, m_ref, *, tt, chunk):
    t = pl.program_id(1)

    @pl.when(t == 0)
    def _():
        cn_ref[0] = cn_in_ref[0]
        m_ref[0] = m_in_ref[0]

    causal = _tri(chunk, "lower")
    tril_pack = jnp.concatenate([causal.astype(F32).astype(BF16)] * 2, axis=1)
    triu_pack = jnp.concatenate([_tri(chunk, "upper").astype(F32).astype(BF16)] * 2, axis=0)
    eye_pack = jnp.concatenate([_tri(chunk, "eye").astype(F32).astype(BF16)] * 2, axis=0)
    ones = jnp.ones((chunk, LANES), BF16)
    heads = range(D_HEADS)
    ks = [slice(h * D_DK, (h + 1) * D_DK) for h in heads]
    vs = [slice(h * D_DV, (h + 1) * D_DV) for h in heads]
    tn = (((0,), (0,)), ((), ()))
    chunks = range(tt // chunk)
    rows = [slice(c * chunk, (c + 1) * chunk) for c in chunks]
    sm = [sm_ref[0, rows[c], :] for c in chunks]
    i_pre = [sm[c] + bi_ref[...] for c in chunks]
    log_f = [-_softplus(-(sm[c] + bf_ref[...])) for c in chunks]
    lf_sp = [_split_f32(log_f[c]) for c in chunks]
    lf_pack = [jnp.concatenate([lf_sp[c][0].astype(BF16), lf_sp[c][1].astype(BF16)], axis=0) for c in chunks]
    ip_sp = [_split_f32(i_pre[c]) for c in chunks]
    ip_pack = [jnp.concatenate([ip_sp[c][0].astype(BF16), ip_sp[c][1].astype(BF16)], axis=0) for c in chunks]
    b_col = [jnp.dot(tril_pack, lf_pack[c], preferred_element_type=F32) for c in chunks]
    b_row = [lax.dot_general(lf_pack[c], triu_pack, tn, preferred_element_type=F32) for c in chunks]
    i_row = [lax.dot_general(ip_pack[c], eye_pack, tn, preferred_element_type=F32) for c in chunks]
    bc_hi, bc_lo = _split_f32(jnp.concatenate(b_col, axis=0))
    b_b = jnp.dot(jnp.concatenate([bc_hi, bc_lo], axis=1).astype(BF16), self_ref[...], preferred_element_type=F32)
    ic_hi, ic_lo = _split_f32(jnp.concatenate(i_pre, axis=0))
    i_b = jnp.dot(jnp.concatenate([ic_hi, ic_lo], axis=1).astype(BF16), seli_ref[...], preferred_element_type=F32)
    ch = [(c, h) for c in chunks for h in heads]
    bi = {(c, h): b_b[rows[c], h * LANES:(h + 1) * LANES] for c, h in ch}
    ii = {(c, h): i_b[rows[c], h * LANES:(h + 1) * LANES] for c, h in ch}
    lane = lax.broadcasted_iota(jnp.int32, (chunk, LANES), 1)
    half = [(lane < D_DK) if h % 2 == 0 else (lane >= D_DK) for h in heads]
    grp = [slice((h // 2) * LANES, (h // 2 + 1) * LANES) for h in heads]
    qf = [q_ref[0, rows[c], :].astype(F32) * (D_DK ** -0.5) for c in chunks]
    kf = [k_ref[0, rows[c], :].astype(F32) for c in chunks]
    q = {(c, h): jnp.where(half[h], qf[c][:, grp[h]], 0.0) for c, h in ch}
    v1 = {(c, h): jnp.concatenate([v_ref[0, rows[c], vs[h]], ones], axis=-1) for c, h in ch}
    log_w = {(c, h): jnp.where(causal, bi[c, h][:, :chunk] - b_row[c][S_DF + h:S_DF + h + 1, :]
                               + i_row[c][S_DI + h:S_DI + h + 1, :], -jnp.inf) for c, h in ch}
    lw_max = {i: jnp.max(log_w[i], axis=-1, keepdims=True) for i in ch}
    qk = {(c, h): _bdot_nt(q[c, h], kf[c][:, grp[h]]) for c, h in ch}
    b_last = {i: bi[i][chunk - 1:chunk, :] for i in ch}
    m_s = {(0, h): m_ref[0, h:h + 1, :] for h in heads}
    for c, h in ch:
        m_s[c + 1, h] = jnp.maximum(b_last[c, h] + m_s[c, h], lw_max[c, h][chunk - 1:chunk, :])
    for h in heads:
        m_ref[0, h:h + 1, :] = m_s[len(chunks), h]
    log_inter = {(c, h): bi[c, h] + m_s[c, h] for c, h in ch}
    m_t = {i: jnp.maximum(log_inter[i], lw_max[i]) for i in ch}
    w_inter = {i: jnp.exp(log_inter[i] - m_t[i]) for i in ch}
    p_w = {i: jnp.exp(log_w[i] - m_t[i][:, :chunk]) * qk[i] for i in ch}
    pv = {i: _bdot(p_w[i], v1[i]) for i in ch}
    w_k = {(c, h): jnp.exp(b_last[c, h] - bi[c, h] + ii[c, h] - m_s[c + 1, h]) for c, h in ch}
    upd = {(c, h): _bdot_tn(jnp.where(half[h], kf[c][:, grp[h]] * w_k[c, h], 0.0), v1[c, h]) for c, h in ch}
    top = lax.broadcasted_iota(jnp.int32, (2 * D_DK, LANES), 0) < D_DK
    pairs = range(D_HEADS // 2)
    cn = {(0, p): cn_ref[0, p] for p in pairs}
    for c in chunks:
        for p in pairs:
            w_a, w_b = (jnp.exp(b_last[c, h] + m_s[c, h] - m_s[c + 1, h]) for h in (2 * p, 2 * p + 1))
            w_state = jnp.where(top, w_a, w_b)
            cn[c + 1, p] = jnp.concatenate([w_state, w_state], axis=1) * cn[c, p] + upd[c, 2 * p] + upd[c, 2 * p + 1]
    for p in pairs:
        cn_ref[0, p] = cn[len(chunks), p]
    qcn = {(c, h): _bdot(q[c, h], cn[c, h // 2]) for c, h in ch}
    inv_dv = jnp.full((LANES, LANES), 1.0 / D_DV, BF16)
    for c in chunks:
        hh = []
        for h in heads:
            num = w_inter[c, h] * qcn[c, h][:, :D_DV] + pv[c, h][:, :D_DV]
            den = w_inter[c, h] * qcn[c, h][:, D_DV:] + pv[c, h][:, D_DV:]
            hh.append(num / jnp.maximum(jnp.abs(den), jnp.exp(-m_t[c, h])))
        ms = jnp.dot(jnp.concatenate([x * x for x in hh], axis=0).astype(BF16), inv_dv, preferred_element_type=F32)
        for h in heads:
            gate = og_ref[0, rows[c], vs[h]].astype(F32)
            y = _sigmoid(gate) * (hh[h] * lax.rsqrt(ms[h * chunk:(h + 1) * chunk, :] + EPS) * nrm_ref[...])
            y_ref[0, rows[c], vs[h]] = y.astype(BF16)


def _mlstm_mixer(z3, sm3, cn0, m0, bi_row, bf_row, norm_row, chunk):
    b, t_len, _ = z3.shape
    tt = min(TIME_TILE, t_len)
    blk = lambda w, off: pl.BlockSpec((1, tt, w), lambda i, j: (i, j, off // w))
    return pl.pallas_call(
        functools.partial(_mlstm_kernel, tt=tt, chunk=chunk),
        grid=(b, t_len // tt),
        in_specs=[blk(D_QK, Z_DQ), blk(D_QK, Z_DK), blk(D_V, Z_DV), blk(D_V, Z_DO),
                  pl.BlockSpec((1, tt, LANES), lambda i, j: (i, j, 0)),
                  pl.BlockSpec((1, D_HEADS // 2, 2 * D_DK, 2 * D_DV), lambda i, j: (i, 0, 0, 0)),
                  pl.BlockSpec((1, 8, LANES), lambda i, j: (i, 0, 0)),
                  _resident((1, LANES)), _resident((1, LANES)), _resident((1, D_DV)),
                  _resident((2 * LANES, D_HEADS * LANES)), _resident((2 * LANES, D_HEADS * LANES))],
        out_specs=[pl.BlockSpec((1, tt, D_V), lambda i, j: (i, j, 0)),
                   pl.BlockSpec((1, D_HEADS // 2, 2 * D_DK, 2 * D_DV), lambda i, j: (i, 0, 0, 0)),
                   pl.BlockSpec((1, 8, LANES), lambda i, j: (i, 0, 0))],
        out_shape=[jax.ShapeDtypeStruct((b, t_len, D_V), BF16),
                   jax.ShapeDtypeStruct((b, D_HEADS // 2, 2 * D_DK, 2 * D_DV), F32),
                   jax.ShapeDtypeStruct((b, 8, LANES), F32)],
        compiler_params=pltpu.CompilerParams(dimension_semantics=("parallel", "arbitrary"),
                                             vmem_limit_bytes=VMEM_LIMIT),
        name="mlstm",
    )(z3, z3, z3, z3, sm3, cn0, m0, bi_row, bf_row, norm_row,
      _head_select(S_DF, D_HEADS), _head_select(S_DI, D_HEADS))


def _merge_ffn_kernel(x_ref, ya_ref, yb_ref, yc_ref, yd_ref, gmix_ref, wgate_ref, bgate_ref, wbr_ref, wout_ref,
                      gffn_ref, wfg_ref, wfu_ref, wfd_ref, gfin_ref, o_ref, *, final):
    x = x_ref[...]
    hb = _rms(x, gmix_ref[...]).astype(BF16)
    merged = None
    for i, y_ref in enumerate((ya_ref, yb_ref, yc_ref, yd_ref)):
        gate = _sigmoid(jnp.dot(hb, wgate_ref[i], preferred_element_type=F32) + bgate_ref[i])
        term = gate * jnp.dot(y_ref[...], wbr_ref[i], preferred_element_type=F32)
        merged = term if merged is None else merged + term
    x = x + jnp.dot(merged.astype(BF16), wout_ref[...], preferred_element_type=F32)
    h2 = _rms(x, gffn_ref[...]).astype(BF16)
    for c0 in range(0, FFN_HIDDEN, FFN_CHUNK):
        a = jnp.dot(h2, wfg_ref[:, c0:c0 + FFN_CHUNK], preferred_element_type=F32)
        u = jnp.dot(h2, wfu_ref[:, c0:c0 + FFN_CHUNK], preferred_element_type=F32)
        f = (a * _sigmoid(a) * u).astype(BF16)
        x = x + jnp.dot(f, wfd_ref[c0:c0 + FFN_CHUNK, :], preferred_element_type=F32)
    if final:
        x = _rms(x, gfin_ref[...])
    o_ref[...] = x


def _merge_ffn(x2d, ys, gmix, wgate, bgate, wbr, wout, gffn, wfg, wfu, wfd, gfin, final):
    n = x2d.shape[0]
    r = min(ROW_TILE, n)
    row = lambda w: pl.BlockSpec((r, w), lambda i: (i, 0))
    return pl.pallas_call(
        functools.partial(_merge_ffn_kernel, final=final),
        grid=(n // r,),
        in_specs=[row(D_MODEL), row(BR_WIDTH), row(BR_WIDTH), row(BR_WIDTH), row(BR_WIDTH),
                  _resident((1, D_MODEL)), _resident((4, D_MODEL, D_MODEL)), _resident((4, 1, D_MODEL)),
                  _resident((4, BR_WIDTH, D_MODEL)), _resident((D_MODEL, D_MODEL)), _resident((1, D_MODEL)),
                  _resident((D_MODEL, FFN_HIDDEN)), _resident((D_MODEL, FFN_HIDDEN)),
                  _resident((FFN_HIDDEN, D_MODEL)), _resident((1, D_MODEL))],
        out_specs=row(D_MODEL),
        out_shape=jax.ShapeDtypeStruct((n, D_MODEL), F32),
        compiler_params=pltpu.CompilerParams(dimension_semantics=("parallel",), vmem_limit_bytes=VMEM_LIMIT),
        name="merge_ffn",
    )(x2d, *ys, gmix, wgate, bgate, wbr, wout, gffn, wfg, wfu, wfd, gfin)


def _lane_row(vec, off):
    return jnp.zeros((1, LANES), F32).at[0, off:off + vec.shape[0]].set(vec.astype(F32))


def _layer_params(l, p):
    w_in = p["w_in"][l]
    offs = [0]
    for s in IN_SPLITS:
        offs.append(offs[-1] + s)
    piece = lambda i: w_in[:, offs[i]:offs[i + 1]]
    wide_ids = (0, 1, 2, 3, 4, 7, 8, 9, 10, 12, 13, 14, 15)
    wz = jnp.concatenate([piece(i) for i in wide_ids], axis=1).astype(BF16)
    small = jnp.concatenate([piece(5), piece(6), piece(16), piece(17), piece(11)], axis=1)
    ws = jnp.zeros((D_MODEL, LANES), F32).at[:, :small.shape[1]].set(small).astype(BF16)
    wup = jnp.zeros((LANES, C_QK), F32).at[S_LR:S_LR + GLA_RANK].set(p["gla_w_up"][l].astype(F32))
    wup_hi = wup.astype(BF16)
    wup_lo = (wup - wup_hi.astype(F32)).astype(BF16)
    return dict(
        gmix=p["norm_mix"][l].reshape(1, D_MODEL), wz=wz, ws=ws,
        pool_w=p["pool_w"][l].astype(BF16), pool_scale=p["pool_scale"][l].reshape(1, A_WIDTH),
        conv_w=p["gdn_conv_w"][l],
        alog_row=_lane_row(-jnp.exp(p["gdn_a_log"][l].astype(F32)), S_BA),
        dt_row=_lane_row(p["gdn_dt_bias"][l], S_BA),
        gdn_norm=p["gdn_norm"][l].reshape(1, B_DV),
        wup=jnp.concatenate([wup_hi, wup_hi, wup_lo], axis=0),
        bup=p["gla_b_up"][l].reshape(1, C_QK), gla_norm=p["gla_norm"][l].reshape(1, C_DV),
        bi_row=_lane_row(p["mlstm_b_i"][l], S_DI), bf_row=_lane_row(p["mlstm_b_f"][l], S_DF),
        mlstm_norm=p["mlstm_norm"][l].reshape(1, D_DV),
        wgate=p["w_gate"][l].astype(BF16), bgate=p["b_gate"][l].reshape(4, 1, D_MODEL),
        wbr=p["w_branch"][l].astype(BF16), wout=p["w_out"][l].astype(BF16),
        gffn=p["norm_ffn"][l].reshape(1, D_MODEL),
        wfg=p["w_ffn_gate"][l].astype(BF16), wfu=p["w_ffn_up"][l].astype(BF16), wfd=p["w_ffn_down"][l].astype(BF16),
    )


def _layer(x, st, lp, gfin, pos0, chunk, final):
    b, t_len, _ = x.shape
    a_buf, conv_buf, gdn_s, gla_s, ml_c, ml_n, ml_m = st
    x2d = x.reshape(b * t_len, D_MODEL)
    z, sm = _project(x2d, lp["gmix"], lp["wz"], lp["ws"])
    z3 = z.reshape(b, t_len, Z_WIDTH)
    sm3 = sm.reshape(b, t_len, LANES)
    y_a, a_new = _pool_mixer(z3, a_buf, lp["pool_w"], lp["pool_scale"], pos0)
    y_b, conv_new, gdn_new = _gdn_mixer(z3, sm3, conv_buf, gdn_s, lp["conv_w"], lp["alog_row"], lp["dt_row"],
                                        lp["gdn_norm"], chunk)
    y_c, gla_new = _gla_mixer(z3, sm3, gla_s.reshape(b, C_HEADS // 2, 2 * C_DK, C_DV), lp["wup"], lp["bup"],
                              lp["gla_norm"], chunk)
    gla_new = gla_new.reshape(b, C_HEADS, C_DK, C_DV)
    cn0 = jnp.concatenate([ml_c, jnp.broadcast_to(ml_n[..., None], ml_c.shape)], axis=-1)
    cn0 = cn0.reshape(b, D_HEADS // 2, 2 * D_DK, 2 * D_DV)
    m0 = jnp.zeros((b, 8, LANES), F32).at[:, :D_HEADS, :].set(jnp.broadcast_to(ml_m[..., None], (b, D_HEADS, LANES)))
    y_d, cn_new, m_new = _mlstm_mixer(z3, sm3, cn0, m0, lp["bi_row"], lp["bf_row"], lp["mlstm_norm"], chunk)
    ys = [y.reshape(b * t_len, BR_WIDTH) for y in (y_a, y_b, y_c, y_d)]
    x_new = _merge_ffn(x2d, ys, lp["gmix"], lp["wgate"], lp["bgate"], lp["wbr"], lp["wout"], lp["gffn"],
                       lp["wfg"], lp["wfu"], lp["wfd"], gfin, final)
    cn_new = cn_new.reshape(b, D_HEADS, D_DK, 2 * D_DV)
    new_st = (a_new, conv_new, gdn_new, gla_new, cn_new[..., :D_DV], cn_new[..., D_DV], m_new[:, :D_HEADS, 0])
    return x_new.reshape(b, t_len, D_MODEL), new_st


def kernel(x_prompt, x_sample, state_a_pool, state_b_conv, state_b_S, state_c_S, state_d_C, state_d_n, state_d_m,
           norm_mix, w_in, pool_w, pool_scale, gdn_conv_w, gdn_a_log, gdn_dt_bias, gdn_norm,
           gla_w_up, gla_b_up, gla_norm, mlstm_b_i, mlstm_b_f, mlstm_norm,
           w_branch, w_gate, b_gate, w_out, norm_ffn, w_ffn_gate, w_ffn_up, w_ffn_down, norm_final):
    p = dict(norm_mix=norm_mix, w_in=w_in, pool_w=pool_w, pool_scale=pool_scale, gdn_conv_w=gdn_conv_w,
             gdn_a_log=gdn_a_log, gdn_dt_bias=gdn_dt_bias, gdn_norm=gdn_norm, gla_w_up=gla_w_up, gla_b_up=gla_b_up,
             gla_norm=gla_norm, mlstm_b_i=mlstm_b_i, mlstm_b_f=mlstm_b_f, mlstm_norm=mlstm_norm,
             w_branch=w_branch, w_gate=w_gate, b_gate=b_gate, w_out=w_out, norm_ffn=norm_ffn,
             w_ffn_gate=w_ffn_gate, w_ffn_up=w_ffn_up, w_ffn_down=w_ffn_down)
    bp = x_prompt.shape[0]
    zero_p = (jnp.zeros((bp, POOL_BUF, A_WIDTH), F32), jnp.zeros((bp, CONV_W - 1, B_CONV_CH), F32),
              jnp.zeros((bp, B_HEADS, B_DK, B_DV), F32), jnp.zeros((bp, C_HEADS, C_DK, C_DV), F32),
              jnp.zeros((bp, D_HEADS, D_DK, D_DV), F32), jnp.zeros((bp, D_HEADS, D_DK), F32),
              jnp.zeros((bp, D_HEADS), F32))
    gfin = norm_final.reshape(1, D_MODEL)
    yp, ys = x_prompt, x_sample
    new_p, new_s = [], []
    for l in range(DEPTH):
        lp = _layer_params(l, p)
        final = l == DEPTH - 1
        yp, sp = _layer(yp, zero_p, lp, gfin, 0, CHUNK, final)
        cache_l = (state_a_pool[l], state_b_conv[l], state_b_S[l], state_c_S[l],
                   state_d_C[l], state_d_n[l], state_d_m[l])
        ys, ss = _layer(ys, cache_l, lp, gfin, PAST_LEN, x_sample.shape[1], final)
        new_p.append(sp)
        new_s.append(ss)
    outs_p = [jnp.stack([s[i] for s in new_p]) for i in range(7)]
    outs_s = [jnp.stack([s[i] for s in new_s]) for i in range(7)]
    return (yp, ys, *outs_p, *outs_s)
```

```python
import functools
import math

import jax
import jax.numpy as jnp
from jax import lax
from jax.experimental import pallas as pl
from jax.experimental.pallas import tpu as pltpu

F32 = jnp.float32
BF16 = jnp.bfloat16

D_MODEL = 1024
DEPTH = 2
PAST_LEN = 2048
CHUNK = 64
EPS = 1e-6
POOL_WINDOWS = (2, 4, 8, 16)
A_WIDTH = 512
A_GROUP = 128
POOL_BUF = 15
B_HEADS, B_DK, B_DV = 4, 128, 128
B_QK = B_HEADS * B_DK
B_V = B_HEADS * B_DV
CONV_W = 4
B_CONV_CH = 2 * B_QK + B_V
C_HEADS, C_DK, C_DV = 4, 64, 128
C_QK = C_HEADS * C_DK
C_V = C_HEADS * C_DV
GLA_RANK = 16
GLA_NORMALIZER = 16.0
D_HEADS, D_DK, D_DV = 4, 64, 128
D_QK = D_HEADS * D_DK
D_V = D_HEADS * D_DV
BR_WIDTH = 512
N_BRANCH = 4
FFN_HIDDEN = 2816
IN_SPLITS = (A_WIDTH, B_QK, B_QK, B_V, B_V, B_HEADS, B_HEADS, C_QK, C_QK, C_V, C_V, GLA_RANK,
             D_QK, D_QK, D_V, D_V, D_HEADS, D_HEADS)

LANES = 128
Z_WIDTH = 5632
Z_A, Z_BQ, Z_BK, Z_BV, Z_BG = 0, 512, 1024, 1536, 2048
Z_CQ, Z_CK, Z_CV, Z_CG = 2560, 2816, 3072, 3584
Z_DQ, Z_DK, Z_DV, Z_DO = 4096, 4352, 4608, 5120
S_BA, S_BB, S_DI, S_DF, S_LR = 0, 4, 8, 12, 16

ROW_TILE = 256
TIME_TILE = 256
TAIL_COLS = 512
FFN_CHUNK = 256
TAIL_RATIO = 3
POOL_HDR = 16
CONV_HDR = 8
VMEM_LIMIT = 61 * 1024 * 1024

_TN = (((0,), (0,)), ((), ()))
_NT = (((1,), (1,)), ((), ()))


def _bdot(a, b):
    return jnp.dot(a.astype(BF16), b.astype(BF16), preferred_element_type=F32)


def _bdot_nt(a, b):
    return lax.dot_general(a.astype(BF16), b.astype(BF16), _NT, preferred_element_type=F32)


def _bdot_tn(a, b):
    return lax.dot_general(a.astype(BF16), b.astype(BF16), _TN, preferred_element_type=F32)


def _sigmoid(x):
    return 1.0 / (1.0 + jnp.exp(-x))


def _softplus(x):
    return jnp.maximum(x, 0.0) + jnp.log(1.0 + jnp.exp(-jnp.abs(x)))


def _rms(x, gain):
    return x * lax.rsqrt(jnp.mean(x * x, axis=-1, keepdims=True) + EPS) * gain


def _tri(c, kind):
    r = lax.broadcasted_iota(jnp.int32, (c, c), 0)
    k = lax.broadcasted_iota(jnp.int32, (c, c), 1)
    return {"lower": r >= k, "upper": r <= k, "eye": r == k}[kind]


def _bf16_mask(mask):
    return mask.astype(F32).astype(BF16)


def _split_f32(x):
    hi = x.astype(BF16).astype(F32)
    return hi, x - hi


def _hi_lo_rows(x):
    hi, lo = _split_f32(x)
    return jnp.concatenate([hi.astype(BF16), lo.astype(BF16)], axis=0)


def _hi_lo_lanes(x):
    hi, lo = _split_f32(x)
    return jnp.concatenate([hi, lo], axis=1).astype(BF16)


def _resident(shape):
    nd = len(shape)
    return pl.BlockSpec(shape, lambda *_: (0,) * nd, pipeline_mode=pl.Buffered(1))


def _run(*gens, lead=None, ratio=1):
    def step(g):
        try:
            next(g)
            return True
        except StopIteration:
            return False

    gens = list(gens)
    issued = 0
    while gens:
        for g in list(gens):
            if not step(g):
                gens.remove(g)
                continue
            issued += 1
            if lead is not None and issued % ratio == 0 and not step(lead):
                lead = None
    while lead is not None and step(lead):
        pass


def _head_select(off, n_heads):
    r = jnp.arange(2 * LANES)[:, None] % LANES
    c = jnp.arange(n_heads * LANES)[None, :] // LANES
    return (r == off + c).astype(BF16)


def _proj_kernel(x_ref, g_ref, wz_ref, ws_ref, z_ref, s_ref):
    hb = _rms(x_ref[...], g_ref[...]).astype(BF16)
    step = 512
    for c0 in range(0, Z_WIDTH, step):
        z_ref[:, c0:c0 + step] = jnp.dot(hb, wz_ref[:, c0:c0 + step], preferred_element_type=F32).astype(BF16)
    s_ref[...] = jnp.dot(hb, ws_ref[...], preferred_element_type=F32)


def _project(x2d, gain, wz, ws):
    n = x2d.shape[0]
    r = min(ROW_TILE, n)
    return pl.pallas_call(
        _proj_kernel,
        grid=(n // r,),
        in_specs=[pl.BlockSpec((r, D_MODEL), lambda i: (i, 0)),
                  _resident((1, D_MODEL)), _resident((D_MODEL, Z_WIDTH)), _resident((D_MODEL, LANES))],
        out_specs=[pl.BlockSpec((r, Z_WIDTH), lambda i: (i, 0)), pl.BlockSpec((r, LANES), lambda i: (i, 0))],
        out_shape=[jax.ShapeDtypeStruct((n, Z_WIDTH), BF16), jax.ShapeDtypeStruct((n, LANES), F32)],
        compiler_params=pltpu.CompilerParams(dimension_semantics=("parallel",), vmem_limit_bytes=VMEM_LIMIT),
        name="proj",
    )(x2d, gain, wz, ws)


def _pool_stages(u_ref, pw_ref, sc_ref, full_ref, store_y, pos, tt):
    hdr = POOL_HDR
    full_ref[hdr:hdr + tt, :] = u_ref[0].astype(F32)
    yield
    for gi, w in enumerate(POOL_WINDOWS):
        cols = slice(gi * A_GROUP, (gi + 1) * A_GROUP)
        u = full_ref[hdr:hdr + tt, cols]
        acc = u
        for j in range(1, w):
            acc = acc + full_ref[hdr - j:hdr - j + tt, cols]
        cnt = jnp.minimum(pos + 1, w).astype(F32)
        diff = acc / cnt - u
        y = _bdot(diff, pw_ref[gi]) * sc_ref[:, cols]
        store_y(slice(0, tt), cols, y.astype(BF16))
        yield


def _gdn_stages(q_ref, k_ref, v_ref, g_ref, sm_ref, cw_ref, alog_ref, dt_ref, nrm_ref, selg_ref, selb_ref,
                s_ref, full_ref, store_y, tt, chunk):
    hdr = CONV_HDR
    full_ref[hdr:hdr + tt, 0:B_QK] = q_ref[0].astype(F32)
    full_ref[hdr:hdr + tt, B_QK:2 * B_QK] = k_ref[0].astype(F32)
    full_ref[hdr:hdr + tt, 2 * B_QK:] = v_ref[0].astype(F32)
    yield
    rep = LANES // chunk
    row = lax.broadcasted_iota(jnp.int32, (chunk, LANES), 0)
    col = lax.broadcasted_iota(jnp.int32, (chunk, LANES), 1) & (chunk - 1)
    causal = row >= col
    strict = row > col
    eye = (row == col).astype(F32)
    tril2 = jnp.concatenate([_bf16_mask(causal[:, :chunk])] * 2, axis=1)
    triu2 = jnp.concatenate([_bf16_mask(row <= col)] * 2, axis=0)
    ones = jnp.ones((LANES, LANES), BF16)
    heads = range(B_HEADS)
    chunks = range(tt // chunk)
    ch = [(c, h) for c in chunks for h in heads]
    rows = [slice(c * chunk, (c + 1) * chunk) for c in chunks]
    n_steps = int(math.log2(chunk)) - 1

    def conv(r0, c0):
        acc = full_ref[r0 + hdr - 3:r0 + hdr - 3 + chunk, c0:c0 + LANES] * cw_ref[0:1, c0:c0 + LANES]
        for j in range(1, CONV_W):
            acc = acc + (full_ref[r0 + hdr - 3 + j:r0 + hdr - 3 + j + chunk, c0:c0 + LANES]
                         * cw_ref[j:j + 1, c0:c0 + LANES])
        return acc * _sigmoid(acc)

    q, k, v = {}, {}, {}
    for c in chunks:
        for h in heads:
            q[c, h] = conv(c * chunk, h * B_DK)
            k[c, h] = conv(c * chunk, B_QK + h * B_DK)
            v[c, h] = conv(c * chunk, 2 * B_QK + h * B_DV)
        yield
    for c in chunks:
        sq = jnp.concatenate([q[c, h] * q[c, h] for h in heads] + [k[c, h] * k[c, h] for h in heads], axis=0)
        ssq = jnp.dot(sq.astype(BF16), ones, preferred_element_type=F32)
        for h in heads:
            q[c, h] = q[c, h] * (lax.rsqrt(ssq[h * chunk:(h + 1) * chunk, :] + EPS) * (B_DK ** -0.5))
            k[c, h] = k[c, h] * lax.rsqrt(ssq[(B_HEADS + h) * chunk:(B_HEADS + h + 1) * chunk, :] + EPS)
    yield
    sm = [sm_ref[0, rows[c], :] for c in chunks]
    log_alpha = [alog_ref[...] * _softplus(sm[c] + dt_ref[...]) for c in chunks]
    beta = [_sigmoid(sm[c]) for c in chunks]
    la2 = [_hi_lo_rows(log_alpha[c]) for c in chunks]
    g_col = [jnp.dot(tril2, la2[c], preferred_element_type=F32) for c in chunks]
    g_row = [lax.dot_general(la2[c], triu2, _TN, preferred_element_type=F32) for c in chunks]
    yield
    g_b = jnp.dot(_hi_lo_lanes(jnp.concatenate(g_col, axis=0)), selg_ref[...], preferred_element_type=F32)
    b_b = jnp.dot(_hi_lo_lanes(jnp.concatenate(beta, axis=0)), selb_ref[...], preferred_element_type=F32)
    gi = {(c, h): g_b[rows[c], h * LANES:(h + 1) * LANES] for c, h in ch}
    bt = {(c, h): b_b[rows[c], h * LANES:(h + 1) * LANES] for c, h in ch}
    gj = {(c, h): g_row[c][S_BA + h:S_BA + h + 1, :] for c, h in ch}
    yield
    decay, kb, eg, low = {}, {}, {}, {}
    for c in chunks:
        for i in [(c, h) for h in heads]:
            decay[i] = jnp.exp(jnp.where(causal, gi[i] - gj[i], -jnp.inf))
            kb[i] = k[i] * bt[i]
            eg[i] = jnp.exp(gi[i])
            low[i] = jnp.where(strict, _bdot_nt(kb[i], jnp.concatenate([k[i]] * rep, axis=0)) * decay[i], 0.0)
        yield
    lb = {i: low[i].astype(BF16) for i in ch}
    m = {i: jnp.dot(lb[i][:, :chunk], lb[i], preferred_element_type=F32) for i in ch}
    tinv = {i: eye - low[i] for i in ch}
    yield
    for step in range(n_steps):
        mb = {i: m[i].astype(BF16) for i in ch}
        tb = {i: tinv[i].astype(BF16) for i in ch}
        if step == n_steps - 1:
            tinv = {i: tinv[i] + jnp.dot(mb[i][:, :chunk], tb[i], preferred_element_type=F32) for i in ch}
        else:
            out = {i: jnp.dot(mb[i][:, :chunk], jnp.concatenate([mb[i], tb[i]], axis=1),
                              preferred_element_type=F32) for i in ch}
            m = {i: out[i][:, :LANES] for i in ch}
            tinv = {i: tinv[i] + out[i][:, LANES:] for i in ch}
        yield
    sol, attn, g_last, k_tail, q_dec = {}, {}, {}, {}, {}
    for c in chunks:
        for i in [(c, h) for h in heads]:
            sol[i] = _bdot(tinv[i][:, :chunk], jnp.concatenate([v[i] * bt[i], kb[i] * eg[i]], axis=-1))
            attn[i] = jnp.where(causal[:, :chunk], _bdot_nt(q[i], k[i]) * decay[i][:, :chunk], 0.0)
            g_last[i] = gi[i][chunk - 1:chunk, :]
            k_tail[i] = k[i] * jnp.exp(g_last[i] - gi[i])
            q_dec[i] = q[i] * eg[i]
        yield
    s = [s_ref[h] for h in heads]
    o = {}
    for c in chunks:
        ws = [_bdot(sol[c, h][:, B_DV:], s[h]) for h in heads]
        qs = [_bdot(q_dec[c, h], s[h]) for h in heads]
        yield
        v_new = [sol[c, h][:, :B_DV] - ws[h] for h in heads]
        for h in heads:
            o[c, h] = qs[h] + _bdot(attn[c, h], v_new[h])
        s = [s[h] * jnp.exp(g_last[c, h]) + _bdot_tn(k_tail[c, h], v_new[h]) for h in heads]
        yield
    for h in heads:
        s_ref[h] = s[h]
    for c in chunks:
        oo = jnp.concatenate([o[c, h] * o[c, h] for h in heads], axis=0)
        ms = jnp.dot(oo.astype(BF16), ones, preferred_element_type=F32) * (1.0 / B_DV)
        for h in heads:
            cols = slice(h * B_DV, (h + 1) * B_DV)
            gate = g_ref[0, rows[c], cols].astype(F32)
            y = o[c, h] * lax.rsqrt(ms[h * chunk:(h + 1) * chunk, :] + EPS) * nrm_ref[...] * (gate * _sigmoid(gate))
            store_y(rows[c], cols, y.astype(BF16))
        yield


def _gla_stages(q_ref, k_ref, v_ref, g_ref, sm_ref, wup_ref, bup_ref, nrm_ref, s_ref, store_y, tt, chunk):
    causal = _tri(chunk, "lower")
    tril2 = jnp.concatenate([_bf16_mask(causal)] * 2, axis=1)
    ones = jnp.ones((2 * chunk, LANES), BF16)
    heads = range(C_HEADS)
    pairs = range(C_HEADS // 2)
    chunks = range(tt // chunk)
    rows = [slice(c * chunk, (c + 1) * chunk) for c in chunks]
    vs = [slice(h * C_DV, (h + 1) * C_DV) for h in heads]
    sm_sp = [_split_f32(sm_ref[0, rows[c], :]) for c in chunks]
    lr = [jnp.dot(jnp.concatenate([sm_sp[c][0], sm_sp[c][1], sm_sp[c][0]], axis=1).astype(BF16), wup_ref[...],
                  preferred_element_type=F32) + bup_ref[...] for c in chunks]
    log_a = [-_softplus(-lr[c]) / GLA_NORMALIZER for c in chunks]
    yield
    la2 = [_hi_lo_rows(log_a[c]) for c in chunks]
    bcum = [jnp.dot(tril2, la2[c], preferred_element_type=F32) for c in chunks]
    d_col = [jnp.exp(lax.dot_general(la2[c], ones, _TN, preferred_element_type=F32)) for c in chunks]
    yield
    mid = [bcum[c][chunk // 2:chunk // 2 + 1, :] for c in chunks]
    last = [bcum[c][chunk - 1:chunk, :] for c in chunks]
    qf = [q_ref[0, rows[c], :].astype(F32) * (C_DK ** -0.5) for c in chunks]
    kf = [k_ref[0, rows[c], :].astype(F32) for c in chunks]
    q_in = [qf[c] * jnp.exp(bcum[c] - mid[c]) for c in chunks]
    k_in = [kf[c] * jnp.exp(mid[c] - bcum[c]) for c in chunks]
    q_x = [qf[c] * jnp.exp(bcum[c]) for c in chunks]
    k_t = [kf[c] * jnp.exp(last[c] - bcum[c]) for c in chunks]
    yield
    lane = lax.broadcasted_iota(jnp.int32, (chunk, LANES), 1)
    half = [(lane < C_DK) if h % 2 == 0 else (lane >= C_DK) for h in heads]
    grp = [slice((h // 2) * LANES, (h // 2 + 1) * LANES) for h in heads]
    v = [[v_ref[0, rows[c], vs[h]] for h in heads] for c in chunks]
    attn = [[jnp.where(causal, _bdot_nt(jnp.where(half[h], q_in[c][:, grp[h]], 0.0), k_in[c][:, grp[h]]), 0.0)
             for h in heads] for c in chunks]
    yield
    o_in = [[_bdot(attn[c][h], v[c][h]) for h in heads] for c in chunks]
    upd = [[_bdot_tn(jnp.where(half[h], k_t[c][:, grp[h]], 0.0), v[c][h]) for h in heads] for c in chunks]
    yield
    s = [[s_ref[p] for p in pairs]]
    for c in chunks:
        s.append([s[c][p] * d_col[c][p * LANES:(p + 1) * LANES, :] + upd[c][2 * p] + upd[c][2 * p + 1]
                  for p in pairs])
    for p in pairs:
        s_ref[p] = s[-1][p]
    o = [[_bdot(jnp.where(half[h], q_x[c][:, grp[h]], 0.0), s[c][h // 2]) + o_in[c][h] for h in heads]
         for c in chunks]
    yield
    inv_dv = jnp.full((LANES, LANES), 1.0 / C_DV, BF16)
    for c in chunks:
        oo = jnp.concatenate([o[c][h] * o[c][h] for h in heads], axis=0)
        ms = jnp.dot(oo.astype(BF16), inv_dv, preferred_element_type=F32)
        for h in heads:
            gate = g_ref[0, rows[c], vs[h]].astype(F32)
            y = o[c][h] * lax.rsqrt(ms[h * chunk:(h + 1) * chunk, :] + EPS) * nrm_ref[...] * (gate * _sigmoid(gate))
            store_y(rows[c], vs[h], y.astype(BF16))
        yield


def _mlstm_stages(q_ref, k_ref, v_ref, og_ref, sm_ref, bi_ref, bf_ref, nrm_ref, self_ref, seli_ref,
                  cn_ref, m_ref, store_y, tt, chunk):
    causal = _tri(chunk, "lower")
    tril2 = jnp.concatenate([_bf16_mask(causal)] * 2, axis=1)
    triu2 = jnp.concatenate([_bf16_mask(_tri(chunk, "upper"))] * 2, axis=0)
    eye2 = jnp.concatenate([_bf16_mask(_tri(chunk, "eye"))] * 2, axis=0)
    ones = jnp.ones((chunk, LANES), BF16)
    heads = range(D_HEADS)
    pairs = range(D_HEADS // 2)
    chunks = range(tt // chunk)
    n_chunks = len(chunks)
    ch = [(c, h) for c in chunks for h in heads]
    rows = [slice(c * chunk, (c + 1) * chunk) for c in chunks]
    vs = [slice(h * D_DV, (h + 1) * D_DV) for h in heads]
    sm = [sm_ref[0, rows[c], :] for c in chunks]
    i_pre = [sm[c] + bi_ref[...] for c in chunks]
    log_f = [-_softplus(-(sm[c] + bf_ref[...])) for c in chunks]
    lf2 = [_hi_lo_rows(log_f[c]) for c in chunks]
    ip2 = [_hi_lo_rows(i_pre[c]) for c in chunks]
    b_col = [jnp.dot(tril2, lf2[c], preferred_element_type=F32) for c in chunks]
    b_row = [lax.dot_general(lf2[c], triu2, _TN, preferred_element_type=F32) for c in chunks]
    i_row = [lax.dot_general(ip2[c], eye2, _TN, preferred_element_type=F32) for c in chunks]
    yield
    b_b = jnp.dot(_hi_lo_lanes(jnp.concatenate(b_col, axis=0)), self_ref[...], preferred_element_type=F32)
    i_b = jnp.dot(_hi_lo_lanes(jnp.concatenate(i_pre, axis=0)), seli_ref[...], preferred_element_type=F32)
    bi = {(c, h): b_b[rows[c], h * LANES:(h + 1) * LANES] for c, h in ch}
    ii = {(c, h): i_b[rows[c], h * LANES:(h + 1) * LANES] for c, h in ch}
    yield
    lane = lax.broadcasted_iota(jnp.int32, (chunk, LANES), 1)
    half = [(lane < D_DK) if h % 2 == 0 else (lane >= D_DK) for h in heads]
    grp = [slice((h // 2) * LANES, (h // 2 + 1) * LANES) for h in heads]
    qf = [q_ref[0, rows[c], :].astype(F32) * (D_DK ** -0.5) for c in chunks]
    kf = [k_ref[0, rows[c], :].astype(F32) for c in chunks]
    q, v1, log_w, lw_max, qk = {}, {}, {}, {}, {}
    for c in chunks:
        for h in heads:
            q[c, h] = jnp.where(half[h], qf[c][:, grp[h]], 0.0)
            v1[c, h] = jnp.concatenate([v_ref[0, rows[c], vs[h]], ones], axis=-1)
            log_w[c, h] = jnp.where(causal, bi[c, h][:, :chunk] - b_row[c][S_DF + h:S_DF + h + 1, :]
                                    + i_row[c][S_DI + h:S_DI + h + 1, :], -jnp.inf)
            lw_max[c, h] = jnp.max(log_w[c, h], axis=-1, keepdims=True)
            qk[c, h] = _bdot_nt(q[c, h], kf[c][:, grp[h]])
        yield
    b_last = {i: bi[i][chunk - 1:chunk, :] for i in ch}
    m_s = {(0, h): m_ref[h:h + 1, :] for h in heads}
    for c, h in ch:
        m_s[c + 1, h] = jnp.maximum(b_last[c, h] + m_s[c, h], lw_max[c, h][chunk - 1:chunk, :])
    for h in heads:
        m_ref[h:h + 1, :] = m_s[n_chunks, h]
    m_t, w_inter, pv, upd = {}, {}, {}, {}
    for c in chunks:
        for h in heads:
            log_inter = bi[c, h] + m_s[c, h]
            m_t[c, h] = jnp.maximum(log_inter, lw_max[c, h])
            w_inter[c, h] = jnp.exp(log_inter - m_t[c, h])
            p_w = jnp.exp(log_w[c, h] - m_t[c, h][:, :chunk]) * qk[c, h]
            pv[c, h] = _bdot(p_w, v1[c, h])
            w_k = jnp.exp(b_last[c, h] - bi[c, h] + ii[c, h] - m_s[c + 1, h])
            upd[c, h] = _bdot_tn(jnp.where(half[h], kf[c][:, grp[h]] * w_k, 0.0), v1[c, h])
        yield
    top = lax.broadcasted_iota(jnp.int32, (2 * D_DK, LANES), 0) < D_DK
    cn = {(0, p): cn_ref[p] for p in pairs}
    for c in chunks:
        for p in pairs:
            w_a, w_b = (jnp.exp(b_last[c, h] + m_s[c, h] - m_s[c + 1, h]) for h in (2 * p, 2 * p + 1))
            w_state = jnp.where(top, w_a, w_b)
            cn[c + 1, p] = jnp.concatenate([w_state, w_state], axis=1) * cn[c, p] + upd[c, 2 * p] + upd[c, 2 * p + 1]
    for p in pairs:
        cn_ref[p] = cn[n_chunks, p]
    qcn = {(c, h): _bdot(q[c, h], cn[c, h // 2]) for c, h in ch}
    yield
    inv_dv = jnp.full((LANES, LANES), 1.0 / D_DV, BF16)
    for c in chunks:
        hh = []
        for h in heads:
            num = w_inter[c, h] * qcn[c, h][:, :D_DV] + pv[c, h][:, :D_DV]
            den = w_inter[c, h] * qcn[c, h][:, D_DV:] + pv[c, h][:, D_DV:]
            hh.append(num / jnp.maximum(jnp.abs(den), jnp.exp(-m_t[c, h])))
        ms = jnp.dot(jnp.concatenate([x * x for x in hh], axis=0).astype(BF16), inv_dv, preferred_element_type=F32)
        for h in heads:
            gate = og_ref[0, rows[c], vs[h]].astype(F32)
            y = _sigmoid(gate) * (hh[h] * lax.rsqrt(ms[h * chunk:(h + 1) * chunk, :] + EPS) * nrm_ref[...])
            store_y(rows[c], vs[h], y.astype(BF16))
        yield


def _tail_stages(x_ref, read_y, gmix_ref, wgate_ref, bgate_ref, wbr_ref, wout_ref, gffn_ref, wfg_ref, wfu_ref,
                 wfd_ref, gfin_ref, o_ref, final):
    x = x_ref[...]
    hb = _rms(x, gmix_ref[...]).astype(BF16)
    yield
    merged = []
    for c0 in range(0, D_MODEL, TAIL_COLS):
        cols = slice(c0, c0 + TAIL_COLS)
        acc = None
        for i in range(N_BRANCH):
            gate = _sigmoid(jnp.dot(hb, wgate_ref[i, :, cols], preferred_element_type=F32) + bgate_ref[i, :, cols])
            term = gate * jnp.dot(read_y(i), wbr_ref[i, :, cols], preferred_element_type=F32)
            acc = term if acc is None else acc + term
            yield
        merged.append(acc.astype(BF16))
    mb = jnp.concatenate(merged, axis=1)
    x1 = []
    for c0 in range(0, D_MODEL, TAIL_COLS):
        cols = slice(c0, c0 + TAIL_COLS)
        x1.append(x[:, cols] + jnp.dot(mb, wout_ref[:, cols], preferred_element_type=F32))
        yield
    x = jnp.concatenate(x1, axis=1)
    h2 = _rms(x, gffn_ref[...]).astype(BF16)

    def hidden(c0):
        a = jnp.dot(h2, wfg_ref[:, c0:c0 + FFN_CHUNK], preferred_element_type=F32)
        u = jnp.dot(h2, wfu_ref[:, c0:c0 + FFN_CHUNK], preferred_element_type=F32)
        return (a * _sigmoid(a) * u).astype(BF16)

    f = hidden(0)
    yield
    for c0 in range(0, FFN_HIDDEN, FFN_CHUNK):
        f_next = hidden(c0 + FFN_CHUNK) if c0 + FFN_CHUNK < FFN_HIDDEN else None
        x = x + jnp.dot(f, wfd_ref[c0:c0 + FFN_CHUNK, :], preferred_element_type=F32)
        f = f_next
        yield
    if final:
        x = _rms(x, gfin_ref[...])
    o_ref[...] = x


def _pool_kernel(u_ref, buf_ref, pw_ref, sc_ref, y_ref, nbuf_ref, full_ref, *, tt, pos0):
    t = pl.program_id(1)
    hdr = POOL_HDR

    @pl.when(t == 0)
    def _():
        full_ref[1:hdr, :] = buf_ref[0]

    @pl.when(t > 0)
    def _():
        full_ref[1:hdr, :] = full_ref[tt + 1:tt + hdr, :]

    def store_y(rows, cols, val):
        y_ref[0, rows, cols] = val

    pos = pos0 + t * tt + lax.broadcasted_iota(jnp.int32, (tt, 1), 0)
    _run(_pool_stages(u_ref, pw_ref, sc_ref, full_ref, store_y, pos, tt))
    nbuf_ref[0] = full_ref[tt + 1:tt + hdr, :]


def _pool_mixer(z3, buf, pool_w, scale, pos0):
    b, t_len, _ = z3.shape
    tt = min(TIME_TILE, t_len)
    return pl.pallas_call(
        functools.partial(_pool_kernel, tt=tt, pos0=pos0),
        grid=(b, t_len // tt),
        in_specs=[pl.BlockSpec((1, tt, A_WIDTH), lambda i, j: (i, j, Z_A // A_WIDTH)),
                  pl.BlockSpec((1, POOL_BUF, A_WIDTH), lambda i, j: (i, 0, 0)),
                  _resident((4, A_GROUP, A_GROUP)), _resident((1, A_WIDTH))],
        out_specs=[pl.BlockSpec((1, tt, A_WIDTH), lambda i, j: (i, j, 0)),
                   pl.BlockSpec((1, POOL_BUF, A_WIDTH), lambda i, j: (i, 0, 0))],
        out_shape=[jax.ShapeDtypeStruct((b, t_len, A_WIDTH), BF16),
                   jax.ShapeDtypeStruct((b, POOL_BUF, A_WIDTH), F32)],
        scratch_shapes=[pltpu.VMEM((POOL_HDR + tt, A_WIDTH), F32)],
        compiler_params=pltpu.CompilerParams(dimension_semantics=("parallel", "arbitrary"),
                                             vmem_limit_bytes=VMEM_LIMIT),
        name="pool",
    )(z3, buf, pool_w, scale)


def _gdn_kernel(q_ref, k_ref, v_ref, g_ref, sm_ref, cbuf_ref, s_in_ref, cw_ref, alog_ref, dt_ref, nrm_ref,
                selg_ref, selb_ref, y_ref, ncbuf_ref, s_ref, full_ref, *, tt, chunk):
    t = pl.program_id(1)
    hdr = CONV_HDR

    @pl.when(t == 0)
    def _():
        full_ref[hdr - 3:hdr, :] = cbuf_ref[0]
        s_ref[0] = s_in_ref[0]

    @pl.when(t > 0)
    def _():
        full_ref[hdr - 3:hdr, :] = full_ref[tt + hdr - 3:tt + hdr, :]

    def store_y(rows, cols, val):
        y_ref[0, rows, cols] = val

    _run(_gdn_stages(q_ref, k_ref, v_ref, g_ref, sm_ref, cw_ref, alog_ref, dt_ref, nrm_ref, selg_ref, selb_ref,
                     s_ref.at[0], full_ref, store_y, tt, chunk))
    ncbuf_ref[0] = full_ref[tt + hdr - 3:tt + hdr, :]


def _gdn_mixer(z3, sm3, cbuf, s0, conv_w, alog_row, dt_row, norm_row, chunk):
    b, t_len, _ = z3.shape
    tt = min(TIME_TILE, t_len)
    wide = lambda off: pl.BlockSpec((1, tt, 512), lambda i, j: (i, j, off // 512))
    sel = _resident((2 * LANES, B_HEADS * LANES))
    return pl.pallas_call(
        functools.partial(_gdn_kernel, tt=tt, chunk=chunk),
        grid=(b, t_len // tt),
        in_specs=[wide(Z_BQ), wide(Z_BK), wide(Z_BV), wide(Z_BG),
                  pl.BlockSpec((1, tt, LANES), lambda i, j: (i, j, 0)),
                  pl.BlockSpec((1, CONV_W - 1, B_CONV_CH), lambda i, j: (i, 0, 0)),
                  pl.BlockSpec((1, B_HEADS, B_DK, B_DV), lambda i, j: (i, 0, 0, 0)),
                  _resident((CONV_W, B_CONV_CH)), _resident((1, LANES)), _resident((1, LANES)),
                  _resident((1, B_DV)), sel, sel],
        out_specs=[pl.BlockSpec((1, tt, B_V), lambda i, j: (i, j, 0)),
                   pl.BlockSpec((1, CONV_W - 1, B_CONV_CH), lambda i, j: (i, 0, 0)),
                   pl.BlockSpec((1, B_HEADS, B_DK, B_DV), lambda i, j: (i, 0, 0, 0))],
        out_shape=[jax.ShapeDtypeStruct((b, t_len, B_V), BF16),
                   jax.ShapeDtypeStruct((b, CONV_W - 1, B_CONV_CH), F32),
                   jax.ShapeDtypeStruct((b, B_HEADS, B_DK, B_DV), F32)],
        scratch_shapes=[pltpu.VMEM((CONV_HDR + tt, B_CONV_CH), F32)],
        compiler_params=pltpu.CompilerParams(dimension_semantics=("parallel", "arbitrary"),
                                             vmem_limit_bytes=VMEM_LIMIT),
        name="gdn",
    )(z3, z3, z3, z3, sm3, cbuf, s0, conv_w, alog_row, dt_row, norm_row,
      _head_select(S_BA, B_HEADS), _head_select(S_BB, B_HEADS))


def _gla_kernel(q_ref, k_ref, v_ref, g_ref, sm_ref, s_in_ref, wup_ref, bup_ref, nrm_ref,
                y_ref, s_ref, *, tt, chunk):
    @pl.when(pl.program_id(1) == 0)
    def _():
        s_ref[0] = s_in_ref[0]

    def store_y(rows, cols, val):
        y_ref[0, rows, cols] = val

    _run(_gla_stages(q_ref, k_ref, v_ref, g_ref, sm_ref, wup_ref, bup_ref, nrm_ref, s_ref.at[0], store_y, tt, chunk))


def _gla_mixer(z3, sm3, s0, wup_pack, bup_row, norm_row, chunk):
    b, t_len, _ = z3.shape
    tt = min(TIME_TILE, t_len)
    blk = lambda w, off: pl.BlockSpec((1, tt, w), lambda i, j: (i, j, off // w))
    state = pl.BlockSpec((1, C_HEADS // 2, 2 * C_DK, C_DV), lambda i, j: (i, 0, 0, 0))
    return pl.pallas_call(
        functools.partial(_gla_kernel, tt=tt, chunk=chunk),
        grid=(b, t_len // tt),
        in_specs=[blk(C_QK, Z_CQ), blk(C_QK, Z_CK), blk(C_V, Z_CV), blk(C_V, Z_CG),
                  pl.BlockSpec((1, tt, LANES), lambda i, j: (i, j, 0)), state,
                  _resident((3 * LANES, C_QK)), _resident((1, C_QK)), _resident((1, C_DV))],
        out_specs=[pl.BlockSpec((1, tt, C_V), lambda i, j: (i, j, 0)), state],
        out_shape=[jax.ShapeDtypeStruct((b, t_len, C_V), BF16),
                   jax.ShapeDtypeStruct((b, C_HEADS // 2, 2 * C_DK, C_DV), F32)],
        compiler_params=pltpu.CompilerParams(dimension_semantics=("parallel", "arbitrary"),
                                             vmem_limit_bytes=VMEM_LIMIT),
        name="gla",
    )(z3, z3, z3, z3, sm3, s0, wup_pack, bup_row, norm_row)


def _mlstm_kernel(q_ref, k_ref, v_ref, og_ref, sm_ref, cn_in_ref, m_in_ref, bi_ref, bf_ref, nrm_ref,
                  self_ref, seli_ref, y_ref, cn_ref, m_ref, *, tt, chunk):
    @pl.when(pl.program_id(1) == 0)
    def _():
        cn_ref[0] = cn_in_ref[0]
        m_ref[0] = m_in_ref[0]

    def store_y(rows, cols, val):
        y_ref[0, rows, cols] = val

    _run(_mlstm_stages(q_ref, k_ref, v_ref, og_ref, sm_ref, bi_ref, bf_ref, nrm_ref, self_ref, seli_ref,
                       cn_ref.at[0], m_ref.at[0], store_y, tt, chunk))


def _mlstm_mixer(z3, sm3, cn0, m0, bi_row, bf_row, norm_row, chunk):
    b, t_len, _ = z3.shape
    tt = min(TIME_TILE, t_len)
    blk = lambda w, off: pl.BlockSpec((1, tt, w), lambda i, j: (i, j, off // w))
    cn_spec = pl.BlockSpec((1, D_HEADS // 2, 2 * D_DK, 2 * D_DV), lambda i, j: (i, 0, 0, 0))
    m_spec = pl.BlockSpec((1, 8, LANES), lambda i, j: (i, 0, 0))
    sel = _resident((2 * LANES, D_HEADS * LANES))
    return pl.pallas_call(
        functools.partial(_mlstm_kernel, tt=tt, chunk=chunk),
        grid=(b, t_len // tt),
        in_specs=[blk(D_QK, Z_DQ), blk(D_QK, Z_DK), blk(D_V, Z_DV), blk(D_V, Z_DO),
                  pl.BlockSpec((1, tt, LANES), lambda i, j: (i, j, 0)), cn_spec, m_spec,
                  _resident((1, LANES)), _resident((1, LANES)), _resident((1, D_DV)), sel, sel],
        out_specs=[pl.BlockSpec((1, tt, D_V), lambda i, j: (i, j, 0)), cn_spec, m_spec],
        out_shape=[jax.ShapeDtypeStruct((b, t_len, D_V), BF16),
                   jax.ShapeDtypeStruct((b, D_HEADS // 2, 2 * D_DK, 2 * D_DV), F32),
                   jax.ShapeDtypeStruct((b, 8, LANES), F32)],
        compiler_params=pltpu.CompilerParams(dimension_semantics=("parallel", "arbitrary"),
                                             vmem_limit_bytes=VMEM_LIMIT),
        name="mlstm",
    )(z3, z3, z3, z3, sm3, cn0, m0, bi_row, bf_row, norm_row,
      _head_select(S_DF, D_HEADS), _head_select(S_DI, D_HEADS))


def _tail_kernel(x_ref, ya_ref, yb_ref, yc_ref, yd_ref, gmix_ref, wgate_ref, bgate_ref, wbr_ref, wout_ref,
                 gffn_ref, wfg_ref, wfu_ref, wfd_ref, gfin_ref, o_ref, *, final):
    ys = (ya_ref, yb_ref, yc_ref, yd_ref)
    _run(_tail_stages(x_ref, lambda i: ys[i][...], gmix_ref, wgate_ref, bgate_ref, wbr_ref, wout_ref, gffn_ref,
                      wfg_ref, wfu_ref, wfd_ref, gfin_ref, o_ref, final))


def _tail_weight_specs():
    return [_resident((1, D_MODEL)), _resident((N_BRANCH, D_MODEL, D_MODEL)), _resident((N_BRANCH, 1, D_MODEL)),
            _resident((N_BRANCH, BR_WIDTH, D_MODEL)), _resident((D_MODEL, D_MODEL)), _resident((1, D_MODEL)),
            _resident((D_MODEL, FFN_HIDDEN)), _resident((D_MODEL, FFN_HIDDEN)),
            _resident((FFN_HIDDEN, D_MODEL)), _resident((1, D_MODEL))]


def _tail_weights(lp, gfin):
    return (lp["gmix"], lp["wgate"], lp["bgate"], lp["wbr"], lp["wout"], lp["gffn"], lp["wfg"], lp["wfu"], lp["wfd"],
            gfin)


def _tail(x2d, ys, lp, gfin, final):
    n = x2d.shape[0]
    r = min(ROW_TILE, n)
    row = lambda w: pl.BlockSpec((r, w), lambda i: (i, 0))
    return pl.pallas_call(
        functools.partial(_tail_kernel, final=final),
        grid=(n // r,),
        in_specs=[row(D_MODEL)] + [row(BR_WIDTH)] * N_BRANCH + _tail_weight_specs(),
        out_specs=row(D_MODEL),
        out_shape=jax.ShapeDtypeStruct((n, D_MODEL), F32),
        compiler_params=pltpu.CompilerParams(dimension_semantics=("parallel",), vmem_limit_bytes=VMEM_LIMIT),
        name="tail",
    )(x2d, *ys, *_tail_weights(lp, gfin))


def _mix_tail_kernel(a_ref, bq_ref, bk_ref, bv_ref, bg_ref, cq_ref, ck_ref, cv_ref, cg_ref,
                     dq_ref, dk_ref, dv_ref, do_ref, sm_ref,
                     abuf_in, cbuf_in, gs_in, cs_in, cn_in, m_in, x_ref,
                     pw_ref, psc_ref, cw_ref, alog_ref, dt_ref, gnrm_ref, selg_ref, selb_ref,
                     wup_ref, bup_ref, cnrm_ref, bi_ref, bf_ref, dnrm_ref, self_ref, seli_ref,
                     gmix_ref, wgate_ref, bgate_ref, wbr_ref, wout_ref, gffn_ref, wfg_ref, wfu_ref, wfd_ref, gfin_ref,
                     o_ref, abuf_out, cbuf_out, gs_out, cs_out, cn_out, m_out,
                     ybuf, pfull, cfull, gs, cs, cn, m, *, tt, chunk, nt, n_tiles, pos0, final):
    s = pl.program_id(0)
    t = jnp.minimum(s, n_tiles - 1) % nt
    slot = s % 2

    @pl.when(s == 0)
    def _():
        ybuf[1] = jnp.zeros(ybuf.shape[1:], BF16)

    @pl.when(t == 0)
    def _():
        pfull[1:POOL_HDR, :] = abuf_in[0]
        cfull[CONV_HDR - 3:CONV_HDR, :] = cbuf_in[0]
        gs[...] = gs_in[0]
        cs[...] = cs_in[0]
        cn[...] = cn_in[0]
        m[...] = m_in[0]

    @pl.when(t > 0)
    def _():
        pfull[1:POOL_HDR, :] = pfull[tt + 1:tt + POOL_HDR, :]
        cfull[CONV_HDR - 3:CONV_HDR, :] = cfull[tt + CONV_HDR - 3:tt + CONV_HDR, :]

    def store_y(branch):
        def store(rows, cols, val):
            ybuf[slot, rows, slice(branch * BR_WIDTH + cols.start, branch * BR_WIDTH + cols.stop)] = val
        return store

    def read_y(i):
        return ybuf[1 - slot, :, i * BR_WIDTH:(i + 1) * BR_WIDTH]

    pos = pos0 + t * tt + lax.broadcasted_iota(jnp.int32, (tt, 1), 0)
    _run(_gdn_stages(bq_ref, bk_ref, bv_ref, bg_ref, sm_ref, cw_ref, alog_ref, dt_ref, gnrm_ref, selg_ref, selb_ref,
                     gs, cfull, store_y(1), tt, chunk),
         _mlstm_stages(dq_ref, dk_ref, dv_ref, do_ref, sm_ref, bi_ref, bf_ref, dnrm_ref, self_ref, seli_ref,
                       cn, m, store_y(3), tt, chunk),
         _gla_stages(cq_ref, ck_ref, cv_ref, cg_ref, sm_ref, wup_ref, bup_ref, cnrm_ref, cs, store_y(2), tt, chunk),
         _pool_stages(a_ref, pw_ref, psc_ref, pfull, store_y(0), pos, tt),
         lead=_tail_stages(x_ref, read_y, gmix_ref, wgate_ref, bgate_ref, wbr_ref, wout_ref, gffn_ref, wfg_ref,
                           wfu_ref, wfd_ref, gfin_ref, o_ref, final),
         ratio=TAIL_RATIO)

    @pl.when((t == nt - 1) & (s < n_tiles))
    def _():
        abuf_out[0] = pfull[tt + 1:tt + POOL_HDR, :]
        cbuf_out[0] = cfull[tt + CONV_HDR - 3:tt + CONV_HDR, :]
        gs_out[0] = gs[...]
        cs_out[0] = cs[...]
        cn_out[0] = cn[...]
        m_out[0] = m[...]


def _mix_tail(x2d, z3, sm3, states, lp, gfin, pos0, chunk, final):
    b, t_len, _ = z3.shape
    tt = TIME_TILE
    nt = t_len // tt
    n_tiles = b * nt
    a_buf, conv_buf, gdn_s, gla_s, cn0, m0 = states

    def bt(s):
        sm = jnp.minimum(s, n_tiles - 1)
        return sm // nt, sm % nt

    def zblk(w, off):
        return pl.BlockSpec((1, tt, w), lambda s: (*bt(s), off // w))

    def per_stream(shape):
        nd = len(shape)
        return pl.BlockSpec((1,) + shape, lambda s: (bt(s)[0],) + (0,) * nd)

    x_spec = pl.BlockSpec((tt, D_MODEL), lambda s: (jnp.maximum(s - 1, 0), 0))
    state_shapes = [(POOL_BUF, A_WIDTH), (CONV_W - 1, B_CONV_CH), (B_HEADS, B_DK, B_DV),
                    (C_HEADS // 2, 2 * C_DK, C_DV), (D_HEADS // 2, 2 * D_DK, 2 * D_DV), (8, LANES)]
    selb = _resident((2 * LANES, B_HEADS * LANES))
    seld = _resident((2 * LANES, D_HEADS * LANES))
    row128 = _resident((1, LANES))
    outs = pl.pallas_call(
        functools.partial(_mix_tail_kernel, tt=tt, chunk=chunk, nt=nt, n_tiles=n_tiles, pos0=pos0, final=final),
        grid=(n_tiles + 1,),
        in_specs=[zblk(A_WIDTH, Z_A), zblk(512, Z_BQ), zblk(512, Z_BK), zblk(512, Z_BV), zblk(512, Z_BG),
                  zblk(C_QK, Z_CQ), zblk(C_QK, Z_CK), zblk(C_V, Z_CV), zblk(C_V, Z_CG),
                  zblk(D_QK, Z_DQ), zblk(D_QK, Z_DK), zblk(D_V, Z_DV), zblk(D_V, Z_DO),
                  pl.BlockSpec((1, tt, LANES), lambda s: (*bt(s), 0))]
                 + [per_stream(sh) for sh in state_shapes] + [x_spec]
                 + [_resident((4, A_GROUP, A_GROUP)), _resident((1, A_WIDTH)),
                    _resident((CONV_W, B_CONV_CH)), row128, row128, _resident((1, B_DV)), selb, selb,
                    _resident((3 * LANES, C_QK)), _resident((1, C_QK)), _resident((1, C_DV)),
                    row128, row128, _resident((1, D_DV)), seld, seld]
                 + _tail_weight_specs(),
        out_specs=[x_spec] + [per_stream(sh) for sh in state_shapes],
        out_shape=[jax.ShapeDtypeStruct((n_tiles * tt, D_MODEL), F32)]
                  + [jax.ShapeDtypeStruct((b,) + sh, F32) for sh in state_shapes],
        scratch_shapes=[pltpu.VMEM((2, tt, N_BRANCH * BR_WIDTH), BF16),
                        pltpu.VMEM((POOL_HDR + tt, A_WIDTH), F32), pltpu.VMEM((CONV_HDR + tt, B_CONV_CH), F32)]
                       + [pltpu.VMEM(sh, F32) for sh in state_shapes[2:]],
        compiler_params=pltpu.CompilerParams(dimension_semantics=("arbitrary",), vmem_limit_bytes=VMEM_LIMIT),
        name="mix_tail",
    )(*([z3] * 13), sm3, a_buf, conv_buf, gdn_s, gla_s, cn0, m0, x2d,
      lp["pool_w"], lp["pool_scale"], lp["conv_w"], lp["alog_row"], lp["dt_row"], lp["gdn_norm"],
      _head_select(S_BA, B_HEADS), _head_select(S_BB, B_HEADS), lp["wup"], lp["bup"], lp["gla_norm"],
      lp["bi_row"], lp["bf_row"], lp["mlstm_norm"], _head_select(S_DF, D_HEADS), _head_select(S_DI, D_HEADS),
      *_tail_weights(lp, gfin))
    return outs[0], outs[1:]


def _lane_row(vec, off):
    return jnp.zeros((1, LANES), F32).at[0, off:off + vec.shape[0]].set(vec.astype(F32))


def _layer_params(l, p):
    w_in = p["w_in"][l]
    offs = [0]
    for s in IN_SPLITS:
        offs.append(offs[-1] + s)
    piece = lambda i: w_in[:, offs[i]:offs[i + 1]]
    wide_ids = (0, 1, 2, 3, 4, 7, 8, 9, 10, 12, 13, 14, 15)
    wz = jnp.concatenate([piece(i) for i in wide_ids], axis=1).astype(BF16)
    small = jnp.concatenate([piece(5), piece(6), piece(16), piece(17), piece(11)], axis=1)
    ws = jnp.zeros((D_MODEL, LANES), F32).at[:, :small.shape[1]].set(small).astype(BF16)
    wup = jnp.zeros((LANES, C_QK), F32).at[S_LR:S_LR + GLA_RANK].set(p["gla_w_up"][l].astype(F32))
    wup_hi = wup.astype(BF16)
    wup_lo = (wup - wup_hi.astype(F32)).astype(BF16)
    return dict(
        gmix=p["norm_mix"][l].reshape(1, D_MODEL), wz=wz, ws=ws,
        pool_w=p["pool_w"][l].astype(BF16), pool_scale=p["pool_scale"][l].reshape(1, A_WIDTH),
        conv_w=p["gdn_conv_w"][l],
        alog_row=_lane_row(-jnp.exp(p["gdn_a_log"][l].astype(F32)), S_BA),
        dt_row=_lane_row(p["gdn_dt_bias"][l], S_BA),
        gdn_norm=p["gdn_norm"][l].reshape(1, B_DV),
        wup=jnp.concatenate([wup_hi, wup_hi, wup_lo], axis=0),
        bup=p["gla_b_up"][l].reshape(1, C_QK), gla_norm=p["gla_norm"][l].reshape(1, C_DV),
        bi_row=_lane_row(p["mlstm_b_i"][l], S_DI), bf_row=_lane_row(p["mlstm_b_f"][l], S_DF),
        mlstm_norm=p["mlstm_norm"][l].reshape(1, D_DV),
        wgate=p["w_gate"][l].astype(BF16), bgate=p["b_gate"][l].reshape(4, 1, D_MODEL),
        wbr=p["w_branch"][l].astype(BF16), wout=p["w_out"][l].astype(BF16),
        gffn=p["norm_ffn"][l].reshape(1, D_MODEL),
        wfg=p["w_ffn_gate"][l].astype(BF16), wfu=p["w_ffn_up"][l].astype(BF16), wfd=p["w_ffn_down"][l].astype(BF16),
    )


def _layer(x, st, lp, gfin, pos0, chunk, final):
    b, t_len, _ = x.shape
    a_buf, conv_buf, gdn_s, gla_s, ml_c, ml_n, ml_m = st
    x2d = x.reshape(b * t_len, D_MODEL)
    z, sm = _project(x2d, lp["gmix"], lp["wz"], lp["ws"])
    z3 = z.reshape(b, t_len, Z_WIDTH)
    sm3 = sm.reshape(b, t_len, LANES)
    gla_s = gla_s.reshape(b, C_HEADS // 2, 2 * C_DK, C_DV)
    cn0 = jnp.concatenate([ml_c, jnp.broadcast_to(ml_n[..., None], ml_c.shape)], axis=-1)
    cn0 = cn0.reshape(b, D_HEADS // 2, 2 * D_DK, 2 * D_DV)
    m0 = jnp.zeros((b, 8, LANES), F32).at[:, :D_HEADS, :].set(jnp.broadcast_to(ml_m[..., None], (b, D_HEADS, LANES)))
    if t_len % TIME_TILE == 0:
        x_new, (a_new, conv_new, gdn_new, gla_new, cn_new, m_new) = _mix_tail(
            x2d, z3, sm3, (a_buf, conv_buf, gdn_s, gla_s, cn0, m0), lp, gfin, pos0, chunk, final)
    else:
        y_a, a_new = _pool_mixer(z3, a_buf, lp["pool_w"], lp["pool_scale"], pos0)
        y_b, conv_new, gdn_new = _gdn_mixer(z3, sm3, conv_buf, gdn_s, lp["conv_w"], lp["alog_row"], lp["dt_row"],
                                            lp["gdn_norm"], chunk)
        y_c, gla_new = _gla_mixer(z3, sm3, gla_s, lp["wup"], lp["bup"], lp["gla_norm"], chunk)
        y_d, cn_new, m_new = _mlstm_mixer(z3, sm3, cn0, m0, lp["bi_row"], lp["bf_row"], lp["mlstm_norm"], chunk)
        ys = [y.reshape(b * t_len, BR_WIDTH) for y in (y_a, y_b, y_c, y_d)]
        x_new = _tail(x2d, ys, lp, gfin, final)
    gla_new = gla_new.reshape(b, C_HEADS, C_DK, C_DV)
    cn_new = cn_new.reshape(b, D_HEADS, D_DK, 2 * D_DV)
    new_st = (a_new, conv_new, gdn_new, gla_new, cn_new[..., :D_DV], cn_new[..., D_DV], m_new[:, :D_HEADS, 0])
    return x_new.reshape(b, t_len, D_MODEL), new_st


def kernel(x_prompt, x_sample, state_a_pool, state_b_conv, state_b_S, state_c_S, state_d_C, state_d_n, state_d_m,
           norm_mix, w_in, pool_w, pool_scale, gdn_conv_w, gdn_a_log, gdn_dt_bias, gdn_norm,
           gla_w_up, gla_b_up, gla_norm, mlstm_b_i, mlstm_b_f, mlstm_norm,
           w_branch, w_gate, b_gate, w_out, norm_ffn, w_ffn_gate, w_ffn_up, w_ffn_down, norm_final):
    p = dict(norm_mix=norm_mix, w_in=w_in, pool_w=pool_w, pool_scale=pool_scale, gdn_conv_w=gdn_conv_w,
             gdn_a_log=gdn_a_log, gdn_dt_bias=gdn_dt_bias, gdn_norm=gdn_norm, gla_w_up=gla_w_up, gla_b_up=gla_b_up,
             gla_norm=gla_norm, mlstm_b_i=mlstm_b_i, mlstm_b_f=mlstm_b_f, mlstm_norm=mlstm_norm,
             w_branch=w_branch, w_gate=w_gate, b_gate=b_gate, w_out=w_out, norm_ffn=norm_ffn,
             w_ffn_gate=w_ffn_gate, w_ffn_up=w_ffn_up, w_ffn_down=w_ffn_down)
    bp = x_prompt.shape[0]
    zero_p = (jnp.zeros((bp, POOL_BUF, A_WIDTH), F32), jnp.zeros((bp, CONV_W - 1, B_CONV_CH), F32),
              jnp.zeros((bp, B_HEADS, B_DK, B_DV), F32), jnp.zeros((bp, C_HEADS, C_DK, C_DV), F32),
              jnp.zeros((bp, D_HEADS, D_DK, D_DV), F32), jnp.zeros((bp, D_HEADS, D_DK), F32),
              jnp.zeros((bp, D_HEADS), F32))
    gfin = norm_final.reshape(1, D_MODEL)
    yp, ys = x_prompt, x_sample
    new_p, new_s = [], []
    for l in range(DEPTH):
        lp = _layer_params(l, p)
        final = l == DEPTH - 1
        yp, sp = _layer(yp, zero_p, lp, gfin, 0, CHUNK, final)
        cache_l = (state_a_pool[l], state_b_conv[l], state_b_S[l], state_c_S[l],
                   state_d_C[l], state_d_n[l], state_d_m[l])
        ys, ss = _layer(ys, cache_l, lp, gfin, PAST_LEN, x_sample.shape[1], final)
        new_p.append(sp)
        new_s.append(ss)
    outs_p = [jnp.stack([s[i] for s in new_p]) for i in range(7)]
    outs_s = [jnp.stack([s[i] for s in new_s]) for i in range(7)]
    return (yp, ys, *outs_p, *outs_s)
```

```python
import functools
import math

import jax
import jax.numpy as jnp
from jax import lax
from jax.experimental import pallas as pl
from jax.experimental.pallas import tpu as pltpu

F32 = jnp.float32
BF16 = jnp.bfloat16

D_MODEL = 1024
DEPTH = 2
PAST_LEN = 2048
CHUNK = 64
EPS = 1e-6
POOL_WINDOWS = (2, 4, 8, 16)
A_WIDTH = 512
A_GROUP = 128
POOL_BUF = 15
B_HEADS, B_DK, B_DV = 4, 128, 128
B_QK = B_HEADS * B_DK
B_V = B_HEADS * B_DV
CONV_W = 4
B_CONV_CH = 2 * B_QK + B_V
C_HEADS, C_DK, C_DV = 4, 64, 128
C_QK = C_HEADS * C_DK
C_V = C_HEADS * C_DV
GLA_RANK = 16
GLA_NORMALIZER = 16.0
D_HEADS, D_DK, D_DV = 4, 64, 128
D_QK = D_HEADS * D_DK
D_V = D_HEADS * D_DV
BR_WIDTH = 512
N_BRANCH = 4
FFN_HIDDEN = 2816
IN_SPLITS = (A_WIDTH, B_QK, B_QK, B_V, B_V, B_HEADS, B_HEADS, C_QK, C_QK, C_V, C_V, GLA_RANK,
             D_QK, D_QK, D_V, D_V, D_HEADS, D_HEADS)

LANES = 128
Z_WIDTH = 5632
Z_A, Z_BQ, Z_BK, Z_BV, Z_BG = 0, 512, 1024, 1536, 2048
Z_CQ, Z_CK, Z_CV, Z_CG = 2560, 2816, 3072, 3584
Z_DQ, Z_DK, Z_DV, Z_DO = 4096, 4352, 4608, 5120
S_BA, S_BB, S_DI, S_DF, S_LR = 0, 4, 8, 12, 16

ROW_TILE = 512
TIME_TILE = 256
TAIL_COLS = 512
FFN_CHUNK = 256
TAIL_RATIO = 3
POOL_HDR = 16
CONV_HDR = 8
VMEM_LIMIT = 61 * 1024 * 1024

_TN = (((0,), (0,)), ((), ()))
_NT = (((1,), (1,)), ((), ()))


def _bdot(a, b):
    return jnp.dot(a.astype(BF16), b.astype(BF16), preferred_element_type=F32)


def _bdot_nt(a, b):
    return lax.dot_general(a.astype(BF16), b.astype(BF16), _NT, preferred_element_type=F32)


def _bdot_tn(a, b):
    return lax.dot_general(a.astype(BF16), b.astype(BF16), _TN, preferred_element_type=F32)


def _sigmoid(x):
    return 1.0 / (1.0 + jnp.exp(-x))


def _softplus(x):
    return jnp.maximum(x, 0.0) + jnp.log(1.0 + jnp.exp(-jnp.abs(x)))


def _rms(x, gain):
    return x * lax.rsqrt(jnp.mean(x * x, axis=-1, keepdims=True) + EPS) * gain


def _tri(c, kind):
    r = lax.broadcasted_iota(jnp.int32, (c, c), 0)
    k = lax.broadcasted_iota(jnp.int32, (c, c), 1)
    return {"lower": r >= k, "upper": r <= k, "eye": r == k}[kind]


def _bf16_mask(mask):
    return mask.astype(F32).astype(BF16)


def _split_f32(x):
    hi = x.astype(BF16).astype(F32)
    return hi, x - hi


def _hi_lo_rows(x):
    hi, lo = _split_f32(x)
    return jnp.concatenate([hi.astype(BF16), lo.astype(BF16)], axis=0)


def _hi_lo_lanes(x):
    hi, lo = _split_f32(x)
    return jnp.concatenate([hi, lo], axis=1).astype(BF16)


def _resident(shape):
    nd = len(shape)
    return pl.BlockSpec(shape, lambda *_: (0,) * nd, pipeline_mode=pl.Buffered(1))


def _run(*gens, lead=None, ratio=1):
    def step(g):
        try:
            next(g)
            return True
        except StopIteration:
            return False

    gens = list(gens)
    issued = 0
    if lead is not None and not step(lead):
        lead = None
    while gens:
        for g in list(gens):
            if not step(g):
                gens.remove(g)
                continue
            issued += 1
            if lead is not None and issued % ratio == 0 and not step(lead):
                lead = None
    while lead is not None and step(lead):
        pass


def _head_select(off, n_heads):
    r = jnp.arange(2 * LANES)[:, None] % LANES
    c = jnp.arange(n_heads * LANES)[None, :] // LANES
    return (r == off + c).astype(BF16)


def _conv_silu(full_ref, cw_ref, r0, rows, c0):
    base = r0 + CONV_HDR - (CONV_W - 1)
    acc = full_ref[base:base + rows, c0:c0 + LANES] * cw_ref[0:1, c0:c0 + LANES]
    for j in range(1, CONV_W):
        acc = acc + full_ref[base + j:base + j + rows, c0:c0 + LANES] * cw_ref[j:j + 1, c0:c0 + LANES]
    return acc * _sigmoid(acc)


def _proj_kernel(x_ref, g_ref, wz_ref, ws_ref, z_ref, s_ref):
    hb = _rms(x_ref[...], g_ref[...]).astype(BF16)
    step = 512
    for c0 in range(0, Z_WIDTH, step):
        zc = jnp.dot(hb, wz_ref[:, c0:c0 + step], preferred_element_type=F32)
        if c0 in (Z_BG, Z_CG):
            zc = zc * _sigmoid(zc)
        elif c0 == Z_DO:
            zc = _sigmoid(zc)
        z_ref[:, c0:c0 + step] = zc.astype(BF16)
    s_ref[...] = jnp.dot(hb, ws_ref[...], preferred_element_type=F32)


def _project(x2d, gain, wz, ws):
    n = x2d.shape[0]
    r = min(ROW_TILE, n)
    return pl.pallas_call(
        _proj_kernel,
        grid=(n // r,),
        in_specs=[pl.BlockSpec((r, D_MODEL), lambda i: (i, 0)),
                  _resident((1, D_MODEL)), _resident((D_MODEL, Z_WIDTH)), _resident((D_MODEL, LANES))],
        out_specs=[pl.BlockSpec((r, Z_WIDTH), lambda i: (i, 0)), pl.BlockSpec((r, LANES), lambda i: (i, 0))],
        out_shape=[jax.ShapeDtypeStruct((n, Z_WIDTH), BF16), jax.ShapeDtypeStruct((n, LANES), F32)],
        compiler_params=pltpu.CompilerParams(dimension_semantics=("parallel",), vmem_limit_bytes=VMEM_LIMIT),
        name="proj",
    )(x2d, gain, wz, ws)


def _pool_stages(u_ref, pw_ref, sc_ref, full_ref, store_y, pos, tt):
    hdr = POOL_HDR
    full_ref[hdr:hdr + tt, :] = u_ref[0].astype(F32)
    yield
    for gi, w in enumerate(POOL_WINDOWS):
        cols = slice(gi * A_GROUP, (gi + 1) * A_GROUP)
        u = full_ref[hdr:hdr + tt, cols]
        acc = u
        for j in range(1, w):
            acc = acc + full_ref[hdr - j:hdr - j + tt, cols]
        cnt = jnp.minimum(pos + 1, w).astype(F32)
        diff = acc / cnt - u
        y = _bdot(diff, pw_ref[gi]) * sc_ref[:, cols]
        store_y(slice(0, tt), cols, y.astype(BF16))
        yield


def _gdn_stages(q_ref, k_ref, v_ref, g_ref, sm_ref, cw_ref, alog_ref, dt_ref, nrm_ref, selg_ref, selb_ref,
                s_ref, full_ref, store_y, tt, chunk):
    hdr = CONV_HDR
    full_ref[hdr:hdr + tt, 0:B_QK] = q_ref[0].astype(F32)
    full_ref[hdr:hdr + tt, B_QK:2 * B_QK] = k_ref[0].astype(F32)
    full_ref[hdr:hdr + tt, 2 * B_QK:] = v_ref[0].astype(F32)
    yield
    rep = LANES // chunk
    row = lax.broadcasted_iota(jnp.int32, (chunk, LANES), 0)
    col = lax.broadcasted_iota(jnp.int32, (chunk, LANES), 1) & (chunk - 1)
    causal = row >= col
    strict = row > col
    eye = (row == col).astype(F32)
    tril2 = jnp.concatenate([_bf16_mask(causal[:, :chunk])] * 2, axis=1)
    triu2 = jnp.concatenate([_bf16_mask(row <= col)] * 2, axis=0)
    ones = jnp.ones((LANES, LANES), BF16)
    heads = range(B_HEADS)
    chunks = range(tt // chunk)
    ch = [(c, h) for c in chunks for h in heads]
    rows = [slice(c * chunk, (c + 1) * chunk) for c in chunks]
    n_steps = int(math.log2(chunk)) - 1

    q, k, v = {}, {}, {}
    for c in chunks:
        for h in heads:
            q[c, h] = _conv_silu(full_ref, cw_ref, c * chunk, chunk, h * B_DK)
            k[c, h] = _conv_silu(full_ref, cw_ref, c * chunk, chunk, B_QK + h * B_DK)
            v[c, h] = _conv_silu(full_ref, cw_ref, c * chunk, chunk, 2 * B_QK + h * B_DV)
        yield
    for c in chunks:
        sq = jnp.concatenate([q[c, h] * q[c, h] for h in heads] + [k[c, h] * k[c, h] for h in heads], axis=0)
        ssq = jnp.dot(sq.astype(BF16), ones, preferred_element_type=F32)
        for h in heads:
            q[c, h] = q[c, h] * (lax.rsqrt(ssq[h * chunk:(h + 1) * chunk, :] + EPS) * (B_DK ** -0.5))
            k[c, h] = k[c, h] * lax.rsqrt(ssq[(B_HEADS + h) * chunk:(B_HEADS + h + 1) * chunk, :] + EPS)
    yield
    sm = [sm_ref[0, rows[c], :] for c in chunks]
    log_alpha = [alog_ref[...] * _softplus(sm[c] + dt_ref[...]) for c in chunks]
    beta = [_sigmoid(sm[c]) for c in chunks]
    la2 = [_hi_lo_rows(log_alpha[c]) for c in chunks]
    g_col = [jnp.dot(tril2, la2[c], preferred_element_type=F32) for c in chunks]
    g_row = [lax.dot_general(la2[c], triu2, _TN, preferred_element_type=F32) for c in chunks]
    yield
    g_b = jnp.dot(_hi_lo_lanes(jnp.concatenate(g_col, axis=0)), selg_ref[...], preferred_element_type=F32)
    b_b = jnp.dot(_hi_lo_lanes(jnp.concatenate(beta, axis=0)), selb_ref[...], preferred_element_type=F32)
    gi = {(c, h): g_b[rows[c], h * LANES:(h + 1) * LANES] for c, h in ch}
    bt = {(c, h): b_b[rows[c], h * LANES:(h + 1) * LANES] for c, h in ch}
    gj = {(c, h): g_row[c][S_BA + h:S_BA + h + 1, :] for c, h in ch}
    yield
    decay, kb, eg, low = {}, {}, {}, {}
    for c in chunks:
        for i in [(c, h) for h in heads]:
            decay[i] = jnp.exp(jnp.where(causal, gi[i] - gj[i], -jnp.inf))
            kb[i] = k[i] * bt[i]
            eg[i] = jnp.exp(gi[i])
            low[i] = jnp.where(strict, _bdot_nt(kb[i], jnp.concatenate([k[i]] * rep, axis=0)) * decay[i], 0.0)
        yield
    lb = {i: low[i].astype(BF16) for i in ch}
    m = {i: jnp.dot(lb[i][:, :chunk], lb[i], preferred_element_type=F32) for i in ch}
    tinv = {i: eye - low[i] for i in ch}
    yield
    for step in range(n_steps):
        mb = {i: m[i].astype(BF16) for i in ch}
        tb = {i: tinv[i].astype(BF16) for i in ch}
        if step == n_steps - 1:
            tinv = {i: tinv[i] + jnp.dot(mb[i][:, :chunk], tb[i], preferred_element_type=F32) for i in ch}
        else:
            out = {i: jnp.dot(mb[i][:, :chunk], jnp.concatenate([mb[i], tb[i]], axis=1),
                              preferred_element_type=F32) for i in ch}
            m = {i: out[i][:, :LANES] for i in ch}
            tinv = {i: tinv[i] + out[i][:, LANES:] for i in ch}
        yield
    sol, attn, g_last, k_tail, q_dec = {}, {}, {}, {}, {}
    for c in chunks:
        for i in [(c, h) for h in heads]:
            sol[i] = _bdot(tinv[i][:, :chunk], jnp.concatenate([v[i] * bt[i], kb[i] * eg[i]], axis=-1))
            attn[i] = jnp.where(causal[:, :chunk], _bdot_nt(q[i], k[i]) * decay[i][:, :chunk], 0.0)
            g_last[i] = gi[i][chunk - 1:chunk, :]
            k_tail[i] = k[i] * jnp.exp(g_last[i] - gi[i])
            q_dec[i] = q[i] * eg[i]
        yield
    s = [s_ref[h] for h in heads]
    o = {}
    for c in chunks:
        ws = [_bdot(sol[c, h][:, B_DV:], s[h]) for h in heads]
        qs = [_bdot(q_dec[c, h], s[h]) for h in heads]
        yield
        v_new = [sol[c, h][:, :B_DV] - ws[h] for h in heads]
        for h in heads:
            o[c, h] = qs[h] + _bdot(attn[c, h], v_new[h])
        s = [s[h] * jnp.exp(g_last[c, h]) + _bdot_tn(k_tail[c, h], v_new[h]) for h in heads]
        yield
    for h in heads:
        s_ref[h] = s[h]
    for c in chunks:
        oo = jnp.concatenate([o[c, h] * o[c, h] for h in heads], axis=0)
        ms = jnp.dot(oo.astype(BF16), ones, preferred_element_type=F32) * (1.0 / B_DV)
        for h in heads:
            cols = slice(h * B_DV, (h + 1) * B_DV)
            gate = g_ref[0, rows[c], cols].astype(F32)
            y = o[c, h] * lax.rsqrt(ms[h * chunk:(h + 1) * chunk, :] + EPS) * nrm_ref[...] * gate
            store_y(rows[c], cols, y.astype(BF16))
        yield


def _gla_stages(q_ref, k_ref, v_ref, g_ref, sm_ref, wup_ref, bup_ref, nrm_ref, s_ref, store_y, tt, chunk):
    causal = _tri(chunk, "lower")
    tril2 = jnp.concatenate([_bf16_mask(causal)] * 2, axis=1)
    ones = jnp.ones((2 * chunk, LANES), BF16)
    heads = range(C_HEADS)
    pairs = range(C_HEADS // 2)
    chunks = range(tt // chunk)
    rows = [slice(c * chunk, (c + 1) * chunk) for c in chunks]
    vs = [slice(h * C_DV, (h + 1) * C_DV) for h in heads]
    sm_sp = [_split_f32(sm_ref[0, rows[c], :]) for c in chunks]
    lr = [jnp.dot(jnp.concatenate([sm_sp[c][0], sm_sp[c][1], sm_sp[c][0]], axis=1).astype(BF16), wup_ref[...],
                  preferred_element_type=F32) + bup_ref[...] for c in chunks]
    log_a = [-_softplus(-lr[c]) / GLA_NORMALIZER for c in chunks]
    yield
    la2 = [_hi_lo_rows(log_a[c]) for c in chunks]
    bcum = [jnp.dot(tril2, la2[c], preferred_element_type=F32) for c in chunks]
    d_col = [jnp.exp(lax.dot_general(la2[c], ones, _TN, preferred_element_type=F32)) for c in chunks]
    yield
    mid = [bcum[c][chunk // 2:chunk // 2 + 1, :] for c in chunks]
    last = [bcum[c][chunk - 1:chunk, :] for c in chunks]
    qf = [q_ref[0, rows[c], :].astype(F32) * (C_DK ** -0.5) for c in chunks]
    kf = [k_ref[0, rows[c], :].astype(F32) for c in chunks]
    q_in = [qf[c] * jnp.exp(bcum[c] - mid[c]) for c in chunks]
    k_in = [kf[c] * jnp.exp(mid[c] - bcum[c]) for c in chunks]
    q_x = [qf[c] * jnp.exp(bcum[c]) for c in chunks]
    k_t = [kf[c] * jnp.exp(last[c] - bcum[c]) for c in chunks]
    yield
    lane = lax.broadcasted_iota(jnp.int32, (chunk, LANES), 1)
    half = [(lane < C_DK) if h % 2 == 0 else (lane >= C_DK) for h in heads]
    grp = [slice((h // 2) * LANES, (h // 2 + 1) * LANES) for h in heads]
    v = [[v_ref[0, rows[c], vs[h]] for h in heads] for c in chunks]
    attn = [[jnp.where(causal, _bdot_nt(jnp.where(half[h], q_in[c][:, grp[h]], 0.0), k_in[c][:, grp[h]]), 0.0)
             for h in heads] for c in chunks]
    yield
    o_in = [[_bdot(attn[c][h], v[c][h]) for h in heads] for c in chunks]
    upd = [[_bdot_tn(jnp.where(half[h], k_t[c][:, grp[h]], 0.0), v[c][h]) for h in heads] for c in chunks]
    yield
    s = [[s_ref[p] for p in pairs]]
    for c in chunks:
        s.append([s[c][p] * d_col[c][p * LANES:(p + 1) * LANES, :] + upd[c][2 * p] + upd[c][2 * p + 1]
                  for p in pairs])
    for p in pairs:
        s_ref[p] = s[-1][p]
    o = [[_bdot(jnp.where(half[h], q_x[c][:, grp[h]], 0.0), s[c][h // 2]) + o_in[c][h] for h in heads]
         for c in chunks]
    yield
    inv_dv = jnp.full((LANES, LANES), 1.0 / C_DV, BF16)
    for c in chunks:
        oo = jnp.concatenate([o[c][h] * o[c][h] for h in heads], axis=0)
        ms = jnp.dot(oo.astype(BF16), inv_dv, preferred_element_type=F32)
        for h in heads:
            gate = g_ref[0, rows[c], vs[h]].astype(F32)
            y = o[c][h] * lax.rsqrt(ms[h * chunk:(h + 1) * chunk, :] + EPS) * nrm_ref[...] * gate
            store_y(rows[c], vs[h], y.astype(BF16))
        yield


def _mlstm_stages(q_ref, k_ref, v_ref, og_ref, sm_ref, bi_ref, bf_ref, nrm_ref, self_ref, seli_ref,
                  cn_ref, m_ref, store_y, tt, chunk):
    causal = _tri(chunk, "lower")
    tril2 = jnp.concatenate([_bf16_mask(causal)] * 2, axis=1)
    triu2 = jnp.concatenate([_bf16_mask(_tri(chunk, "upper"))] * 2, axis=0)
    eye2 = jnp.concatenate([_bf16_mask(_tri(chunk, "eye"))] * 2, axis=0)
    ones = jnp.ones((chunk, LANES), BF16)
    heads = range(D_HEADS)
    pairs = range(D_HEADS // 2)
    chunks = range(tt // chunk)
    n_chunks = len(chunks)
    ch = [(c, h) for c in chunks for h in heads]
    rows = [slice(c * chunk, (c + 1) * chunk) for c in chunks]
    vs = [slice(h * D_DV, (h + 1) * D_DV) for h in heads]
    sm = [sm_ref[0, rows[c], :] for c in chunks]
    i_pre = [sm[c] + bi_ref[...] for c in chunks]
    log_f = [-_softplus(-(sm[c] + bf_ref[...])) for c in chunks]
    lf2 = [_hi_lo_rows(log_f[c]) for c in chunks]
    ip2 = [_hi_lo_rows(i_pre[c]) for c in chunks]
    b_col = [jnp.dot(tril2, lf2[c], preferred_element_type=F32) for c in chunks]
    b_row = [lax.dot_general(lf2[c], triu2, _TN, preferred_element_type=F32) for c in chunks]
    i_row = [lax.dot_general(ip2[c], eye2, _TN, preferred_element_type=F32) for c in chunks]
    yield
    b_b = jnp.dot(_hi_lo_lanes(jnp.concatenate(b_col, axis=0)), self_ref[...], preferred_element_type=F32)
    i_b = jnp.dot(_hi_lo_lanes(jnp.concatenate(i_pre, axis=0)), seli_ref[...], preferred_element_type=F32)
    bi = {(c, h): b_b[rows[c], h * LANES:(h + 1) * LANES] for c, h in ch}
    ii = {(c, h): i_b[rows[c], h * LANES:(h + 1) * LANES] for c, h in ch}
    yield
    lane = lax.broadcasted_iota(jnp.int32, (chunk, LANES), 1)
    half = [(lane < D_DK) if h % 2 == 0 else (lane >= D_DK) for h in heads]
    grp = [slice((h // 2) * LANES, (h // 2 + 1) * LANES) for h in heads]
    qf = [q_ref[0, rows[c], :].astype(F32) * (D_DK ** -0.5) for c in chunks]
    kf = [k_ref[0, rows[c], :].astype(F32) for c in chunks]
    q, v1, log_w, lw_max, qk = {}, {}, {}, {}, {}
    for c in chunks:
        for h in heads:
            q[c, h] = jnp.where(half[h], qf[c][:, grp[h]], 0.0)
            v1[c, h] = jnp.concatenate([v_ref[0, rows[c], vs[h]], ones], axis=-1)
            log_w[c, h] = jnp.where(causal, bi[c, h][:, :chunk] - b_row[c][S_DF + h:S_DF + h + 1, :]
                                    + i_row[c][S_DI + h:S_DI + h + 1, :], -jnp.inf)
            lw_max[c, h] = jnp.max(log_w[c, h], axis=-1, keepdims=True)
            qk[c, h] = _bdot_nt(q[c, h], kf[c][:, grp[h]])
        yield
    b_last = {i: bi[i][chunk - 1:chunk, :] for i in ch}
    m_s = {(0, h): m_ref[h:h + 1, :] for h in heads}
    for c, h in ch:
        m_s[c + 1, h] = jnp.maximum(b_last[c, h] + m_s[c, h], lw_max[c, h][chunk - 1:chunk, :])
    for h in heads:
        m_ref[h:h + 1, :] = m_s[n_chunks, h]
    m_t, w_inter, pv, upd = {}, {}, {}, {}
    for c in chunks:
        for h in heads:
            log_inter = bi[c, h] + m_s[c, h]
            m_t[c, h] = jnp.maximum(log_inter, lw_max[c, h])
            w_inter[c, h] = jnp.exp(log_inter - m_t[c, h])
            p_w = jnp.exp(log_w[c, h] - m_t[c, h][:, :chunk]) * qk[c, h]
            pv[c, h] = _bdot(p_w, v1[c, h])
            w_k = jnp.exp(b_last[c, h] - bi[c, h] + ii[c, h] - m_s[c + 1, h])
            upd[c, h] = _bdot_tn(jnp.where(half[h], kf[c][:, grp[h]] * w_k, 0.0), v1[c, h])
        yield
    top = lax.broadcasted_iota(jnp.int32, (2 * D_DK, LANES), 0) < D_DK
    cn = {(0, p): cn_ref[p] for p in pairs}
    for c in chunks:
        for p in pairs:
            w_a, w_b = (jnp.exp(b_last[c, h] + m_s[c, h] - m_s[c + 1, h]) for h in (2 * p, 2 * p + 1))
            w_state = jnp.where(top, w_a, w_b)
            cn[c + 1, p] = jnp.concatenate([w_state, w_state], axis=1) * cn[c, p] + upd[c, 2 * p] + upd[c, 2 * p + 1]
    for p in pairs:
        cn_ref[p] = cn[n_chunks, p]
    qcn = {(c, h): _bdot(q[c, h], cn[c, h // 2]) for c, h in ch}
    yield
    inv_dv = jnp.full((LANES, LANES), 1.0 / D_DV, BF16)
    for c in chunks:
        hh = []
        for h in heads:
            num = w_inter[c, h] * qcn[c, h][:, :D_DV] + pv[c, h][:, :D_DV]
            den = w_inter[c, h] * qcn[c, h][:, D_DV:] + pv[c, h][:, D_DV:]
            hh.append(num / jnp.maximum(jnp.abs(den), jnp.exp(-m_t[c, h])))
        ms = jnp.dot(jnp.concatenate([x * x for x in hh], axis=0).astype(BF16), inv_dv, preferred_element_type=F32)
        for h in heads:
            gate = og_ref[0, rows[c], vs[h]].astype(F32)
            y = gate * (hh[h] * lax.rsqrt(ms[h * chunk:(h + 1) * chunk, :] + EPS) * nrm_ref[...])
            store_y(rows[c], vs[h], y.astype(BF16))
        yield


def _tail_stages(x_ref, read_y, gmix_ref, wgate_ref, bgate_ref, wbr_ref, wout_ref, gffn_ref, wfg_ref, wfu_ref,
                 wfd_ref, gfin_ref, o_ref, final):
    x = x_ref[...]
    hb = _rms(x, gmix_ref[...]).astype(BF16)
    yield
    merged = []
    for c0 in range(0, D_MODEL, TAIL_COLS):
        cols = slice(c0, c0 + TAIL_COLS)
        acc = None
        for i in range(N_BRANCH):
            gate = _sigmoid(jnp.dot(hb, wgate_ref[i, :, cols], preferred_element_type=F32) + bgate_ref[i, :, cols])
            term = gate * jnp.dot(read_y(i), wbr_ref[i, :, cols], preferred_element_type=F32)
            acc = term if acc is None else acc + term
            yield
        merged.append(acc.astype(BF16))
    mb = jnp.concatenate(merged, axis=1)
    x1 = []
    for c0 in range(0, D_MODEL, TAIL_COLS):
        cols = slice(c0, c0 + TAIL_COLS)
        x1.append(x[:, cols] + jnp.dot(mb, wout_ref[:, cols], preferred_element_type=F32))
        yield
    x = jnp.concatenate(x1, axis=1)
    h2 = _rms(x, gffn_ref[...]).astype(BF16)

    def hidden(c):
        a = jnp.dot(h2, wfg_ref[:, c], preferred_element_type=F32)
        u = jnp.dot(h2, wfu_ref[:, c], preferred_element_type=F32)
        return (a * _sigmoid(a) * u).astype(BF16)

    bounds = list(range(0, FFN_HIDDEN, FFN_CHUNK)) + [FFN_HIDDEN]
    pieces = [slice(lo, hi) for lo, hi in zip(bounds[:-1], bounds[1:])]
    f = hidden(pieces[0])
    yield
    for n, piece in enumerate(pieces):
        f_next = hidden(pieces[n + 1]) if n + 1 < len(pieces) else None
        x = x + jnp.dot(f, wfd_ref[piece, :], preferred_element_type=F32)
        f = f_next
        yield
    if final:
        x = _rms(x, gfin_ref[...])
    o_ref[...] = x


def _pool_kernel(u_ref, buf_ref, pw_ref, sc_ref, y_ref, nbuf_ref, full_ref, *, tt, pos0):
    t = pl.program_id(1)
    hdr = POOL_HDR

    @pl.when(t == 0)
    def _():
        full_ref[1:hdr, :] = buf_ref[0]

    @pl.when(t > 0)
    def _():
        full_ref[1:hdr, :] = full_ref[tt + 1:tt + hdr, :]

    def store_y(rows, cols, val):
        y_ref[0, rows, cols] = val

    pos = pos0 + t * tt + lax.broadcasted_iota(jnp.int32, (tt, 1), 0)
    _run(_pool_stages(u_ref, pw_ref, sc_ref, full_ref, store_y, pos, tt))
    nbuf_ref[0] = full_ref[tt + 1:tt + hdr, :]


def _pool_mixer(z3, buf, pool_w, scale, pos0):
    b, t_len, _ = z3.shape
    tt = min(TIME_TILE, t_len)
    return pl.pallas_call(
        functools.partial(_pool_kernel, tt=tt, pos0=pos0),
        grid=(b, t_len // tt),
        in_specs=[pl.BlockSpec((1, tt, A_WIDTH), lambda i, j: (i, j, Z_A // A_WIDTH)),
                  pl.BlockSpec((1, POOL_BUF, A_WIDTH), lambda i, j: (i, 0, 0)),
                  _resident((4, A_GROUP, A_GROUP)), _resident((1, A_WIDTH))],
        out_specs=[pl.BlockSpec((1, tt, A_WIDTH), lambda i, j: (i, j, 0)),
                   pl.BlockSpec((1, POOL_BUF, A_WIDTH), lambda i, j: (i, 0, 0))],
        out_shape=[jax.ShapeDtypeStruct((b, t_len, A_WIDTH), BF16),
                   jax.ShapeDtypeStruct((b, POOL_BUF, A_WIDTH), F32)],
        scratch_shapes=[pltpu.VMEM((POOL_HDR + tt, A_WIDTH), F32)],
        compiler_params=pltpu.CompilerParams(dimension_semantics=("parallel", "arbitrary"),
                                             vmem_limit_bytes=VMEM_LIMIT),
        name="pool",
    )(z3, buf, pool_w, scale)


def _gdn_kernel(q_ref, k_ref, v_ref, g_ref, sm_ref, cbuf_ref, s_in_ref, cw_ref, alog_ref, dt_ref, nrm_ref,
                selg_ref, selb_ref, y_ref, ncbuf_ref, s_ref, full_ref, *, tt, chunk):
    t = pl.program_id(1)
    hdr = CONV_HDR

    @pl.when(t == 0)
    def _():
        full_ref[hdr - 3:hdr, :] = cbuf_ref[0]
        s_ref[0] = s_in_ref[0]

    @pl.when(t > 0)
    def _():
        full_ref[hdr - 3:hdr, :] = full_ref[tt + hdr - 3:tt + hdr, :]

    def store_y(rows, cols, val):
        y_ref[0, rows, cols] = val

    _run(_gdn_stages(q_ref, k_ref, v_ref, g_ref, sm_ref, cw_ref, alog_ref, dt_ref, nrm_ref, selg_ref, selb_ref,
                     s_ref.at[0], full_ref, store_y, tt, chunk))
    ncbuf_ref[0] = full_ref[tt + hdr - 3:tt + hdr, :]


def _gdn_mixer(z3, sm3, cbuf, s0, conv_w, alog_row, dt_row, norm_row, chunk):
    b, t_len, _ = z3.shape
    tt = min(TIME_TILE, t_len)
    wide = lambda off: pl.BlockSpec((1, tt, 512), lambda i, j: (i, j, off // 512))
    sel = _resident((2 * LANES, B_HEADS * LANES))
    return pl.pallas_call(
        functools.partial(_gdn_kernel, tt=tt, chunk=chunk),
        grid=(b, t_len // tt),
        in_specs=[wide(Z_BQ), wide(Z_BK), wide(Z_BV), wide(Z_BG),
                  pl.BlockSpec((1, tt, LANES), lambda i, j: (i, j, 0)),
                  pl.BlockSpec((1, CONV_W - 1, B_CONV_CH), lambda i, j: (i, 0, 0)),
                  pl.BlockSpec((1, B_HEADS, B_DK, B_DV), lambda i, j: (i, 0, 0, 0)),
                  _resident((CONV_W, B_CONV_CH)), _resident((1, LANES)), _resident((1, LANES)),
                  _resident((1, B_DV)), sel, sel],
        out_specs=[pl.BlockSpec((1, tt, B_V), lambda i, j: (i, j, 0)),
                   pl.BlockSpec((1, CONV_W - 1, B_CONV_CH), lambda i, j: (i, 0, 0)),
                   pl.BlockSpec((1, B_HEADS, B_DK, B_DV), lambda i, j: (i, 0, 0, 0))],
        out_shape=[jax.ShapeDtypeStruct((b, t_len, B_V), BF16),
                   jax.ShapeDtypeStruct((b, CONV_W - 1, B_CONV_CH), F32),
                   jax.ShapeDtypeStruct((b, B_HEADS, B_DK, B_DV), F32)],
        scratch_shapes=[pltpu.VMEM((CONV_HDR + tt, B_CONV_CH), F32)],
        compiler_params=pltpu.CompilerParams(dimension_semantics=("parallel", "arbitrary"),
                                             vmem_limit_bytes=VMEM_LIMIT),
        name="gdn",
    )(z3, z3, z3, z3, sm3, cbuf, s0, conv_w, alog_row, dt_row, norm_row,
      _head_select(S_BA, B_HEADS), _head_select(S_BB, B_HEADS))


def _gla_kernel(q_ref, k_ref, v_ref, g_ref, sm_ref, s_in_ref, wup_ref, bup_ref, nrm_ref,
                y_ref, s_ref, *, tt, chunk):
    @pl.when(pl.program_id(1) == 0)
    def _():
        s_ref[0] = s_in_ref[0]

    def store_y(rows, cols, val):
        y_ref[0, rows, cols] = val

    _run(_gla_stages(q_ref, k_ref, v_ref, g_ref, sm_ref, wup_ref, bup_ref, nrm_ref, s_ref.at[0], store_y, tt, chunk))


def _gla_mixer(z3, sm3, s0, wup_pack, bup_row, norm_row, chunk):
    b, t_len, _ = z3.shape
    tt = min(TIME_TILE, t_len)
    blk = lambda w, off: pl.BlockSpec((1, tt, w), lambda i, j: (i, j, off // w))
    state = pl.BlockSpec((1, C_HEADS // 2, 2 * C_DK, C_DV), lambda i, j: (i, 0, 0, 0))
    return pl.pallas_call(
        functools.partial(_gla_kernel, tt=tt, chunk=chunk),
        grid=(b, t_len // tt),
        in_specs=[blk(C_QK, Z_CQ), blk(C_QK, Z_CK), blk(C_V, Z_CV), blk(C_V, Z_CG),
                  pl.BlockSpec((1, tt, LANES), lambda i, j: (i, j, 0)), state,
                  _resident((3 * LANES, C_QK)), _resident((1, C_QK)), _resident((1, C_DV))],
        out_specs=[pl.BlockSpec((1, tt, C_V), lambda i, j: (i, j, 0)), state],
        out_shape=[jax.ShapeDtypeStruct((b, t_len, C_V), BF16),
                   jax.ShapeDtypeStruct((b, C_HEADS // 2, 2 * C_DK, C_DV), F32)],
        compiler_params=pltpu.CompilerParams(dimension_semantics=("parallel", "arbitrary"),
                                             vmem_limit_bytes=VMEM_LIMIT),
        name="gla",
    )(z3, z3, z3, z3, sm3, s0, wup_pack, bup_row, norm_row)


def _mlstm_kernel(q_ref, k_ref, v_ref, og_ref, sm_ref, cn_in_ref, m_in_ref, bi_ref, bf_ref, nrm_ref,
                  self_ref, seli_ref, y_ref, cn_ref, m_ref, *, tt, chunk):
    @pl.when(pl.program_id(1) == 0)
    def _():
        cn_ref[0] = cn_in_ref[0]
        m_ref[0] = m_in_ref[0]

    def store_y(rows, cols, val):
        y_ref[0, rows, cols] = val

    _run(_mlstm_stages(q_ref, k_ref, v_ref, og_ref, sm_ref, bi_ref, bf_ref, nrm_ref, self_ref, seli_ref,
                       cn_ref.at[0], m_ref.at[0], store_y, tt, chunk))


def _mlstm_mixer(z3, sm3, cn0, m0, bi_row, bf_row, norm_row, chunk):
    b, t_len, _ = z3.shape
    tt = min(TIME_TILE, t_len)
    blk = lambda w, off: pl.BlockSpec((1, tt, w), lambda i, j: (i, j, off // w))
    cn_spec = pl.BlockSpec((1, D_HEADS // 2, 2 * D_DK, 2 * D_DV), lambda i, j: (i, 0, 0, 0))
    m_spec = pl.BlockSpec((1, 8, LANES), lambda i, j: (i, 0, 0))
    sel = _resident((2 * LANES, D_HEADS * LANES))
    return pl.pallas_call(
        functools.partial(_mlstm_kernel, tt=tt, chunk=chunk),
        grid=(b, t_len // tt),
        in_specs=[blk(D_QK, Z_DQ), blk(D_QK, Z_DK), blk(D_V, Z_DV), blk(D_V, Z_DO),
                  pl.BlockSpec((1, tt, LANES), lambda i, j: (i, j, 0)), cn_spec, m_spec,
                  _resident((1, LANES)), _resident((1, LANES)), _resident((1, D_DV)), sel, sel],
        out_specs=[pl.BlockSpec((1, tt, D_V), lambda i, j: (i, j, 0)), cn_spec, m_spec],
        out_shape=[jax.ShapeDtypeStruct((b, t_len, D_V), BF16),
                   jax.ShapeDtypeStruct((b, D_HEADS // 2, 2 * D_DK, 2 * D_DV), F32),
                   jax.ShapeDtypeStruct((b, 8, LANES), F32)],
        compiler_params=pltpu.CompilerParams(dimension_semantics=("parallel", "arbitrary"),
                                             vmem_limit_bytes=VMEM_LIMIT),
        name="mlstm",
    )(z3, z3, z3, z3, sm3, cn0, m0, bi_row, bf_row, norm_row,
      _head_select(S_DF, D_HEADS), _head_select(S_DI, D_HEADS))


def _tail_kernel(x_ref, ya_ref, yb_ref, yc_ref, yd_ref, gmix_ref, wgate_ref, bgate_ref, wbr_ref, wout_ref,
                 gffn_ref, wfg_ref, wfu_ref, wfd_ref, gfin_ref, o_ref, *, final):
    ys = (ya_ref, yb_ref, yc_ref, yd_ref)
    _run(_tail_stages(x_ref, lambda i: ys[i][...], gmix_ref, wgate_ref, bgate_ref, wbr_ref, wout_ref, gffn_ref,
                      wfg_ref, wfu_ref, wfd_ref, gfin_ref, o_ref, final))


def _tail_weight_specs():
    return [_resident((1, D_MODEL)), _resident((N_BRANCH, D_MODEL, D_MODEL)), _resident((N_BRANCH, 1, D_MODEL)),
            _resident((N_BRANCH, BR_WIDTH, D_MODEL)), _resident((D_MODEL, D_MODEL)), _resident((1, D_MODEL)),
            _resident((D_MODEL, FFN_HIDDEN)), _resident((D_MODEL, FFN_HIDDEN)),
            _resident((FFN_HIDDEN, D_MODEL)), _resident((1, D_MODEL))]


def _tail_weights(lp, gfin):
    return (lp["gmix"], lp["wgate"], lp["bgate"], lp["wbr"], lp["wout"], lp["gffn"], lp["wfg"], lp["wfu"], lp["wfd"],
            gfin)


def _tail(x2d, ys, lp, gfin, final):
    n = x2d.shape[0]
    r = min(ROW_TILE, n)
    row = lambda w: pl.BlockSpec((r, w), lambda i: (i, 0))
    return pl.pallas_call(
        functools.partial(_tail_kernel, final=final),
        grid=(n // r,),
        in_specs=[row(D_MODEL)] + [row(BR_WIDTH)] * N_BRANCH + _tail_weight_specs(),
        out_specs=row(D_MODEL),
        out_shape=jax.ShapeDtypeStruct((n, D_MODEL), F32),
        compiler_params=pltpu.CompilerParams(dimension_semantics=("parallel",), vmem_limit_bytes=VMEM_LIMIT),
        name="tail",
    )(x2d, *ys, *_tail_weights(lp, gfin))


def _mix_tail_kernel(a_ref, bq_ref, bk_ref, bv_ref, bg_ref, cq_ref, ck_ref, cv_ref, cg_ref,
                     dq_ref, dk_ref, dv_ref, do_ref, sm_ref,
                     abuf_in, cbuf_in, gs_in, cs_in, cn_in, m_in, x_ref,
                     pw_ref, psc_ref, cw_ref, alog_ref, dt_ref, gnrm_ref, selg_ref, selb_ref,
                     wup_ref, bup_ref, cnrm_ref, bi_ref, bf_ref, dnrm_ref, self_ref, seli_ref,
                     gmix_ref, wgate_ref, bgate_ref, wbr_ref, wout_ref, gffn_ref, wfg_ref, wfu_ref, wfd_ref, gfin_ref,
                     o_ref, abuf_out, cbuf_out, gs_out, cs_out, cn_out, m_out,
                     ybuf, pfull, cfull, gs, cs, cn, m, *, tt, chunk, nt, n_tiles, pos0, final):
    s = pl.program_id(0)
    t = jnp.minimum(s, n_tiles - 1) % nt
    slot = s % 2

    @pl.when(s == 0)
    def _():
        ybuf[1] = jnp.zeros(ybuf.shape[1:], BF16)

    @pl.when(t == 0)
    def _():
        pfull[1:POOL_HDR, :] = abuf_in[0]
        cfull[CONV_HDR - 3:CONV_HDR, :] = cbuf_in[0]
        gs[...] = gs_in[0]
        cs[...] = cs_in[0]
        cn[...] = cn_in[0]
        m[...] = m_in[0]

    @pl.when(t > 0)
    def _():
        pfull[1:POOL_HDR, :] = pfull[tt + 1:tt + POOL_HDR, :]
        cfull[CONV_HDR - 3:CONV_HDR, :] = cfull[tt + CONV_HDR - 3:tt + CONV_HDR, :]

    def store_y(branch):
        def store(rows, cols, val):
            ybuf[slot, rows, slice(branch * BR_WIDTH + cols.start, branch * BR_WIDTH + cols.stop)] = val
        return store

    def read_y(i):
        return ybuf[1 - slot, :, i * BR_WIDTH:(i + 1) * BR_WIDTH]

    pos = pos0 + t * tt + lax.broadcasted_iota(jnp.int32, (tt, 1), 0)
    _run(_gdn_stages(bq_ref, bk_ref, bv_ref, bg_ref, sm_ref, cw_ref, alog_ref, dt_ref, gnrm_ref, selg_ref, selb_ref,
                     gs, cfull, store_y(1), tt, chunk),
         _mlstm_stages(dq_ref, dk_ref, dv_ref, do_ref, sm_ref, bi_ref, bf_ref, dnrm_ref, self_ref, seli_ref,
                       cn, m, store_y(3), tt, chunk),
         _gla_stages(cq_ref, ck_ref, cv_ref, cg_ref, sm_ref, wup_ref, bup_ref, cnrm_ref, cs, store_y(2), tt, chunk),
         _pool_stages(a_ref, pw_ref, psc_ref, pfull, store_y(0), pos, tt),
         lead=_tail_stages(x_ref, read_y, gmix_ref, wgate_ref, bgate_ref, wbr_ref, wout_ref, gffn_ref, wfg_ref,
                           wfu_ref, wfd_ref, gfin_ref, o_ref, final),
         ratio=TAIL_RATIO)

    @pl.when((t == nt - 1) & (s < n_tiles))
    def _():
        abuf_out[0] = pfull[tt + 1:tt + POOL_HDR, :]
        cbuf_out[0] = cfull[tt + CONV_HDR - 3:tt + CONV_HDR, :]
        gs_out[0] = gs[...]
        cs_out[0] = cs[...]
        cn_out[0] = cn[...]
        m_out[0] = m[...]


def _mix_tail(x2d, z3, sm3, states, lp, gfin, pos0, chunk, final):
    b, t_len, _ = z3.shape
    tt = TIME_TILE
    nt = t_len // tt
    n_tiles = b * nt
    a_buf, conv_buf, gdn_s, gla_s, cn0, m0 = states

    def bt(s):
        sm = jnp.minimum(s, n_tiles - 1)
        return sm // nt, sm % nt

    def zblk(w, off):
        return pl.BlockSpec((1, tt, w), lambda s: (*bt(s), off // w))

    def per_stream(shape):
        nd = len(shape)
        return pl.BlockSpec((1,) + shape, lambda s: (bt(s)[0],) + (0,) * nd)

    x_spec = pl.BlockSpec((tt, D_MODEL), lambda s: (jnp.maximum(s - 1, 0), 0))
    state_shapes = [(POOL_BUF, A_WIDTH), (CONV_W - 1, B_CONV_CH), (B_HEADS, B_DK, B_DV),
                    (C_HEADS // 2, 2 * C_DK, C_DV), (D_HEADS // 2, 2 * D_DK, 2 * D_DV), (8, LANES)]
    selb = _resident((2 * LANES, B_HEADS * LANES))
    seld = _resident((2 * LANES, D_HEADS * LANES))
    row128 = _resident((1, LANES))
    outs = pl.pallas_call(
        functools.partial(_mix_tail_kernel, tt=tt, chunk=chunk, nt=nt, n_tiles=n_tiles, pos0=pos0, final=final),
        grid=(n_tiles + 1,),
        in_specs=[zblk(A_WIDTH, Z_A), zblk(512, Z_BQ), zblk(512, Z_BK), zblk(512, Z_BV), zblk(512, Z_BG),
                  zblk(C_QK, Z_CQ), zblk(C_QK, Z_CK), zblk(C_V, Z_CV), zblk(C_V, Z_CG),
                  zblk(D_QK, Z_DQ), zblk(D_QK, Z_DK), zblk(D_V, Z_DV), zblk(D_V, Z_DO),
                  pl.BlockSpec((1, tt, LANES), lambda s: (*bt(s), 0))]
                 + [per_stream(sh) for sh in state_shapes] + [x_spec]
                 + [_resident((4, A_GROUP, A_GROUP)), _resident((1, A_WIDTH)),
                    _resident((CONV_W, B_CONV_CH)), row128, row128, _resident((1, B_DV)), selb, selb,
                    _resident((3 * LANES, C_QK)), _resident((1, C_QK)), _resident((1, C_DV)),
                    row128, row128, _resident((1, D_DV)), seld, seld]
                 + _tail_weight_specs(),
        out_specs=[x_spec] + [per_stream(sh) for sh in state_shapes],
        out_shape=[jax.ShapeDtypeStruct((n_tiles * tt, D_MODEL), F32)]
                  + [jax.ShapeDtypeStruct((b,) + sh, F32) for sh in state_shapes],
        scratch_shapes=[pltpu.VMEM((2, tt, N_BRANCH * BR_WIDTH), BF16),
                        pltpu.VMEM((POOL_HDR + tt, A_WIDTH), F32), pltpu.VMEM((CONV_HDR + tt, B_CONV_CH), F32)]
                       + [pltpu.VMEM(sh, F32) for sh in state_shapes[2:]],
        compiler_params=pltpu.CompilerParams(dimension_semantics=("arbitrary",), vmem_limit_bytes=VMEM_LIMIT),
        name="mix_tail",
    )(*([z3] * 13), sm3, a_buf, conv_buf, gdn_s, gla_s, cn0, m0, x2d,
      lp["pool_w"], lp["pool_scale"], lp["conv_w"], lp["alog_row"], lp["dt_row"], lp["gdn_norm"],
      _head_select(S_BA, B_HEADS), _head_select(S_BB, B_HEADS), lp["wup"], lp["bup"], lp["gla_norm"],
      lp["bi_row"], lp["bf_row"], lp["mlstm_norm"], _head_select(S_DF, D_HEADS), _head_select(S_DI, D_HEADS),
      *_tail_weights(lp, gfin))
    return outs[0], outs[1:]


def _lane_row(vec, off):
    return jnp.zeros((1, LANES), F32).at[0, off:off + vec.shape[0]].set(vec.astype(F32))


def _layer_params(l, p):
    w_in = p["w_in"][l]
    offs = [0]
    for s in IN_SPLITS:
        offs.append(offs[-1] + s)
    piece = lambda i: w_in[:, offs[i]:offs[i + 1]]
    wide_ids = (0, 1, 2, 3, 4, 7, 8, 9, 10, 12, 13, 14, 15)
    wz = jnp.concatenate([piece(i) for i in wide_ids], axis=1).astype(BF16)
    small = jnp.concatenate([piece(5), piece(6), piece(16), piece(17), piece(11)], axis=1)
    ws = jnp.zeros((D_MODEL, LANES), F32).at[:, :small.shape[1]].set(small).astype(BF16)
    wup = jnp.zeros((LANES, C_QK), F32).at[S_LR:S_LR + GLA_RANK].set(p["gla_w_up"][l].astype(F32))
    wup_hi = wup.astype(BF16)
    wup_lo = (wup - wup_hi.astype(F32)).astype(BF16)
    return dict(
        gmix=p["norm_mix"][l].reshape(1, D_MODEL), wz=wz, ws=ws,
        pool_w=p["pool_w"][l].astype(BF16), pool_scale=p["pool_scale"][l].reshape(1, A_WIDTH),
        conv_w=p["gdn_conv_w"][l],
        alog_row=_lane_row(-jnp.exp(p["gdn_a_log"][l].astype(F32)), S_BA),
        dt_row=_lane_row(p["gdn_dt_bias"][l], S_BA),
        gdn_norm=p["gdn_norm"][l].reshape(1, B_DV),
        wup=jnp.concatenate([wup_hi, wup_hi, wup_lo], axis=0),
        bup=p["gla_b_up"][l].reshape(1, C_QK), gla_norm=p["gla_norm"][l].reshape(1, C_DV),
        bi_row=_lane_row(p["mlstm_b_i"][l], S_DI), bf_row=_lane_row(p["mlstm_b_f"][l], S_DF),
        mlstm_norm=p["mlstm_norm"][l].reshape(1, D_DV),
        wgate=p["w_gate"][l].astype(BF16), bgate=p["b_gate"][l].reshape(4, 1, D_MODEL),
        wbr=p["w_branch"][l].astype(BF16), wout=p["w_out"][l].astype(BF16),
        gffn=p["norm_ffn"][l].reshape(1, D_MODEL),
        wfg=p["w_ffn_gate"][l].astype(BF16), wfu=p["w_ffn_up"][l].astype(BF16), wfd=p["w_ffn_down"][l].astype(BF16),
    )


def _layer(x, st, lp, gfin, pos0, chunk, final):
    b, t_len, _ = x.shape
    a_buf, conv_buf, gdn_s, gla_s, ml_c, ml_n, ml_m = st
    x2d = x.reshape(b * t_len, D_MODEL)
    z, sm = _project(x2d, lp["gmix"], lp["wz"], lp["ws"])
    z3 = z.reshape(b, t_len, Z_WIDTH)
    sm3 = sm.reshape(b, t_len, LANES)
    gla_s = gla_s.reshape(b, C_HEADS // 2, 2 * C_DK, C_DV)
    cn0 = jnp.concatenate([ml_c, jnp.broadcast_to(ml_n[..., None], ml_c.shape)], axis=-1)
    cn0 = cn0.reshape(b, D_HEADS // 2, 2 * D_DK, 2 * D_DV)
    m0 = jnp.zeros((b, 8, LANES), F32).at[:, :D_HEADS, :].set(jnp.broadcast_to(ml_m[..., None], (b, D_HEADS, LANES)))
    if t_len % TIME_TILE == 0:
        x_new, (a_new, conv_new, gdn_new, gla_new, cn_new, m_new) = _mix_tail(
            x2d, z3, sm3, (a_buf, conv_buf, gdn_s, gla_s, cn0, m0), lp, gfin, pos0, chunk, final)
    else:
        y_a, a_new = _pool_mixer(z3, a_buf, lp["pool_w"], lp["pool_scale"], pos0)
        y_b, conv_new, gdn_new = _gdn_mixer(z3, sm3, conv_buf, gdn_s, lp["conv_w"], lp["alog_row"], lp["dt_row"],
                                            lp["gdn_norm"], chunk)
        y_c, gla_new = _gla_mixer(z3, sm3, gla_s, lp["wup"], lp["bup"], lp["gla_norm"], chunk)
        y_d, cn_new, m_new = _mlstm_mixer(z3, sm3, cn0, m0, lp["bi_row"], lp["bf_row"], lp["mlstm_norm"], chunk)
        ys = [y.reshape(b * t_len, BR_WIDTH) for y in (y_a, y_b, y_c, y_d)]
        x_new = _tail(x2d, ys, lp, gfin, final)
    gla_new = gla_new.reshape(b, C_HEADS, C_DK, C_DV)
    cn_new = cn_new.reshape(b, D_HEADS, D_DK, 2 * D_DV)
    new_st = (a_new, conv_new, gdn_new, gla_new, cn_new[..., :D_DV], cn_new[..., D_DV], m_new[:, :D_HEADS, 0])
    return x_new.reshape(b, t_len, D_MODEL), new_st


def kernel(x_prompt, x_sample, state_a_pool, state_b_conv, state_b_S, state_c_S, state_d_C, state_d_n, state_d_m,
           norm_mix, w_in, pool_w, pool_scale, gdn_conv_w, gdn_a_log, gdn_dt_bias, gdn_norm,
           gla_w_up, gla_b_up, gla_norm, mlstm_b_i, mlstm_b_f, mlstm_norm,
           w_branch, w_gate, b_gate, w_out, norm_ffn, w_ffn_gate, w_ffn_up, w_ffn_down, norm_final):
    p = dict(norm_mix=norm_mix, w_in=w_in, pool_w=pool_w, pool_scale=pool_scale, gdn_conv_w=gdn_conv_w,
             gdn_a_log=gdn_a_log, gdn_dt_bias=gdn_dt_bias, gdn_norm=gdn_norm, gla_w_up=gla_w_up, gla_b_up=gla_b_up,
             gla_norm=gla_norm, mlstm_b_i=mlstm_b_i, mlstm_b_f=mlstm_b_f, mlstm_norm=mlstm_norm,
             w_branch=w_branch, w_gate=w_gate, b_gate=b_gate, w_out=w_out, norm_ffn=norm_ffn,
             w_ffn_gate=w_ffn_gate, w_ffn_up=w_ffn_up, w_ffn_down=w_ffn_down)
    bp = x_prompt.shape[0]
    zero_p = (jnp.zeros((bp, POOL_BUF, A_WIDTH), F32), jnp.zeros((bp, CONV_W - 1, B_CONV_CH), F32),
              jnp.zeros((bp, B_HEADS, B_DK, B_DV), F32), jnp.zeros((bp, C_HEADS, C_DK, C_DV), F32),
              jnp.zeros((bp, D_HEADS, D_DK, D_DV), F32), jnp.zeros((bp, D_HEADS, D_DK), F32),
              jnp.zeros((bp, D_HEADS), F32))
    gfin = norm_final.reshape(1, D_MODEL)
    yp, ys = x_prompt, x_sample
    new_p, new_s = [], []
    for l in range(DEPTH):
        lp = _layer_params(l, p)
        final = l == DEPTH - 1
        yp, sp = _layer(yp, zero_p, lp, gfin, 0, CHUNK, final)
        cache_l = (state_a_pool[l], state_b_conv[l], state_b_S[l], state_c_S[l],
                   state_d_C[l], state_d_n[l], state_d_m[l])
        ys, ss = _layer(ys, cache_l, lp, gfin, PAST_LEN, x_sample.shape[1], final)
        new_p.append(sp)
        new_s.append(ss)
    outs_p = [jnp.stack([s[i] for s in new_p]) for i in range(7)]
    outs_s = [jnp.stack([s[i] for s in new_s]) for i in range(7)]
    return (yp, ys, *outs_p, *outs_s)
```

```python
import functools
import math

import jax
import jax.numpy as jnp
from jax import lax
from jax.experimental import pallas as pl
from jax.experimental.pallas import tpu as pltpu

F32 = jnp.float32
BF16 = jnp.bfloat16

D_MODEL = 1024
DEPTH = 2
PAST_LEN = 2048
CHUNK = 64
EPS = 1e-6
POOL_WINDOWS = (2, 4, 8, 16)
A_WIDTH = 512
A_GROUP = 128
POOL_BUF = 15
B_HEADS, B_DK, B_DV = 4, 128, 128
B_QK = B_HEADS * B_DK
B_V = B_HEADS * B_DV
CONV_W = 4
B_CONV_CH = 2 * B_QK + B_V
C_HEADS, C_DK, C_DV = 4, 64, 128
C_QK = C_HEADS * C_DK
C_V = C_HEADS * C_DV
GLA_RANK = 16
GLA_NORMALIZER = 16.0
D_HEADS, D_DK, D_DV = 4, 64, 128
D_QK = D_HEADS * D_DK
D_V = D_HEADS * D_DV
BR_WIDTH = 512
N_BRANCH = 4
FFN_HIDDEN = 2816
IN_SPLITS = (A_WIDTH, B_QK, B_QK, B_V, B_V, B_HEADS, B_HEADS, C_QK, C_QK, C_V, C_V, GLA_RANK,
             D_QK, D_QK, D_V, D_V, D_HEADS, D_HEADS)

LANES = 128
Z_WIDTH = 5632
Z_A, Z_BQ, Z_BK, Z_BV, Z_BG = 0, 512, 1024, 1536, 2048
Z_CQ, Z_CK, Z_CV, Z_CG = 2560, 2816, 3072, 3584
Z_DQ, Z_DK, Z_DV, Z_DO = 4096, 4352, 4608, 5120
S_BA, S_BB, S_DI, S_DF, S_LR = 0, 4, 8, 12, 16

ROW_TILE = 512
TIME_TILE = 256
TAIL_COLS = 512
FFN_CHUNK = 256
TAIL_RATIO = 3
POOL_HDR = 16
CONV_HDR = 8
VMEM_LIMIT = 61 * 1024 * 1024

_TN = (((0,), (0,)), ((), ()))
_NT = (((1,), (1,)), ((), ()))


def _bdot(a, b):
    return jnp.dot(a.astype(BF16), b.astype(BF16), preferred_element_type=F32)


def _bdot_nt(a, b):
    return lax.dot_general(a.astype(BF16), b.astype(BF16), _NT, preferred_element_type=F32)


def _bdot_tn(a, b):
    return lax.dot_general(a.astype(BF16), b.astype(BF16), _TN, preferred_element_type=F32)


def _sigmoid(x):
    return 1.0 / (1.0 + jnp.exp(-x))


def _softplus(x):
    return jnp.maximum(x, 0.0) + jnp.log(1.0 + jnp.exp(-jnp.abs(x)))


def _rms(x, gain):
    return x * lax.rsqrt(jnp.mean(x * x, axis=-1, keepdims=True) + EPS) * gain


def _tri(c, kind):
    r = lax.broadcasted_iota(jnp.int32, (c, c), 0)
    k = lax.broadcasted_iota(jnp.int32, (c, c), 1)
    return {"lower": r >= k, "upper": r <= k, "eye": r == k}[kind]


def _bf16_mask(mask):
    return mask.astype(F32).astype(BF16)


def _split_f32(x):
    hi = x.astype(BF16).astype(F32)
    return hi, x - hi


def _hi_lo_rows(x):
    hi, lo = _split_f32(x)
    return jnp.concatenate([hi.astype(BF16), lo.astype(BF16)], axis=0)


def _hi_lo_lanes(x):
    hi, lo = _split_f32(x)
    return jnp.concatenate([hi, lo], axis=1).astype(BF16)


def _resident(shape):
    nd = len(shape)
    return pl.BlockSpec(shape, lambda *_: (0,) * nd, pipeline_mode=pl.Buffered(1))


def _run(*gens, lead=None, ratio=1):
    def step(g):
        try:
            next(g)
            return True
        except StopIteration:
            return False

    gens = list(gens)
    issued = 0
    if lead is not None and not step(lead):
        lead = None
    while gens:
        for g in list(gens):
            if not step(g):
                gens.remove(g)
                continue
            issued += 1
            if lead is not None and issued % ratio == 0 and not step(lead):
                lead = None
    while lead is not None and step(lead):
        pass


def _head_select(off, n_heads):
    r = jnp.arange(2 * LANES)[:, None] % LANES
    c = jnp.arange(n_heads * LANES)[None, :] // LANES
    return (r == off + c).astype(BF16)


def _conv_silu(full_ref, cw_ref, r0, rows, c0):
    base = r0 + CONV_HDR - (CONV_W - 1)
    acc = full_ref[base:base + rows, c0:c0 + LANES] * cw_ref[0:1, c0:c0 + LANES]
    for j in range(1, CONV_W):
        acc = acc + full_ref[base + j:base + j + rows, c0:c0 + LANES] * cw_ref[j:j + 1, c0:c0 + LANES]
    return acc * _sigmoid(acc)


def _proj_kernel(x_ref, g_ref, wz_ref, ws_ref, z_ref, s_ref):
    hb = _rms(x_ref[...], g_ref[...]).astype(BF16)
    step = 512
    for c0 in range(0, Z_WIDTH, step):
        zc = jnp.dot(hb, wz_ref[:, c0:c0 + step], preferred_element_type=F32)
        if c0 in (Z_BG, Z_CG):
            zc = zc * _sigmoid(zc)
        elif c0 == Z_DO:
            zc = _sigmoid(zc)
        z_ref[:, c0:c0 + step] = zc.astype(BF16)
    s_ref[...] = jnp.dot(hb, ws_ref[...], preferred_element_type=F32)


def _project(x2d, gain, wz, ws):
    n = x2d.shape[0]
    r = min(ROW_TILE, n)
    return pl.pallas_call(
        _proj_kernel,
        grid=(n // r,),
        in_specs=[pl.BlockSpec((r, D_MODEL), lambda i: (i, 0)),
                  _resident((1, D_MODEL)), _resident((D_MODEL, Z_WIDTH)), _resident((D_MODEL, LANES))],
        out_specs=[pl.BlockSpec((r, Z_WIDTH), lambda i: (i, 0)), pl.BlockSpec((r, LANES), lambda i: (i, 0))],
        out_shape=[jax.ShapeDtypeStruct((n, Z_WIDTH), BF16), jax.ShapeDtypeStruct((n, LANES), F32)],
        compiler_params=pltpu.CompilerParams(dimension_semantics=("parallel",), vmem_limit_bytes=VMEM_LIMIT),
        name="proj",
    )(x2d, gain, wz, ws)


def _pool_stages(u_ref, pw_ref, sc_ref, full_ref, store_y, pos, tt):
    hdr = POOL_HDR
    full_ref[hdr:hdr + tt, :] = u_ref[0].astype(F32)
    yield
    for gi, w in enumerate(POOL_WINDOWS):
        cols = slice(gi * A_GROUP, (gi + 1) * A_GROUP)
        u = full_ref[hdr:hdr + tt, cols]
        acc = u
        for j in range(1, w):
            acc = acc + full_ref[hdr - j:hdr - j + tt, cols]
        cnt = jnp.minimum(pos + 1, w).astype(F32)
        diff = acc / cnt - u
        y = _bdot(diff, pw_ref[gi]) * sc_ref[:, cols]
        store_y(slice(0, tt), cols, y.astype(BF16))
        yield


def _gdn_stages(q_ref, k_ref, v_ref, g_ref, sm_ref, cw_ref, alog_ref, dt_ref, nrm_ref, selg_ref, selb_ref,
                s_ref, full_ref, store_y, tt, chunk):
    hdr = CONV_HDR
    full_ref[hdr:hdr + tt, 0:B_QK] = q_ref[0].astype(F32)
    full_ref[hdr:hdr + tt, B_QK:2 * B_QK] = k_ref[0].astype(F32)
    full_ref[hdr:hdr + tt, 2 * B_QK:] = v_ref[0].astype(F32)
    yield
    rep = LANES // chunk
    row = lax.broadcasted_iota(jnp.int32, (chunk, LANES), 0)
    col = lax.broadcasted_iota(jnp.int32, (chunk, LANES), 1) & (chunk - 1)
    causal = row >= col
    strict = row > col
    eye = (row == col).astype(F32)
    tril2 = jnp.concatenate([_bf16_mask(causal[:, :chunk])] * 2, axis=1)
    triu2 = jnp.concatenate([_bf16_mask(row <= col)] * 2, axis=0)
    ones = jnp.ones((LANES, LANES), BF16)
    heads = range(B_HEADS)
    chunks = range(tt // chunk)
    ch = [(c, h) for c in chunks for h in heads]
    rows = [slice(c * chunk, (c + 1) * chunk) for c in chunks]
    n_steps = int(math.log2(chunk)) - 1

    q, k, v = {}, {}, {}
    for c in chunks:
        for h in heads:
            q[c, h] = _conv_silu(full_ref, cw_ref, c * chunk, chunk, h * B_DK)
            k[c, h] = _conv_silu(full_ref, cw_ref, c * chunk, chunk, B_QK + h * B_DK)
            v[c, h] = _conv_silu(full_ref, cw_ref, c * chunk, chunk, 2 * B_QK + h * B_DV)
        yield
    for c in chunks:
        sq = jnp.concatenate([q[c, h] * q[c, h] for h in heads] + [k[c, h] * k[c, h] for h in heads], axis=0)
        ssq = jnp.dot(sq.astype(BF16), ones, preferred_element_type=F32)
        for h in heads:
            q[c, h] = q[c, h] * (lax.rsqrt(ssq[h * chunk:(h + 1) * chunk, :] + EPS) * (B_DK ** -0.5))
            k[c, h] = k[c, h] * lax.rsqrt(ssq[(B_HEADS + h) * chunk:(B_HEADS + h + 1) * chunk, :] + EPS)
    yield
    sm = [sm_ref[0, rows[c], :] for c in chunks]
    log_alpha = [alog_ref[...] * _softplus(sm[c] + dt_ref[...]) for c in chunks]
    beta = [_sigmoid(sm[c]) for c in chunks]
    la2 = [_hi_lo_rows(log_alpha[c]) for c in chunks]
    g_col = [jnp.dot(tril2, la2[c], preferred_element_type=F32) for c in chunks]
    g_row = [lax.dot_general(la2[c], triu2, _TN, preferred_element_type=F32) for c in chunks]
    yield
    g_b = jnp.dot(_hi_lo_lanes(jnp.concatenate(g_col, axis=0)), selg_ref[...], preferred_element_type=F32)
    b_b = jnp.dot(_hi_lo_lanes(jnp.concatenate(beta, axis=0)), selb_ref[...], preferred_element_type=F32)
    gi = {(c, h): g_b[rows[c], h * LANES:(h + 1) * LANES] for c, h in ch}
    bt = {(c, h): b_b[rows[c], h * LANES:(h + 1) * LANES] for c, h in ch}
    gj = {(c, h): g_row[c][S_BA + h:S_BA + h + 1, :] for c, h in ch}
    yield
    decay, kb, eg, low = {}, {}, {}, {}
    for c in chunks:
        for i in [(c, h) for h in heads]:
            decay[i] = jnp.exp(jnp.where(causal, gi[i] - gj[i], -jnp.inf))
            kb[i] = k[i] * bt[i]
            eg[i] = jnp.exp(gi[i])
            low[i] = jnp.where(strict, _bdot_nt(kb[i], jnp.concatenate([k[i]] * rep, axis=0)) * decay[i], 0.0)
        yield
    lb = {i: low[i].astype(BF16) for i in ch}
    m = {i: jnp.dot(lb[i][:, :chunk], lb[i], preferred_element_type=F32) for i in ch}
    tinv = {i: eye - low[i] for i in ch}
    yield
    for step in range(n_steps):
        mb = {i: m[i].astype(BF16) for i in ch}
        tb = {i: tinv[i].astype(BF16) for i in ch}
        if step == n_steps - 1:
            tinv = {i: tinv[i] + jnp.dot(mb[i][:, :chunk], tb[i], preferred_element_type=F32) for i in ch}
        else:
            out = {i: jnp.dot(mb[i][:, :chunk], jnp.concatenate([mb[i], tb[i]], axis=1),
                              preferred_element_type=F32) for i in ch}
            m = {i: out[i][:, :LANES] for i in ch}
            tinv = {i: tinv[i] + out[i][:, LANES:] for i in ch}
        yield
    sol, attn, g_last, k_tail, q_dec = {}, {}, {}, {}, {}
    for c in chunks:
        for i in [(c, h) for h in heads]:
            sol[i] = _bdot(tinv[i][:, :chunk], jnp.concatenate([v[i] * bt[i], kb[i] * eg[i]], axis=-1))
            attn[i] = jnp.where(causal[:, :chunk], _bdot_nt(q[i], k[i]) * decay[i][:, :chunk], 0.0)
            g_last[i] = gi[i][chunk - 1:chunk, :]
            k_tail[i] = k[i] * jnp.exp(g_last[i] - gi[i])
            q_dec[i] = q[i] * eg[i]
        yield
    s = [s_ref[h] for h in heads]
    o = {}
    for c in chunks:
        ws = [_bdot(sol[c, h][:, B_DV:], s[h]) for h in heads]
        qs = [_bdot(q_dec[c, h], s[h]) for h in heads]
        yield
        v_new = [sol[c, h][:, :B_DV] - ws[h] for h in heads]
        for h in heads:
            o[c, h] = qs[h] + _bdot(attn[c, h], v_new[h])
        s = [s[h] * jnp.exp(g_last[c, h]) + _bdot_tn(k_tail[c, h], v_new[h]) for h in heads]
        yield
    for h in heads:
        s_ref[h] = s[h]
    for c in chunks:
        oo = jnp.concatenate([o[c, h] * o[c, h] for h in heads], axis=0)
        ms = jnp.dot(oo.astype(BF16), ones, preferred_element_type=F32) * (1.0 / B_DV)
        for h in heads:
            cols = slice(h * B_DV, (h + 1) * B_DV)
            gate = g_ref[0, rows[c], cols].astype(F32)
            y = o[c, h] * lax.rsqrt(ms[h * chunk:(h + 1) * chunk, :] + EPS) * nrm_ref[...] * gate
            store_y(rows[c], cols, y.astype(BF16))
        yield


def _gla_stages(q_ref, k_ref, v_ref, g_ref, sm_ref, wup_ref, bup_ref, nrm_ref, s_ref, store_y, tt, chunk):
    causal = _tri(chunk, "lower")
    tril2 = jnp.concatenate([_bf16_mask(causal)] * 2, axis=1)
    ones = jnp.ones((2 * chunk, LANES), BF16)
    heads = range(C_HEADS)
    pairs = range(C_HEADS // 2)
    chunks = range(tt // chunk)
    rows = [slice(c * chunk, (c + 1) * chunk) for c in chunks]
    vs = [slice(h * C_DV, (h + 1) * C_DV) for h in heads]
    sm_sp = [_split_f32(sm_ref[0, rows[c], :]) for c in chunks]
    lr = [jnp.dot(jnp.concatenate([sm_sp[c][0], sm_sp[c][1], sm_sp[c][0]], axis=1).astype(BF16), wup_ref[...],
                  preferred_element_type=F32) + bup_ref[...] for c in chunks]
    log_a = [-_softplus(-lr[c]) / GLA_NORMALIZER for c in chunks]
    yield
    la2 = [_hi_lo_rows(log_a[c]) for c in chunks]
    bcum = [jnp.dot(tril2, la2[c], preferred_element_type=F32) for c in chunks]
    d_col = [jnp.exp(lax.dot_general(la2[c], ones, _TN, preferred_element_type=F32)) for c in chunks]
    yield
    mid = [bcum[c][chunk // 2:chunk // 2 + 1, :] for c in chunks]
    last = [bcum[c][chunk - 1:chunk, :] for c in chunks]
    qf = [q_ref[0, rows[c], :].astype(F32) * (C_DK ** -0.5) for c in chunks]
    kf = [k_ref[0, rows[c], :].astype(F32) for c in chunks]
    q_in = [qf[c] * jnp.exp(bcum[c] - mid[c]) for c in chunks]
    k_in = [kf[c] * jnp.exp(mid[c] - bcum[c]) for c in chunks]
    q_x = [qf[c] * jnp.exp(bcum[c]) for c in chunks]
    k_t = [kf[c] * jnp.exp(last[c] - bcum[c]) for c in chunks]
    yield
    lane = lax.broadcasted_iota(jnp.int32, (chunk, LANES), 1)
    half = [(lane < C_DK) if h % 2 == 0 else (lane >= C_DK) for h in heads]
    grp = [slice((h // 2) * LANES, (h // 2 + 1) * LANES) for h in heads]
    v = [[v_ref[0, rows[c], vs[h]] for h in heads] for c in chunks]
    attn = [[jnp.where(causal, _bdot_nt(jnp.where(half[h], q_in[c][:, grp[h]], 0.0), k_in[c][:, grp[h]]), 0.0)
             for h in heads] for c in chunks]
    yield
    o_in = [[_bdot(attn[c][h], v[c][h]) for h in heads] for c in chunks]
    upd = [[_bdot_tn(jnp.where(half[h], k_t[c][:, grp[h]], 0.0), v[c][h]) for h in heads] for c in chunks]
    yield
    s = [[s_ref[p] for p in pairs]]
    for c in chunks:
        s.append([s[c][p] * d_col[c][p * LANES:(p + 1) * LANES, :] + upd[c][2 * p] + upd[c][2 * p + 1]
                  for p in pairs])
    for p in pairs:
        s_ref[p] = s[-1][p]
    o = [[_bdot(jnp.where(half[h], q_x[c][:, grp[h]], 0.0), s[c][h // 2]) + o_in[c][h] for h in heads]
         for c in chunks]
    yield
    inv_dv = jnp.full((LANES, LANES), 1.0 / C_DV, BF16)
    for c in chunks:
        oo = jnp.concatenate([o[c][h] * o[c][h] for h in heads], axis=0)
        ms = jnp.dot(oo.astype(BF16), inv_dv, preferred_element_type=F32)
        for h in heads:
            gate = g_ref[0, rows[c], vs[h]].astype(F32)
            y = o[c][h] * lax.rsqrt(ms[h * chunk:(h + 1) * chunk, :] + EPS) * nrm_ref[...] * gate
            store_y(rows[c], vs[h], y.astype(BF16))
        yield


def _mlstm_stages(q_ref, k_ref, v_ref, og_ref, sm_ref, bi_ref, bf_ref, nrm_ref, self_ref, seli_ref,
                  cn_ref, m_ref, store_y, tt, chunk):
    causal = _tri(chunk, "lower")
    tril2 = jnp.concatenate([_bf16_mask(causal)] * 2, axis=1)
    triu2 = jnp.concatenate([_bf16_mask(_tri(chunk, "upper"))] * 2, axis=0)
    eye2 = jnp.concatenate([_bf16_mask(_tri(chunk, "eye"))] * 2, axis=0)
    ones = jnp.ones((chunk, LANES), BF16)
    heads = range(D_HEADS)
    pairs = range(D_HEADS // 2)
    chunks = range(tt // chunk)
    n_chunks = len(chunks)
    ch = [(c, h) for c in chunks for h in heads]
    rows = [slice(c * chunk, (c + 1) * chunk) for c in chunks]
    vs = [slice(h * D_DV, (h + 1) * D_DV) for h in heads]
    sm = [sm_ref[0, rows[c], :] for c in chunks]
    i_pre = [sm[c] + bi_ref[...] for c in chunks]
    log_f = [-_softplus(-(sm[c] + bf_ref[...])) for c in chunks]
    lf2 = [_hi_lo_rows(log_f[c]) for c in chunks]
    ip2 = [_hi_lo_rows(i_pre[c]) for c in chunks]
    b_col = [jnp.dot(tril2, lf2[c], preferred_element_type=F32) for c in chunks]
    b_row = [lax.dot_general(lf2[c], triu2, _TN, preferred_element_type=F32) for c in chunks]
    i_row = [lax.dot_general(ip2[c], eye2, _TN, preferred_element_type=F32) for c in chunks]
    yield
    b_b = jnp.dot(_hi_lo_lanes(jnp.concatenate(b_col, axis=0)), self_ref[...], preferred_element_type=F32)
    i_b = jnp.dot(_hi_lo_lanes(jnp.concatenate(i_pre, axis=0)), seli_ref[...], preferred_element_type=F32)
    bi = {(c, h): b_b[rows[c], h * LANES:(h + 1) * LANES] for c, h in ch}
    ii = {(c, h): i_b[rows[c], h * LANES:(h + 1) * LANES] for c, h in ch}
    yield
    lane = lax.broadcasted_iota(jnp.int32, (chunk, LANES), 1)
    half = [(lane < D_DK) if h % 2 == 0 else (lane >= D_DK) for h in heads]
    grp = [slice((h // 2) * LANES, (h // 2 + 1) * LANES) for h in heads]
    qf = [q_ref[0, rows[c], :].astype(F32) * (D_DK ** -0.5) for c in chunks]
    kf = [k_ref[0, rows[c], :].astype(F32) for c in chunks]
    q, v1, log_w, lw_max, qk = {}, {}, {}, {}, {}
    for c in chunks:
        for h in heads:
            q[c, h] = jnp.where(half[h], qf[c][:, grp[h]], 0.0)
            v1[c, h] = jnp.concatenate([v_ref[0, rows[c], vs[h]], ones], axis=-1)
            log_w[c, h] = jnp.where(causal, bi[c, h][:, :chunk] - b_row[c][S_DF + h:S_DF + h + 1, :]
                                    + i_row[c][S_DI + h:S_DI + h + 1, :], -jnp.inf)
            lw_max[c, h] = jnp.max(log_w[c, h], axis=-1, keepdims=True)
            qk[c, h] = _bdot_nt(q[c, h], kf[c][:, grp[h]])
        yield
    b_last = {i: bi[i][chunk - 1:chunk, :] for i in ch}
    m_s = {(0, h): m_ref[h:h + 1, :] for h in heads}
    for c, h in ch:
        m_s[c + 1, h] = jnp.maximum(b_last[c, h] + m_s[c, h], lw_max[c, h][chunk - 1:chunk, :])
    for h in heads:
        m_ref[h:h + 1, :] = m_s[n_chunks, h]
    m_t, w_inter, pv, upd = {}, {}, {}, {}
    for c in chunks:
        for h in heads:
            log_inter = bi[c, h] + m_s[c, h]
            m_t[c, h] = jnp.maximum(log_inter, lw_max[c, h])
            w_inter[c, h] = jnp.exp(log_inter - m_t[c, h])
            p_w = jnp.exp(log_w[c, h] - m_t[c, h][:, :chunk]) * qk[c, h]
            pv[c, h] = _bdot(p_w, v1[c, h])
            w_k = jnp.exp(b_last[c, h] - bi[c, h] + ii[c, h] - m_s[c + 1, h])
            upd[c, h] = _bdot_tn(jnp.where(half[h], kf[c][:, grp[h]] * w_k, 0.0), v1[c, h])
        yield
    top = lax.broadcasted_iota(jnp.int32, (2 * D_DK, LANES), 0) < D_DK
    cn = {(0, p): cn_ref[p] for p in pairs}
    for c in chunks:
        for p in pairs:
            w_a, w_b = (jnp.exp(b_last[c, h] + m_s[c, h] - m_s[c + 1, h]) for h in (2 * p, 2 * p + 1))
            w_state = jnp.where(top, w_a, w_b)
            cn[c + 1, p] = jnp.concatenate([w_state, w_state], axis=1) * cn[c, p] + upd[c, 2 * p] + upd[c, 2 * p + 1]
    for p in pairs:
        cn_ref[p] = cn[n_chunks, p]
    qcn = {(c, h): _bdot(q[c, h], cn[c, h // 2]) for c, h in ch}
    yield
    inv_dv = jnp.full((LANES, LANES), 1.0 / D_DV, BF16)
    for c in chunks:
        hh = []
        for h in heads:
            num = w_inter[c, h] * qcn[c, h][:, :D_DV] + pv[c, h][:, :D_DV]
            den = w_inter[c, h] * qcn[c, h][:, D_DV:] + pv[c, h][:, D_DV:]
            hh.append(num / jnp.maximum(jnp.abs(den), jnp.exp(-m_t[c, h])))
        ms = jnp.dot(jnp.concatenate([x * x for x in hh], axis=0).astype(BF16), inv_dv, preferred_element_type=F32)
        for h in heads:
            gate = og_ref[0, rows[c], vs[h]].astype(F32)
            y = gate * (hh[h] * lax.rsqrt(ms[h * chunk:(h + 1) * chunk, :] + EPS) * nrm_ref[...])
            store_y(rows[c], vs[h], y.astype(BF16))
        yield


def _tail_stages(x_ref, read_y, gmix_ref, wgate_ref, bgate_ref, wbr_ref, wout_ref, gffn_ref, wfg_ref, wfu_ref,
                 wfd_ref, gfin_ref, o_ref, final):
    x = x_ref[...]
    hb = _rms(x, gmix_ref[...]).astype(BF16)
    yield
    merged = []
    for c0 in range(0, D_MODEL, TAIL_COLS):
        cols = slice(c0, c0 + TAIL_COLS)
        acc = None
        for i in range(N_BRANCH):
            gate = _sigmoid(jnp.dot(hb, wgate_ref[i, :, cols], preferred_element_type=F32) + bgate_ref[i, :, cols])
            term = gate * jnp.dot(read_y(i), wbr_ref[i, :, cols], preferred_element_type=F32)
            acc = term if acc is None else acc + term
            yield
        merged.append(acc.astype(BF16))
    mb = jnp.concatenate(merged, axis=1)
    x1 = []
    for c0 in range(0, D_MODEL, TAIL_COLS):
        cols = slice(c0, c0 + TAIL_COLS)
        x1.append(x[:, cols] + jnp.dot(mb, wout_ref[:, cols], preferred_element_type=F32))
        yield
    x = jnp.concatenate(x1, axis=1)
    h2 = _rms(x, gffn_ref[...]).astype(BF16)

    def hidden(c):
        a = jnp.dot(h2, wfg_ref[:, c], preferred_element_type=F32)
        u = jnp.dot(h2, wfu_ref[:, c], preferred_element_type=F32)
        return (a * _sigmoid(a) * u).astype(BF16)

    bounds = list(range(0, FFN_HIDDEN, FFN_CHUNK)) + [FFN_HIDDEN]
    pieces = [slice(lo, hi) for lo, hi in zip(bounds[:-1], bounds[1:])]
    f = hidden(pieces[0])
    yield
    for n, piece in enumerate(pieces):
        f_next = hidden(pieces[n + 1]) if n + 1 < len(pieces) else None
        x = x + jnp.dot(f, wfd_ref[piece, :], preferred_element_type=F32)
        f = f_next
        yield
    if final:
        x = _rms(x, gfin_ref[...])
    o_ref[...] = x


def _pool_kernel(u_ref, buf_ref, pw_ref, sc_ref, y_ref, nbuf_ref, full_ref, *, tt, pos0):
    t = pl.program_id(1)
    hdr = POOL_HDR

    @pl.when(t == 0)
    def _():
        full_ref[1:hdr, :] = buf_ref[0]

    @pl.when(t > 0)
    def _():
        full_ref[1:hdr, :] = full_ref[tt + 1:tt + hdr, :]

    def store_y(rows, cols, val):
        y_ref[0, rows, cols] = val

    pos = pos0 + t * tt + lax.broadcasted_iota(jnp.int32, (tt, 1), 0)
    _run(_pool_stages(u_ref, pw_ref, sc_ref, full_ref, store_y, pos, tt))
    nbuf_ref[0] = full_ref[tt + 1:tt + hdr, :]


def _pool_mixer(z3, buf, pool_w, scale, pos0):
    b, t_len, _ = z3.shape
    tt = min(TIME_TILE, t_len)
    return pl.pallas_call(
        functools.partial(_pool_kernel, tt=tt, pos0=pos0),
        grid=(b, t_len // tt),
        in_specs=[pl.BlockSpec((1, tt, A_WIDTH), lambda i, j: (i, j, Z_A // A_WIDTH)),
                  pl.BlockSpec((1, POOL_BUF, A_WIDTH), lambda i, j: (i, 0, 0)),
                  _resident((4, A_GROUP, A_GROUP)), _resident((1, A_WIDTH))],
        out_specs=[pl.BlockSpec((1, tt, A_WIDTH), lambda i, j: (i, j, 0)),
                   pl.BlockSpec((1, POOL_BUF, A_WIDTH), lambda i, j: (i, 0, 0))],
        out_shape=[jax.ShapeDtypeStruct((b, t_len, A_WIDTH), BF16),
                   jax.ShapeDtypeStruct((b, POOL_BUF, A_WIDTH), F32)],
        scratch_shapes=[pltpu.VMEM((POOL_HDR + tt, A_WIDTH), F32)],
        compiler_params=pltpu.CompilerParams(dimension_semantics=("parallel", "arbitrary"),
                                             vmem_limit_bytes=VMEM_LIMIT),
        name="pool",
    )(z3, buf, pool_w, scale)


def _gdn_kernel(q_ref, k_ref, v_ref, g_ref, sm_ref, cbuf_ref, s_in_ref, cw_ref, alog_ref, dt_ref, nrm_ref,
                selg_ref, selb_ref, y_ref, ncbuf_ref, s_ref, full_ref, *, tt, chunk):
    t = pl.program_id(1)
    hdr = CONV_HDR

    @pl.when(t == 0)
    def _():
        full_ref[hdr - 3:hdr, :] = cbuf_ref[0]
        s_ref[0] = s_in_ref[0]

    @pl.when(t > 0)
    def _():
        full_ref[hdr - 3:hdr, :] = full_ref[tt + hdr - 3:tt + hdr, :]

    def store_y(rows, cols, val):
        y_ref[0, rows, cols] = val

    _run(_gdn_stages(q_ref, k_ref, v_ref, g_ref, sm_ref, cw_ref, alog_ref, dt_ref, nrm_ref, selg_ref, selb_ref,
                     s_ref.at[0], full_ref, store_y, tt, chunk))
    ncbuf_ref[0] = full_ref[tt + hdr - 3:tt + hdr, :]


def _gdn_mixer(z3, sm3, cbuf, s0, conv_w, alog_row, dt_row, norm_row, chunk):
    b, t_len, _ = z3.shape
    tt = min(TIME_TILE, t_len)
    wide = lambda off: pl.BlockSpec((1, tt, 512), lambda i, j: (i, j, off // 512))
    sel = _resident((2 * LANES, B_HEADS * LANES))
    return pl.pallas_call(
        functools.partial(_gdn_kernel, tt=tt, chunk=chunk),
        grid=(b, t_len // tt),
        in_specs=[wide(Z_BQ), wide(Z_BK), wide(Z_BV), wide(Z_BG),
                  pl.BlockSpec((1, tt, LANES), lambda i, j: (i, j, 0)),
                  pl.BlockSpec((1, CONV_W - 1, B_CONV_CH), lambda i, j: (i, 0, 0)),
                  pl.BlockSpec((1, B_HEADS, B_DK, B_DV), lambda i, j: (i, 0, 0, 0)),
                  _resident((CONV_W, B_CONV_CH)), _resident((1, LANES)), _resident((1, LANES)),
                  _resident((1, B_DV)), sel, sel],
        out_specs=[pl.BlockSpec((1, tt, B_V), lambda i, j: (i, j, 0)),
                   pl.BlockSpec((1, CONV_W - 1, B_CONV_CH), lambda i, j: (i, 0, 0)),
                   pl.BlockSpec((1, B_HEADS, B_DK, B_DV), lambda i, j: (i, 0, 0, 0))],
        out_shape=[jax.ShapeDtypeStruct((b, t_len, B_V), BF16),
                   jax.ShapeDtypeStruct((b, CONV_W - 1, B_CONV_CH), F32),
                   jax.ShapeDtypeStruct((b, B_HEADS, B_DK, B_DV), F32)],
        scratch_shapes=[pltpu.VMEM((CONV_HDR + tt, B_CONV_CH), F32)],
        compiler_params=pltpu.CompilerParams(dimension_semantics=("parallel", "arbitrary"),
                                             vmem_limit_bytes=VMEM_LIMIT),
        name="gdn",
    )(z3, z3, z3, z3, sm3, cbuf, s0, conv_w, alog_row, dt_row, norm_row,
      _head_select(S_BA, B_HEADS), _head_select(S_BB, B_HEADS))


def _gla_kernel(q_ref, k_ref, v_ref, g_ref, sm_ref, s_in_ref, wup_ref, bup_ref, nrm_ref,
                y_ref, s_ref, *, tt, chunk):
    @pl.when(pl.program_id(1) == 0)
    def _():
        s_ref[0] = s_in_ref[0]

    def store_y(rows, cols, val):
        y_ref[0, rows, cols] = val

    _run(_gla_stages(q_ref, k_ref, v_ref, g_ref, sm_ref, wup_ref, bup_ref, nrm_ref, s_ref.at[0], store_y, tt, chunk))


def _gla_mixer(z3, sm3, s0, wup_pack, bup_row, norm_row, chunk):
    b, t_len, _ = z3.shape
    tt = min(TIME_TILE, t_len)
    blk = lambda w, off: pl.BlockSpec((1, tt, w), lambda i, j: (i, j, off // w))
    state = pl.BlockSpec((1, C_HEADS // 2, 2 * C_DK, C_DV), lambda i, j: (i, 0, 0, 0))
    return pl.pallas_call(
        functools.partial(_gla_kernel, tt=tt, chunk=chunk),
        grid=(b, t_len // tt),
        in_specs=[blk(C_QK, Z_CQ), blk(C_QK, Z_CK), blk(C_V, Z_CV), blk(C_V, Z_CG),
                  pl.BlockSpec((1, tt, LANES), lambda i, j: (i, j, 0)), state,
                  _resident((3 * LANES, C_QK)), _resident((1, C_QK)), _resident((1, C_DV))],
        out_specs=[pl.BlockSpec((1, tt, C_V), lambda i, j: (i, j, 0)), state],
        out_shape=[jax.ShapeDtypeStruct((b, t_len, C_V), BF16),
                   jax.ShapeDtypeStruct((b, C_HEADS // 2, 2 * C_DK, C_DV), F32)],
        compiler_params=pltpu.CompilerParams(dimension_semantics=("parallel", "arbitrary"),
                                             vmem_limit_bytes=VMEM_LIMIT),
        name="gla",
    )(z3, z3, z3, z3, sm3, s0, wup_pack, bup_row, norm_row)


def _mlstm_kernel(q_ref, k_ref, v_ref, og_ref, sm_ref, cn_in_ref, m_in_ref, bi_ref, bf_ref, nrm_ref,
                  self_ref, seli_ref, y_ref, cn_ref, m_ref, *, tt, chunk):
    @pl.when(pl.program_id(1) == 0)
    def _():
        cn_ref[0] = cn_in_ref[0]
        m_ref[0] = m_in_ref[0]

    def store_y(rows, cols, val):
        y_ref[0, rows, cols] = val

    _run(_mlstm_stages(q_ref, k_ref, v_ref, og_ref, sm_ref, bi_ref, bf_ref, nrm_ref, self_ref, seli_ref,
                       cn_ref.at[0], m_ref.at[0], store_y, tt, chunk))


def _mlstm_mixer(z3, sm3, cn0, m0, bi_row, bf_row, norm_row, chunk):
    b, t_len, _ = z3.shape
    tt = min(TIME_TILE, t_len)
    blk = lambda w, off: pl.BlockSpec((1, tt, w), lambda i, j: (i, j, off // w))
    cn_spec = pl.BlockSpec((1, D_HEADS // 2, 2 * D_DK, 2 * D_DV), lambda i, j: (i, 0, 0, 0))
    m_spec = pl.BlockSpec((1, 8, LANES), lambda i, j: (i, 0, 0))
    sel = _resident((2 * LANES, D_HEADS * LANES))
    return pl.pallas_call(
        functools.partial(_mlstm_kernel, tt=tt, chunk=chunk),
        grid=(b, t_len // tt),
        in_specs=[blk(D_QK, Z_DQ), blk(D_QK, Z_DK), blk(D_V, Z_DV), blk(D_V, Z_DO),
                  pl.BlockSpec((1, tt, LANES), lambda i, j: (i, j, 0)), cn_spec, m_spec,
                  _resident((1, LANES)), _resident((1, LANES)), _resident((1, D_DV)), sel, sel],
        out_specs=[pl.BlockSpec((1, tt, D_V), lambda i, j: (i, j, 0)), cn_spec, m_spec],
        out_shape=[jax.ShapeDtypeStruct((b, t_len, D_V), BF16),
                   jax.ShapeDtypeStruct((b, D_HEADS // 2, 2 * D_DK, 2 * D_DV), F32),
                   jax.ShapeDtypeStruct((b, 8, LANES), F32)],
        compiler_params=pltpu.CompilerParams(dimension_semantics=("parallel", "arbitrary"),
                                             vmem_limit_bytes=VMEM_LIMIT),
        name="mlstm",
    )(z3, z3, z3, z3, sm3, cn0, m0, bi_row, bf_row, norm_row,
      _head_select(S_DF, D_HEADS), _head_select(S_DI, D_HEADS))


def _tail_kernel(x_ref, ya_ref, yb_ref, yc_ref, yd_ref, gmix_ref, wgate_ref, bgate_ref, wbr_ref, wout_ref,
                 gffn_ref, wfg_ref, wfu_ref, wfd_ref, gfin_ref, o_ref, *, final):
    ys = (ya_ref, yb_ref, yc_ref, yd_ref)
    _run(_tail_stages(x_ref, lambda i: ys[i][...], gmix_ref, wgate_ref, bgate_ref, wbr_ref, wout_ref, gffn_ref,
                      wfg_ref, wfu_ref, wfd_ref, gfin_ref, o_ref, final))


def _tail_weight_specs():
    return [_resident((1, D_MODEL)), _resident((N_BRANCH, D_MODEL, D_MODEL)), _resident((N_BRANCH, 1, D_MODEL)),
            _resident((N_BRANCH, BR_WIDTH, D_MODEL)), _resident((D_MODEL, D_MODEL)), _resident((1, D_MODEL)),
            _resident((D_MODEL, FFN_HIDDEN)), _resident((D_MODEL, FFN_HIDDEN)),
            _resident((FFN_HIDDEN, D_MODEL)), _resident((1, D_MODEL))]


def _tail_weights(lp, gfin):
    return (lp["gmix"], lp["wgate"], lp["bgate"], lp["wbr"], lp["wout"], lp["gffn"], lp["wfg"], lp["wfu"], lp["wfd"],
            gfin)


def _tail(x2d, ys, lp, gfin, final):
    n = x2d.shape[0]
    r = min(ROW_TILE, n)
    row = lambda w: pl.BlockSpec((r, w), lambda i: (i, 0))
    return pl.pallas_call(
        functools.partial(_tail_kernel, final=final),
        grid=(n // r,),
        in_specs=[row(D_MODEL)] + [row(BR_WIDTH)] * N_BRANCH + _tail_weight_specs(),
        out_specs=row(D_MODEL),
        out_shape=jax.ShapeDtypeStruct((n, D_MODEL), F32),
        compiler_params=pltpu.CompilerParams(dimension_semantics=("parallel",), vmem_limit_bytes=VMEM_LIMIT),
        name="tail",
    )(x2d, *ys, *_tail_weights(lp, gfin))


def _mix_tail_kernel(z_ref, sm_ref,
                     abuf_in, cbuf_in, gs_in, cs_in, cn_in, m_in, x_ref,
                     pw_ref, psc_ref, cw_ref, alog_ref, dt_ref, gnrm_ref, selg_ref, selb_ref,
                     wup_ref, bup_ref, cnrm_ref, bi_ref, bf_ref, dnrm_ref, self_ref, seli_ref,
                     gmix_ref, wgate_ref, bgate_ref, wbr_ref, wout_ref, gffn_ref, wfg_ref, wfu_ref, wfd_ref, gfin_ref,
                     o_ref, abuf_out, cbuf_out, gs_out, cs_out, cn_out, m_out,
                     ybuf, pfull, cfull, gs, cs, cn, m, *, tt, chunk, nt, n_tiles, pos0, final):
    s = pl.program_id(0)
    t = jnp.minimum(s, n_tiles - 1) % nt
    slot = s % 2
    piece = lambda off, w: z_ref.at[:, :, off:off + w]
    a_ref = piece(Z_A, A_WIDTH)
    bq_ref, bk_ref, bv_ref, bg_ref = piece(Z_BQ, B_QK), piece(Z_BK, B_QK), piece(Z_BV, B_V), piece(Z_BG, B_V)
    cq_ref, ck_ref, cv_ref, cg_ref = piece(Z_CQ, C_QK), piece(Z_CK, C_QK), piece(Z_CV, C_V), piece(Z_CG, C_V)
    dq_ref, dk_ref, dv_ref, do_ref = piece(Z_DQ, D_QK), piece(Z_DK, D_QK), piece(Z_DV, D_V), piece(Z_DO, D_V)

    @pl.when(s == 0)
    def _():
        ybuf[1] = jnp.zeros(ybuf.shape[1:], BF16)

    @pl.when(t == 0)
    def _():
        pfull[1:POOL_HDR, :] = abuf_in[0]
        cfull[CONV_HDR - 3:CONV_HDR, :] = cbuf_in[0]
        gs[...] = gs_in[0]
        cs[...] = cs_in[0]
        cn[...] = cn_in[0]
        m[...] = m_in[0]

    @pl.when(t > 0)
    def _():
        pfull[1:POOL_HDR, :] = pfull[tt + 1:tt + POOL_HDR, :]
        cfull[CONV_HDR - 3:CONV_HDR, :] = cfull[tt + CONV_HDR - 3:tt + CONV_HDR, :]

    def store_y(branch):
        def store(rows, cols, val):
            ybuf[slot, rows, slice(branch * BR_WIDTH + cols.start, branch * BR_WIDTH + cols.stop)] = val
        return store

    def read_y(i):
        return ybuf[1 - slot, :, i * BR_WIDTH:(i + 1) * BR_WIDTH]

    pos = pos0 + t * tt + lax.broadcasted_iota(jnp.int32, (tt, 1), 0)
    _run(_gdn_stages(bq_ref, bk_ref, bv_ref, bg_ref, sm_ref, cw_ref, alog_ref, dt_ref, gnrm_ref, selg_ref, selb_ref,
                     gs, cfull, store_y(1), tt, chunk),
         _mlstm_stages(dq_ref, dk_ref, dv_ref, do_ref, sm_ref, bi_ref, bf_ref, dnrm_ref, self_ref, seli_ref,
                       cn, m, store_y(3), tt, chunk),
         _gla_stages(cq_ref, ck_ref, cv_ref, cg_ref, sm_ref, wup_ref, bup_ref, cnrm_ref, cs, store_y(2), tt, chunk),
         _pool_stages(a_ref, pw_ref, psc_ref, pfull, store_y(0), pos, tt),
         lead=_tail_stages(x_ref, read_y, gmix_ref, wgate_ref, bgate_ref, wbr_ref, wout_ref, gffn_ref, wfg_ref,
                           wfu_ref, wfd_ref, gfin_ref, o_ref, final),
         ratio=TAIL_RATIO)

    @pl.when((t == nt - 1) & (s < n_tiles))
    def _():
        abuf_out[0] = pfull[tt + 1:tt + POOL_HDR, :]
        cbuf_out[0] = cfull[tt + CONV_HDR - 3:tt + CONV_HDR, :]
        gs_out[0] = gs[...]
        cs_out[0] = cs[...]
        cn_out[0] = cn[...]
        m_out[0] = m[...]


def _mix_tail(x2d, z3, sm3, states, lp, gfin, pos0, chunk, final):
    b, t_len, _ = z3.shape
    tt = TIME_TILE
    nt = t_len // tt
    n_tiles = b * nt
    a_buf, conv_buf, gdn_s, gla_s, cn0, m0 = states

    def bt(s):
        sm = jnp.minimum(s, n_tiles - 1)
        return sm // nt, sm % nt

    def per_stream(shape):
        nd = len(shape)
        return pl.BlockSpec((1,) + shape, lambda s: (bt(s)[0],) + (0,) * nd)

    x_spec = pl.BlockSpec((tt, D_MODEL), lambda s: (jnp.maximum(s - 1, 0), 0))
    state_shapes = [(POOL_BUF, A_WIDTH), (CONV_W - 1, B_CONV_CH), (B_HEADS, B_DK, B_DV),
                    (C_HEADS // 2, 2 * C_DK, C_DV), (D_HEADS // 2, 2 * D_DK, 2 * D_DV), (8, LANES)]
    selb = _resident((2 * LANES, B_HEADS * LANES))
    seld = _resident((2 * LANES, D_HEADS * LANES))
    row128 = _resident((1, LANES))
    outs = pl.pallas_call(
        functools.partial(_mix_tail_kernel, tt=tt, chunk=chunk, nt=nt, n_tiles=n_tiles, pos0=pos0, final=final),
        grid=(n_tiles + 1,),
        in_specs=[pl.BlockSpec((1, tt, Z_WIDTH), lambda s: (*bt(s), 0)),
                  pl.BlockSpec((1, tt, LANES), lambda s: (*bt(s), 0))]
                 + [per_stream(sh) for sh in state_shapes] + [x_spec]
                 + [_resident((4, A_GROUP, A_GROUP)), _resident((1, A_WIDTH)),
                    _resident((CONV_W, B_CONV_CH)), row128, row128, _resident((1, B_DV)), selb, selb,
                    _resident((3 * LANES, C_QK)), _resident((1, C_QK)), _resident((1, C_DV)),
                    row128, row128, _resident((1, D_DV)), seld, seld]
                 + _tail_weight_specs(),
        out_specs=[x_spec] + [per_stream(sh) for sh in state_shapes],
        out_shape=[jax.ShapeDtypeStruct((n_tiles * tt, D_MODEL), F32)]
                  + [jax.ShapeDtypeStruct((b,) + sh, F32) for sh in state_shapes],
        scratch_shapes=[pltpu.VMEM((2, tt, N_BRANCH * BR_WIDTH), BF16),
                        pltpu.VMEM((POOL_HDR + tt, A_WIDTH), F32), pltpu.VMEM((CONV_HDR + tt, B_CONV_CH), F32)]
                       + [pltpu.VMEM(sh, F32) for sh in state_shapes[2:]],
        compiler_params=pltpu.CompilerParams(dimension_semantics=("arbitrary",), vmem_limit_bytes=VMEM_LIMIT),
        name="mix_tail",
    )(z3, sm3, a_buf, conv_buf, gdn_s, gla_s, cn0, m0, x2d,
      lp["pool_w"], lp["pool_scale"], lp["conv_w"], lp["alog_row"], lp["dt_row"], lp["gdn_norm"],
      _head_select(S_BA, B_HEADS), _head_select(S_BB, B_HEADS), lp["wup"], lp["bup"], lp["gla_norm"],
      lp["bi_row"], lp["bf_row"], lp["mlstm_norm"], _head_select(S_DF, D_HEADS), _head_select(S_DI, D_HEADS),
      *_tail_weights(lp, gfin))
    return outs[0], outs[1:]


def _lane_row(vec, off):
    return jnp.zeros((1, LANES), F32).at[0, off:off + vec.shape[0]].set(vec.astype(F32))


def _layer_params(l, p):
    w_in = p["w_in"][l]
    offs = [0]
    for s in IN_SPLITS:
        offs.append(offs[-1] + s)
    piece = lambda i: w_in[:, offs[i]:offs[i + 1]]
    wide_ids = (0, 1, 2, 3, 4, 7, 8, 9, 10, 12, 13, 14, 15)
    wz = jnp.concatenate([piece(i) for i in wide_ids], axis=1).astype(BF16)
    small = jnp.concatenate([piece(5), piece(6), piece(16), piece(17), piece(11)], axis=1)
    ws = jnp.zeros((D_MODEL, LANES), F32).at[:, :small.shape[1]].set(small).astype(BF16)
    wup = jnp.zeros((LANES, C_QK), F32).at[S_LR:S_LR + GLA_RANK].set(p["gla_w_up"][l].astype(F32))
    wup_hi = wup.astype(BF16)
    wup_lo = (wup - wup_hi.astype(F32)).astype(BF16)
    return dict(
        gmix=p["norm_mix"][l].reshape(1, D_MODEL), wz=wz, ws=ws,
        pool_w=p["pool_w"][l].astype(BF16), pool_scale=p["pool_scale"][l].reshape(1, A_WIDTH),
        conv_w=p["gdn_conv_w"][l],
        alog_row=_lane_row(-jnp.exp(p["gdn_a_log"][l].astype(F32)), S_BA),
        dt_row=_lane_row(p["gdn_dt_bias"][l], S_BA),
        gdn_norm=p["gdn_norm"][l].reshape(1, B_DV),
        wup=jnp.concatenate([wup_hi, wup_hi, wup_lo], axis=0),
        bup=p["gla_b_up"][l].reshape(1, C_QK), gla_norm=p["gla_norm"][l].reshape(1, C_DV),
        bi_row=_lane_row(p["mlstm_b_i"][l], S_DI), bf_row=_lane_row(p["mlstm_b_f"][l], S_DF),
        mlstm_norm=p["mlstm_norm"][l].reshape(1, D_DV),
        wgate=p["w_gate"][l].astype(BF16), bgate=p["b_gate"][l].reshape(4, 1, D_MODEL),
        wbr=p["w_branch"][l].astype(BF16), wout=p["w_out"][l].astype(BF16),
        gffn=p["norm_ffn"][l].reshape(1, D_MODEL),
        wfg=p["w_ffn_gate"][l].astype(BF16), wfu=p["w_ffn_up"][l].astype(BF16), wfd=p["w_ffn_down"][l].astype(BF16),
    )


def _layer(x, st, lp, gfin, pos0, chunk, final):
    b, t_len, _ = x.shape
    a_buf, conv_buf, gdn_s, gla_s, ml_c, ml_n, ml_m = st
    x2d = x.reshape(b * t_len, D_MODEL)
    z, sm = _project(x2d, lp["gmix"], lp["wz"], lp["ws"])
    z3 = z.reshape(b, t_len, Z_WIDTH)
    sm3 = sm.reshape(b, t_len, LANES)
    gla_s = gla_s.reshape(b, C_HEADS // 2, 2 * C_DK, C_DV)
    cn0 = jnp.concatenate([ml_c, jnp.broadcast_to(ml_n[..., None], ml_c.shape)], axis=-1)
    cn0 = cn0.reshape(b, D_HEADS // 2, 2 * D_DK, 2 * D_DV)
    m0 = jnp.zeros((b, 8, LANES), F32).at[:, :D_HEADS, :].set(jnp.broadcast_to(ml_m[..., None], (b, D_HEADS, LANES)))
    if t_len % TIME_TILE == 0:
        x_new, (a_new, conv_new, gdn_new, gla_new, cn_new, m_new) = _mix_tail(
            x2d, z3, sm3, (a_buf, conv_buf, gdn_s, gla_s, cn0, m0), lp, gfin, pos0, chunk, final)
    else:
        y_a, a_new = _pool_mixer(z3, a_buf, lp["pool_w"], lp["pool_scale"], pos0)
        y_b, conv_new, gdn_new = _gdn_mixer(z3, sm3, conv_buf, gdn_s, lp["conv_w"], lp["alog_row"], lp["dt_row"],
                                            lp["gdn_norm"], chunk)
        y_c, gla_new = _gla_mixer(z3, sm3, gla_s, lp["wup"], lp["bup"], lp["gla_norm"], chunk)
        y_d, cn_new, m_new = _mlstm_mixer(z3, sm3, cn0, m0, lp["bi_row"], lp["bf_row"], lp["mlstm_norm"], chunk)
        ys = [y.reshape(b * t_len, BR_WIDTH) for y in (y_a, y_b, y_c, y_d)]
        x_new = _tail(x2d, ys, lp, gfin, final)
    gla_new = gla_new.reshape(b, C_HEADS, C_DK, C_DV)
    cn_new = cn_new.reshape(b, D_HEADS, D_DK, 2 * D_DV)
    new_st = (a_new, conv_new, gdn_new, gla_new, cn_new[..., :D_DV], cn_new[..., D_DV], m_new[:, :D_HEADS, 0])
    return x_new.reshape(b, t_len, D_MODEL), new_st


def kernel(x_prompt, x_sample, state_a_pool, state_b_conv, state_b_S, state_c_S, state_d_C, state_d_n, state_d_m,
           norm_mix, w_in, pool_w, pool_scale, gdn_conv_w, gdn_a_log, gdn_dt_bias, gdn_norm,
           gla_w_up, gla_b_up, gla_norm, mlstm_b_i, mlstm_b_f, mlstm_norm,
           w_branch, w_gate, b_gate, w_out, norm_ffn, w_ffn_gate, w_ffn_up, w_ffn_down, norm_final):
    p = dict(norm_mix=norm_mix, w_in=w_in, pool_w=pool_w, pool_scale=pool_scale, gdn_conv_w=gdn_conv_w,
             gdn_a_log=gdn_a_log, gdn_dt_bias=gdn_dt_bias, gdn_norm=gdn_norm, gla_w_up=gla_w_up, gla_b_up=gla_b_up,
             gla_norm=gla_norm, mlstm_b_i=mlstm_b_i, mlstm_b_f=mlstm_b_f, mlstm_norm=mlstm_norm,
             w_branch=w_branch, w_gate=w_gate, b_gate=b_gate, w_out=w_out, norm_ffn=norm_ffn,
             w_ffn_gate=w_ffn_gate, w_ffn_up=w_ffn_up, w_ffn_down=w_ffn_down)
    bp = x_prompt.shape[0]
    zero_p = (jnp.zeros((bp, POOL_BUF, A_WIDTH), F32), jnp.zeros((bp, CONV_W - 1, B_CONV_CH), F32),
              jnp.zeros((bp, B_HEADS, B_DK, B_DV), F32), jnp.zeros((bp, C_HEADS, C_DK, C_DV), F32),
              jnp.zeros((bp, D_HEADS, D_DK, D_DV), F32), jnp.zeros((bp, D_HEADS, D_DK), F32),
              jnp.zeros((bp, D_HEADS), F32))
    gfin = norm_final.reshape(1, D_MODEL)
    yp, ys = x_prompt, x_sample
    new_p, new_s = [], []
    for l in range(DEPTH):
        lp = _layer_params(l, p)
        final = l == DEPTH - 1
        yp, sp = _layer(yp, zero_p, lp, gfin, 0, CHUNK, final)
        cache_l = (state_a_pool[l], state_b_conv[l], state_b_S[l], state_c_S[l],
                   state_d_C[l], state_d_n[l], state_d_m[l])
        ys, ss = _layer(ys, cache_l, lp, gfin, PAST_LEN, x_sample.shape[1], final)
        new_p.append(sp)
        new_s.append(ss)
    outs_p = [jnp.stack([s[i] for s in new_p]) for i in range(7)]
    outs_s = [jnp.stack([s[i] for s in new_s]) for i in range(7)]
    return (yp, ys, *outs_p, *outs_s)
```

```python
import functools
import math

import jax
import jax.numpy as jnp
from jax import lax
from jax.experimental import pallas as pl
from jax.experimental.pallas import tpu as pltpu

F32 = jnp.float32
BF16 = jnp.bfloat16

D_MODEL = 1024
DEPTH = 2
PAST_LEN = 2048
CHUNK = 64
EPS = 1e-6
POOL_WINDOWS = (2, 4, 8, 16)
A_WIDTH = 512
A_GROUP = 128
POOL_BUF = 15
B_HEADS, B_DK, B_DV = 4, 128, 128
B_QK = B_HEADS * B_DK
B_V = B_HEADS * B_DV
CONV_W = 4
B_CONV_CH = 2 * B_QK + B_V
C_HEADS, C_DK, C_DV = 4, 64, 128
C_QK = C_HEADS * C_DK
C_V = C_HEADS * C_DV
GLA_RANK = 16
GLA_NORMALIZER = 16.0
D_HEADS, D_DK, D_DV = 4, 64, 128
D_QK = D_HEADS * D_DK
D_V = D_HEADS * D_DV
BR_WIDTH = 512
N_BRANCH = 4
FFN_HIDDEN = 2816
IN_SPLITS = (A_WIDTH, B_QK, B_QK, B_V, B_V, B_HEADS, B_HEADS, C_QK, C_QK, C_V, C_V, GLA_RANK,
             D_QK, D_QK, D_V, D_V, D_HEADS, D_HEADS)

LANES = 128
Z_WIDTH = 5632
Z_A, Z_BQ, Z_BK, Z_BV, Z_BG = 0, 512, 1024, 1536, 2048
Z_CQ, Z_CK, Z_CV, Z_CG = 2560, 2816, 3072, 3584
Z_DQ, Z_DK, Z_DV, Z_DO = 4096, 4352, 4608, 5120
S_BA, S_BB, S_DI, S_DF, S_LR = 0, 4, 8, 12, 16

ROW_TILE = 512
TIME_TILE = 256
TAIL_COLS = 512
FFN_CHUNK = 256
TAIL_RATIO = 3
POOL_HDR = 16
CONV_HDR = 8
VMEM_LIMIT = 61 * 1024 * 1024

_TN = (((0,), (0,)), ((), ()))
_NT = (((1,), (1,)), ((), ()))


def _bdot(a, b):
    return jnp.dot(a.astype(BF16), b.astype(BF16), preferred_element_type=F32)


def _bdot_nt(a, b):
    return lax.dot_general(a.astype(BF16), b.astype(BF16), _NT, preferred_element_type=F32)


def _bdot_tn(a, b):
    return lax.dot_general(a.astype(BF16), b.astype(BF16), _TN, preferred_element_type=F32)


def _sigmoid(x):
    return 1.0 / (1.0 + jnp.exp(-x))


def _softplus(x):
    return jnp.maximum(x, 0.0) + jnp.log(1.0 + jnp.exp(-jnp.abs(x)))


def _rms(x, gain):
    return x * lax.rsqrt(jnp.mean(x * x, axis=-1, keepdims=True) + EPS) * gain


def _tri(c, kind):
    r = lax.broadcasted_iota(jnp.int32, (c, c), 0)
    k = lax.broadcasted_iota(jnp.int32, (c, c), 1)
    return {"lower": r >= k, "upper": r <= k, "eye": r == k}[kind]


def _bf16_mask(mask):
    return mask.astype(F32).astype(BF16)


def _split_f32(x):
    hi = x.astype(BF16).astype(F32)
    return hi, x - hi


def _hi_lo_rows(x):
    hi, lo = _split_f32(x)
    return jnp.concatenate([hi.astype(BF16), lo.astype(BF16)], axis=0)


def _hi_lo_lanes(x):
    hi, lo = _split_f32(x)
    return jnp.concatenate([hi, lo], axis=1).astype(BF16)


def _resident(shape):
    nd = len(shape)
    return pl.BlockSpec(shape, lambda *_: (0,) * nd, pipeline_mode=pl.Buffered(1))


def _run(*gens, lead=None, ratio=1):
    def step(g):
        try:
            next(g)
            return True
        except StopIteration:
            return False

    gens = list(gens)
    issued = 0
    if lead is not None and not step(lead):
        lead = None
    while gens:
        for g in list(gens):
            if not step(g):
                gens.remove(g)
                continue
            issued += 1
            if lead is not None and issued % ratio == 0 and not step(lead):
                lead = None
    while lead is not None and step(lead):
        pass


def _head_select(off, n_heads):
    r = jnp.arange(2 * LANES)[:, None] % LANES
    c = jnp.arange(n_heads * LANES)[None, :] // LANES
    return (r == off + c).astype(BF16)


def _conv_silu(full_ref, cw_ref, r0, rows, c0, anchor=None):
    base = r0 + CONV_HDR - (CONV_W - 1)
    w0 = cw_ref[0:1, c0:c0 + LANES] if anchor is None else cw_ref[0:1, c0:c0 + LANES] + anchor
    acc = full_ref[base:base + rows, c0:c0 + LANES] * w0
    for j in range(1, CONV_W):
        acc = acc + full_ref[base + j:base + j + rows, c0:c0 + LANES] * cw_ref[j:j + 1, c0:c0 + LANES]
    return acc * _sigmoid(acc)


def _proj_kernel(x_ref, g_ref, wz_ref, ws_ref, z_ref, s_ref):
    hb = _rms(x_ref[...], g_ref[...]).astype(BF16)
    step = 512
    for c0 in range(0, Z_WIDTH, step):
        zc = jnp.dot(hb, wz_ref[:, c0:c0 + step], preferred_element_type=F32)
        if c0 in (Z_BG, Z_CG):
            zc = zc * _sigmoid(zc)
        elif c0 == Z_DO:
            zc = _sigmoid(zc)
        z_ref[:, c0:c0 + step] = zc.astype(BF16)
    s_ref[...] = jnp.dot(hb, ws_ref[...], preferred_element_type=F32)


def _project(x2d, gain, wz, ws):
    n = x2d.shape[0]
    r = min(ROW_TILE, n)
    return pl.pallas_call(
        _proj_kernel,
        grid=(n // r,),
        in_specs=[pl.BlockSpec((r, D_MODEL), lambda i: (i, 0)),
                  _resident((1, D_MODEL)), _resident((D_MODEL, Z_WIDTH)), _resident((D_MODEL, LANES))],
        out_specs=[pl.BlockSpec((r, Z_WIDTH), lambda i: (i, 0)), pl.BlockSpec((r, LANES), lambda i: (i, 0))],
        out_shape=[jax.ShapeDtypeStruct((n, Z_WIDTH), BF16), jax.ShapeDtypeStruct((n, LANES), F32)],
        compiler_params=pltpu.CompilerParams(dimension_semantics=("parallel",), vmem_limit_bytes=VMEM_LIMIT),
        name="proj",
    )(x2d, gain, wz, ws)


def _zero_anchor(progress):
    row = progress.get("row") if progress else None
    return None if row is None else row - row


def _pool_stages(u_ref, pw_ref, sc_ref, full_ref, store_y, pos, tt, progress=None):
    hdr = POOL_HDR
    full_ref[hdr:hdr + tt, :] = u_ref[0].astype(F32)
    yield
    for gi, w in enumerate(POOL_WINDOWS):
        cols = slice(gi * A_GROUP, (gi + 1) * A_GROUP)
        anchor = _zero_anchor(progress)
        u = full_ref[hdr:hdr + tt, cols] if anchor is None else full_ref[hdr:hdr + tt, cols] + anchor
        acc = u
        for j in range(1, w):
            acc = acc + full_ref[hdr - j:hdr - j + tt, cols]
        cnt = jnp.minimum(pos + 1, w).astype(F32)
        diff = acc / cnt - u
        y = _bdot(diff, pw_ref[gi]) * sc_ref[:, cols]
        store_y(slice(0, tt), cols, y.astype(BF16))
        yield


def _gdn_stages(q_ref, k_ref, v_ref, g_ref, sm_ref, cw_ref, alog_ref, dt_ref, nrm_ref, selg_ref, selb_ref,
                s_ref, full_ref, store_y, tt, chunk, progress=None):
    hdr = CONV_HDR
    full_ref[hdr:hdr + tt, 0:B_QK] = q_ref[0].astype(F32)
    full_ref[hdr:hdr + tt, B_QK:2 * B_QK] = k_ref[0].astype(F32)
    full_ref[hdr:hdr + tt, 2 * B_QK:] = v_ref[0].astype(F32)
    yield
    rep = LANES // chunk
    row = lax.broadcasted_iota(jnp.int32, (chunk, LANES), 0)
    col = lax.broadcasted_iota(jnp.int32, (chunk, LANES), 1) & (chunk - 1)
    causal = row >= col
    strict = row > col
    eye = (row == col).astype(F32)
    tril2 = jnp.concatenate([_bf16_mask(causal[:, :chunk])] * 2, axis=1)
    triu2 = jnp.concatenate([_bf16_mask(row <= col)] * 2, axis=0)
    ones = jnp.ones((LANES, LANES), BF16)
    heads = range(B_HEADS)
    chunks = range(tt // chunk)
    ch = [(c, h) for c in chunks for h in heads]
    rows = [slice(c * chunk, (c + 1) * chunk) for c in chunks]
    n_steps = int(math.log2(chunk)) - 1

    q, k, v = {}, {}, {}
    for c in chunks:
        anchor = _zero_anchor(progress)
        for h in heads:
            q[c, h] = _conv_silu(full_ref, cw_ref, c * chunk, chunk, h * B_DK, anchor)
            k[c, h] = _conv_silu(full_ref, cw_ref, c * chunk, chunk, B_QK + h * B_DK, anchor)
            v[c, h] = _conv_silu(full_ref, cw_ref, c * chunk, chunk, 2 * B_QK + h * B_DV, anchor)
        yield
    for c in chunks:
        sq = jnp.concatenate([q[c, h] * q[c, h] for h in heads] + [k[c, h] * k[c, h] for h in heads], axis=0)
        ssq = jnp.dot(sq.astype(BF16), ones, preferred_element_type=F32)
        for h in heads:
            q[c, h] = q[c, h] * (lax.rsqrt(ssq[h * chunk:(h + 1) * chunk, :] + EPS) * (B_DK ** -0.5))
            k[c, h] = k[c, h] * lax.rsqrt(ssq[(B_HEADS + h) * chunk:(B_HEADS + h + 1) * chunk, :] + EPS)
    yield
    sm = [sm_ref[0, rows[c], :] for c in chunks]
    log_alpha = [alog_ref[...] * _softplus(sm[c] + dt_ref[...]) for c in chunks]
    beta = [_sigmoid(sm[c]) for c in chunks]
    la2 = [_hi_lo_rows(log_alpha[c]) for c in chunks]
    g_col = [jnp.dot(tril2, la2[c], preferred_element_type=F32) for c in chunks]
    g_row = [lax.dot_general(la2[c], triu2, _TN, preferred_element_type=F32) for c in chunks]
    yield
    g_b = jnp.dot(_hi_lo_lanes(jnp.concatenate(g_col, axis=0)), selg_ref[...], preferred_element_type=F32)
    b_b = jnp.dot(_hi_lo_lanes(jnp.concatenate(beta, axis=0)), selb_ref[...], preferred_element_type=F32)
    gi = {(c, h): g_b[rows[c], h * LANES:(h + 1) * LANES] for c, h in ch}
    bt = {(c, h): b_b[rows[c], h * LANES:(h + 1) * LANES] for c, h in ch}
    gj = {(c, h): g_row[c][S_BA + h:S_BA + h + 1, :] for c, h in ch}
    yield
    decay, kb, eg, low = {}, {}, {}, {}
    for c in chunks:
        anchor = _zero_anchor(progress)
        for i in [(c, h) for h in heads]:
            if anchor is not None:
                gj[i] = gj[i] + anchor
            decay[i] = jnp.exp(jnp.where(causal, gi[i] - gj[i], -jnp.inf))
            kb[i] = k[i] * bt[i]
            eg[i] = jnp.exp(gi[i])
            low[i] = jnp.where(strict, _bdot_nt(kb[i], jnp.concatenate([k[i]] * rep, axis=0)) * decay[i], 0.0)
        yield
    lb = {i: low[i].astype(BF16) for i in ch}
    m = {i: jnp.dot(lb[i][:, :chunk], lb[i], preferred_element_type=F32) for i in ch}
    tinv = {i: eye - low[i] for i in ch}
    yield
    for step in range(n_steps):
        mb = {i: m[i].astype(BF16) for i in ch}
        tb = {i: tinv[i].astype(BF16) for i in ch}
        if step == n_steps - 1:
            tinv = {i: tinv[i] + jnp.dot(mb[i][:, :chunk], tb[i], preferred_element_type=F32) for i in ch}
        else:
            out = {i: jnp.dot(mb[i][:, :chunk], jnp.concatenate([mb[i], tb[i]], axis=1),
                              preferred_element_type=F32) for i in ch}
            m = {i: out[i][:, :LANES] for i in ch}
            tinv = {i: tinv[i] + out[i][:, LANES:] for i in ch}
        yield
    sol, attn, g_last, k_tail, q_dec = {}, {}, {}, {}, {}
    for c in chunks:
        for i in [(c, h) for h in heads]:
            sol[i] = _bdot(tinv[i][:, :chunk], jnp.concatenate([v[i] * bt[i], kb[i] * eg[i]], axis=-1))
            attn[i] = jnp.where(causal[:, :chunk], _bdot_nt(q[i], k[i]) * decay[i][:, :chunk], 0.0)
            g_last[i] = gi[i][chunk - 1:chunk, :]
            k_tail[i] = k[i] * jnp.exp(g_last[i] - gi[i])
            q_dec[i] = q[i] * eg[i]
        yield
    s = [s_ref[h] for h in heads]
    o = {}
    for c in chunks:
        ws = [_bdot(sol[c, h][:, B_DV:], s[h]) for h in heads]
        qs = [_bdot(q_dec[c, h], s[h]) for h in heads]
        yield
        v_new = [sol[c, h][:, :B_DV] - ws[h] for h in heads]
        for h in heads:
            o[c, h] = qs[h] + _bdot(attn[c, h], v_new[h])
        s = [s[h] * jnp.exp(g_last[c, h]) + _bdot_tn(k_tail[c, h], v_new[h]) for h in heads]
        yield
    for h in heads:
        s_ref[h] = s[h]
    for c in chunks:
        oo = jnp.concatenate([o[c, h] * o[c, h] for h in heads], axis=0)
        ms = jnp.dot(oo.astype(BF16), ones, preferred_element_type=F32) * (1.0 / B_DV)
        for h in heads:
            cols = slice(h * B_DV, (h + 1) * B_DV)
            gate = g_ref[0, rows[c], cols].astype(F32)
            y = o[c, h] * lax.rsqrt(ms[h * chunk:(h + 1) * chunk, :] + EPS) * nrm_ref[...] * gate
            store_y(rows[c], cols, y.astype(BF16))
        yield


def _gla_stages(q_ref, k_ref, v_ref, g_ref, sm_ref, wup_ref, bup_ref, nrm_ref, s_ref, store_y, tt, chunk):
    causal = _tri(chunk, "lower")
    tril2 = jnp.concatenate([_bf16_mask(causal)] * 2, axis=1)
    ones = jnp.ones((2 * chunk, LANES), BF16)
    heads = range(C_HEADS)
    pairs = range(C_HEADS // 2)
    chunks = range(tt // chunk)
    rows = [slice(c * chunk, (c + 1) * chunk) for c in chunks]
    vs = [slice(h * C_DV, (h + 1) * C_DV) for h in heads]
    sm_sp = [_split_f32(sm_ref[0, rows[c], :]) for c in chunks]
    lr = [jnp.dot(jnp.concatenate([sm_sp[c][0], sm_sp[c][1], sm_sp[c][0]], axis=1).astype(BF16), wup_ref[...],
                  preferred_element_type=F32) + bup_ref[...] for c in chunks]
    log_a = [-_softplus(-lr[c]) / GLA_NORMALIZER for c in chunks]
    yield
    la2 = [_hi_lo_rows(log_a[c]) for c in chunks]
    bcum = [jnp.dot(tril2, la2[c], preferred_element_type=F32) for c in chunks]
    d_col = [jnp.exp(lax.dot_general(la2[c], ones, _TN, preferred_element_type=F32)) for c in chunks]
    yield
    mid = [bcum[c][chunk // 2:chunk // 2 + 1, :] for c in chunks]
    last = [bcum[c][chunk - 1:chunk, :] for c in chunks]
    qf = [q_ref[0, rows[c], :].astype(F32) * (C_DK ** -0.5) for c in chunks]
    kf = [k_ref[0, rows[c], :].astype(F32) for c in chunks]
    q_in = [qf[c] * jnp.exp(bcum[c] - mid[c]) for c in chunks]
    k_in = [kf[c] * jnp.exp(mid[c] - bcum[c]) for c in chunks]
    q_x = [qf[c] * jnp.exp(bcum[c]) for c in chunks]
    k_t = [kf[c] * jnp.exp(last[c] - bcum[c]) for c in chunks]
    yield
    lane = lax.broadcasted_iota(jnp.int32, (chunk, LANES), 1)
    half = [(lane < C_DK) if h % 2 == 0 else (lane >= C_DK) for h in heads]
    grp = [slice((h // 2) * LANES, (h // 2 + 1) * LANES) for h in heads]
    v = [[v_ref[0, rows[c], vs[h]] for h in heads] for c in chunks]
    attn = [[jnp.where(causal, _bdot_nt(jnp.where(half[h], q_in[c][:, grp[h]], 0.0), k_in[c][:, grp[h]]), 0.0)
             for h in heads] for c in chunks]
    yield
    o_in = [[_bdot(attn[c][h], v[c][h]) for h in heads] for c in chunks]
    upd = [[_bdot_tn(jnp.where(half[h], k_t[c][:, grp[h]], 0.0), v[c][h]) for h in heads] for c in chunks]
    yield
    s = [[s_ref[p] for p in pairs]]
    for c in chunks:
        s.append([s[c][p] * d_col[c][p * LANES:(p + 1) * LANES, :] + upd[c][2 * p] + upd[c][2 * p + 1]
                  for p in pairs])
    for p in pairs:
        s_ref[p] = s[-1][p]
    o = [[_bdot(jnp.where(half[h], q_x[c][:, grp[h]], 0.0), s[c][h // 2]) + o_in[c][h] for h in heads]
         for c in chunks]
    yield
    inv_dv = jnp.full((LANES, LANES), 1.0 / C_DV, BF16)
    for c in chunks:
        oo = jnp.concatenate([o[c][h] * o[c][h] for h in heads], axis=0)
        ms = jnp.dot(oo.astype(BF16), inv_dv, preferred_element_type=F32)
        for h in heads:
            gate = g_ref[0, rows[c], vs[h]].astype(F32)
            y = o[c][h] * lax.rsqrt(ms[h * chunk:(h + 1) * chunk, :] + EPS) * nrm_ref[...] * gate
            store_y(rows[c], vs[h], y.astype(BF16))
        yield


def _mlstm_stages(q_ref, k_ref, v_ref, og_ref, sm_ref, bi_ref, bf_ref, nrm_ref, self_ref, seli_ref,
                  cn_ref, m_ref, store_y, tt, chunk):
    causal = _tri(chunk, "lower")
    tril2 = jnp.concatenate([_bf16_mask(causal)] * 2, axis=1)
    triu2 = jnp.concatenate([_bf16_mask(_tri(chunk, "upper"))] * 2, axis=0)
    eye2 = jnp.concatenate([_bf16_mask(_tri(chunk, "eye"))] * 2, axis=0)
    ones = jnp.ones((chunk, LANES), BF16)
    heads = range(D_HEADS)
    pairs = range(D_HEADS // 2)
    chunks = range(tt // chunk)
    n_chunks = len(chunks)
    ch = [(c, h) for c in chunks for h in heads]
    rows = [slice(c * chunk, (c + 1) * chunk) for c in chunks]
    vs = [slice(h * D_DV, (h + 1) * D_DV) for h in heads]
    sm = [sm_ref[0, rows[c], :] for c in chunks]
    i_pre = [sm[c] + bi_ref[...] for c in chunks]
    log_f = [-_softplus(-(sm[c] + bf_ref[...])) for c in chunks]
    lf2 = [_hi_lo_rows(log_f[c]) for c in chunks]
    ip2 = [_hi_lo_rows(i_pre[c]) for c in chunks]
    b_col = [jnp.dot(tril2, lf2[c], preferred_element_type=F32) for c in chunks]
    b_row = [lax.dot_general(lf2[c], triu2, _TN, preferred_element_type=F32) for c in chunks]
    i_row = [lax.dot_general(ip2[c], eye2, _TN, preferred_element_type=F32) for c in chunks]
    yield
    b_b = jnp.dot(_hi_lo_lanes(jnp.concatenate(b_col, axis=0)), self_ref[...], preferred_element_type=F32)
    i_b = jnp.dot(_hi_lo_lanes(jnp.concatenate(i_pre, axis=0)), seli_ref[...], preferred_element_type=F32)
    bi = {(c, h): b_b[rows[c], h * LANES:(h + 1) * LANES] for c, h in ch}
    ii = {(c, h): i_b[rows[c], h * LANES:(h + 1) * LANES] for c, h in ch}
    yield
    lane = lax.broadcasted_iota(jnp.int32, (chunk, LANES), 1)
    half = [(lane < D_DK) if h % 2 == 0 else (lane >= D_DK) for h in heads]
    grp = [slice((h // 2) * LANES, (h // 2 + 1) * LANES) for h in heads]
    qf = [q_ref[0, rows[c], :].astype(F32) * (D_DK ** -0.5) for c in chunks]
    kf = [k_ref[0, rows[c], :].astype(F32) for c in chunks]
    q, v1, log_w, lw_max, qk = {}, {}, {}, {}, {}
    for c in chunks:
        for h in heads:
            q[c, h] = jnp.where(half[h], qf[c][:, grp[h]], 0.0)
            v1[c, h] = jnp.concatenate([v_ref[0, rows[c], vs[h]], ones], axis=-1)
            log_w[c, h] = jnp.where(causal, bi[c, h][:, :chunk] - b_row[c][S_DF + h:S_DF + h + 1, :]
                                    + i_row[c][S_DI + h:S_DI + h + 1, :], -jnp.inf)
            lw_max[c, h] = jnp.max(log_w[c, h], axis=-1, keepdims=True)
            qk[c, h] = _bdot_nt(q[c, h], kf[c][:, grp[h]])
        yield
    b_last = {i: bi[i][chunk - 1:chunk, :] for i in ch}
    m_s = {(0, h): m_ref[h:h + 1, :] for h in heads}
    for c, h in ch:
        m_s[c + 1, h] = jnp.maximum(b_last[c, h] + m_s[c, h], lw_max[c, h][chunk - 1:chunk, :])
    for h in heads:
        m_ref[h:h + 1, :] = m_s[n_chunks, h]
    m_t, w_inter, pv, upd = {}, {}, {}, {}
    for c in chunks:
        for h in heads:
            log_inter = bi[c, h] + m_s[c, h]
            m_t[c, h] = jnp.maximum(log_inter, lw_max[c, h])
            w_inter[c, h] = jnp.exp(log_inter - m_t[c, h])
            p_w = jnp.exp(log_w[c, h] - m_t[c, h][:, :chunk]) * qk[c, h]
            pv[c, h] = _bdot(p_w, v1[c, h])
            w_k = jnp.exp(b_last[c, h] - bi[c, h] + ii[c, h] - m_s[c + 1, h])
            upd[c, h] = _bdot_tn(jnp.where(half[h], kf[c][:, grp[h]] * w_k, 0.0), v1[c, h])
        yield
    top = lax.broadcasted_iota(jnp.int32, (2 * D_DK, LANES), 0) < D_DK
    cn = {(0, p): cn_ref[p] for p in pairs}
    for c in chunks:
        for p in pairs:
            w_a, w_b = (jnp.exp(b_last[c, h] + m_s[c, h] - m_s[c + 1, h]) for h in (2 * p, 2 * p + 1))
            w_state = jnp.where(top, w_a, w_b)
            cn[c + 1, p] = jnp.concatenate([w_state, w_state], axis=1) * cn[c, p] + upd[c, 2 * p] + upd[c, 2 * p + 1]
    for p in pairs:
        cn_ref[p] = cn[n_chunks, p]
    qcn = {(c, h): _bdot(q[c, h], cn[c, h // 2]) for c, h in ch}
    yield
    inv_dv = jnp.full((LANES, LANES), 1.0 / D_DV, BF16)
    for c in chunks:
        hh = []
        for h in heads:
            num = w_inter[c, h] * qcn[c, h][:, :D_DV] + pv[c, h][:, :D_DV]
            den = w_inter[c, h] * qcn[c, h][:, D_DV:] + pv[c, h][:, D_DV:]
            hh.append(num / jnp.maximum(jnp.abs(den), jnp.exp(-m_t[c, h])))
        ms = jnp.dot(jnp.concatenate([x * x for x in hh], axis=0).astype(BF16), inv_dv, preferred_element_type=F32)
        for h in heads:
            gate = og_ref[0, rows[c], vs[h]].astype(F32)
            y = gate * (hh[h] * lax.rsqrt(ms[h * chunk:(h + 1) * chunk, :] + EPS) * nrm_ref[...])
            store_y(rows[c], vs[h], y.astype(BF16))
        yield


def _tail_stages(x_ref, read_y, gmix_ref, wgate_ref, bgate_ref, wbr_ref, wout_ref, gffn_ref, wfg_ref, wfu_ref,
                 wfd_ref, gfin_ref, o_ref, final, progress=None):
    x = x_ref[...]
    hb = _rms(x, gmix_ref[...]).astype(BF16)
    yield
    merged = []
    for c0 in range(0, D_MODEL, TAIL_COLS):
        cols = slice(c0, c0 + TAIL_COLS)
        acc = None
        for i in range(N_BRANCH):
            gate = _sigmoid(jnp.dot(hb, wgate_ref[i, :, cols], preferred_element_type=F32) + bgate_ref[i, :, cols])
            term = gate * jnp.dot(read_y(i), wbr_ref[i, :, cols], preferred_element_type=F32)
            acc = term if acc is None else acc + term
            if progress is not None:
                progress["row"] = term[0:1, 0:LANES]
            yield
        merged.append(acc.astype(BF16))
    mb = jnp.concatenate(merged, axis=1)
    x1 = []
    for c0 in range(0, D_MODEL, TAIL_COLS):
        cols = slice(c0, c0 + TAIL_COLS)
        x1.append(x[:, cols] + jnp.dot(mb, wout_ref[:, cols], preferred_element_type=F32))
        if progress is not None:
            progress["row"] = x1[-1][0:1, 0:LANES]
        yield
    x = jnp.concatenate(x1, axis=1)
    h2 = _rms(x, gffn_ref[...]).astype(BF16)

    def hidden(c):
        a = jnp.dot(h2, wfg_ref[:, c], preferred_element_type=F32)
        u = jnp.dot(h2, wfu_ref[:, c], preferred_element_type=F32)
        return (a * _sigmoid(a) * u).astype(BF16)

    bounds = list(range(0, FFN_HIDDEN, FFN_CHUNK)) + [FFN_HIDDEN]
    pieces = [slice(lo, hi) for lo, hi in zip(bounds[:-1], bounds[1:])]
    f = hidden(pieces[0])
    yield
    for n, piece in enumerate(pieces):
        f_next = hidden(pieces[n + 1]) if n + 1 < len(pieces) else None
        x = x + jnp.dot(f, wfd_ref[piece, :], preferred_element_type=F32)
        f = f_next
        if progress is not None:
            progress["row"] = x[0:1, 0:LANES]
        yield
    if final:
        x = _rms(x, gfin_ref[...])
    o_ref[...] = x


def _pool_kernel(u_ref, buf_ref, pw_ref, sc_ref, y_ref, nbuf_ref, full_ref, *, tt, pos0):
    t = pl.program_id(1)
    hdr = POOL_HDR

    @pl.when(t == 0)
    def _():
        full_ref[1:hdr, :] = buf_ref[0]

    @pl.when(t > 0)
    def _():
        full_ref[1:hdr, :] = full_ref[tt + 1:tt + hdr, :]

    def store_y(rows, cols, val):
        y_ref[0, rows, cols] = val

    pos = pos0 + t * tt + lax.broadcasted_iota(jnp.int32, (tt, 1), 0)
    _run(_pool_stages(u_ref, pw_ref, sc_ref, full_ref, store_y, pos, tt))
    nbuf_ref[0] = full_ref[tt + 1:tt + hdr, :]


def _pool_mixer(z3, buf, pool_w, scale, pos0):
    b, t_len, _ = z3.shape
    tt = min(TIME_TILE, t_len)
    return pl.pallas_call(
        functools.partial(_pool_kernel, tt=tt, pos0=pos0),
        grid=(b, t_len // tt),
        in_specs=[pl.BlockSpec((1, tt, A_WIDTH), lambda i, j: (i, j, Z_A // A_WIDTH)),
                  pl.BlockSpec((1, POOL_BUF, A_WIDTH), lambda i, j: (i, 0, 0)),
                  _resident((4, A_GROUP, A_GROUP)), _resident((1, A_WIDTH))],
        out_specs=[pl.BlockSpec((1, tt, A_WIDTH), lambda i, j: (i, j, 0)),
                   pl.BlockSpec((1, POOL_BUF, A_WIDTH), lambda i, j: (i, 0, 0))],
        out_shape=[jax.ShapeDtypeStruct((b, t_len, A_WIDTH), BF16),
                   jax.ShapeDtypeStruct((b, POOL_BUF, A_WIDTH), F32)],
        scratch_shapes=[pltpu.VMEM((POOL_HDR + tt, A_WIDTH), F32)],
        compiler_params=pltpu.CompilerParams(dimension_semantics=("parallel", "arbitrary"),
                                             vmem_limit_bytes=VMEM_LIMIT),
        name="pool",
    )(z3, buf, pool_w, scale)


def _gdn_kernel(q_ref, k_ref, v_ref, g_ref, sm_ref, cbuf_ref, s_in_ref, cw_ref, alog_ref, dt_ref, nrm_ref,
                selg_ref, selb_ref, y_ref, ncbuf_ref, s_ref, full_ref, *, tt, chunk):
    t = pl.program_id(1)
    hdr = CONV_HDR

    @pl.when(t == 0)
    def _():
        full_ref[hdr - 3:hdr, :] = cbuf_ref[0]
        s_ref[0] = s_in_ref[0]

    @pl.when(t > 0)
    def _():
        full_ref[hdr - 3:hdr, :] = full_ref[tt + hdr - 3:tt + hdr, :]

    def store_y(rows, cols, val):
        y_ref[0, rows, cols] = val

    _run(_gdn_stages(q_ref, k_ref, v_ref, g_ref, sm_ref, cw_ref, alog_ref, dt_ref, nrm_ref, selg_ref, selb_ref,
                     s_ref.at[0], full_ref, store_y, tt, chunk))
    ncbuf_ref[0] = full_ref[tt + hdr - 3:tt + hdr, :]


def _gdn_mixer(z3, sm3, cbuf, s0, conv_w, alog_row, dt_row, norm_row, chunk):
    b, t_len, _ = z3.shape
    tt = min(TIME_TILE, t_len)
    wide = lambda off: pl.BlockSpec((1, tt, 512), lambda i, j: (i, j, off // 512))
    sel = _resident((2 * LANES, B_HEADS * LANES))
    return pl.pallas_call(
        functools.partial(_gdn_kernel, tt=tt, chunk=chunk),
        grid=(b, t_len // tt),
        in_specs=[wide(Z_BQ), wide(Z_BK), wide(Z_BV), wide(Z_BG),
                  pl.BlockSpec((1, tt, LANES), lambda i, j: (i, j, 0)),
                  pl.BlockSpec((1, CONV_W - 1, B_CONV_CH), lambda i, j: (i, 0, 0)),
                  pl.BlockSpec((1, B_HEADS, B_DK, B_DV), lambda i, j: (i, 0, 0, 0)),
                  _resident((CONV_W, B_CONV_CH)), _resident((1, LANES)), _resident((1, LANES)),
                  _resident((1, B_DV)), sel, sel],
        out_specs=[pl.BlockSpec((1, tt, B_V), lambda i, j: (i, j, 0)),
                   pl.BlockSpec((1, CONV_W - 1, B_CONV_CH), lambda i, j: (i, 0, 0)),
                   pl.BlockSpec((1, B_HEADS, B_DK, B_DV), lambda i, j: (i, 0, 0, 0))],
        out_shape=[jax.ShapeDtypeStruct((b, t_len, B_V), BF16),
                   jax.ShapeDtypeStruct((b, CONV_W - 1, B_CONV_CH), F32),
                   jax.ShapeDtypeStruct((b, B_HEADS, B_DK, B_DV), F32)],
        scratch_shapes=[pltpu.VMEM((CONV_HDR + tt, B_CONV_CH), F32)],
        compiler_params=pltpu.CompilerParams(dimension_semantics=("parallel", "arbitrary"),
                                             vmem_limit_bytes=VMEM_LIMIT),
        name="gdn",
    )(z3, z3, z3, z3, sm3, cbuf, s0, conv_w, alog_row, dt_row, norm_row,
      _head_select(S_BA, B_HEADS), _head_select(S_BB, B_HEADS))


def _gla_kernel(q_ref, k_ref, v_ref, g_ref, sm_ref, s_in_ref, wup_ref, bup_ref, nrm_ref,
                y_ref, s_ref, *, tt, chunk):
    @pl.when(pl.program_id(1) == 0)
    def _():
        s_ref[0] = s_in_ref[0]

    def store_y(rows, cols, val):
        y_ref[0, rows, cols] = val

    _run(_gla_stages(q_ref, k_ref, v_ref, g_ref, sm_ref, wup_ref, bup_ref, nrm_ref, s_ref.at[0], store_y, tt, chunk))


def _gla_mixer(z3, sm3, s0, wup_pack, bup_row, norm_row, chunk):
    b, t_len, _ = z3.shape
    tt = min(TIME_TILE, t_len)
    blk = lambda w, off: pl.BlockSpec((1, tt, w), lambda i, j: (i, j, off // w))
    state = pl.BlockSpec((1, C_HEADS // 2, 2 * C_DK, C_DV), lambda i, j: (i, 0, 0, 0))
    return pl.pallas_call(
        functools.partial(_gla_kernel, tt=tt, chunk=chunk),
        grid=(b, t_len // tt),
        in_specs=[blk(C_QK, Z_CQ), blk(C_QK, Z_CK), blk(C_V, Z_CV), blk(C_V, Z_CG),
                  pl.BlockSpec((1, tt, LANES), lambda i, j: (i, j, 0)), state,
                  _resident((3 * LANES, C_QK)), _resident((1, C_QK)), _resident((1, C_DV))],
        out_specs=[pl.BlockSpec((1, tt, C_V), lambda i, j: (i, j, 0)), state],
        out_shape=[jax.ShapeDtypeStruct((b, t_len, C_V), BF16),
                   jax.ShapeDtypeStruct((b, C_HEADS // 2, 2 * C_DK, C_DV), F32)],
        compiler_params=pltpu.CompilerParams(dimension_semantics=("parallel", "arbitrary"),
                                             vmem_limit_bytes=VMEM_LIMIT),
        name="gla",
    )(z3, z3, z3, z3, sm3, s0, wup_pack, bup_row, norm_row)


def _mlstm_kernel(q_ref, k_ref, v_ref, og_ref, sm_ref, cn_in_ref, m_in_ref, bi_ref, bf_ref, nrm_ref,
                  self_ref, seli_ref, y_ref, cn_ref, m_ref, *, tt, chunk):
    @pl.when(pl.program_id(1) == 0)
    def _():
        cn_ref[0] = cn_in_ref[0]
        m_ref[0] = m_in_ref[0]

    def store_y(rows, cols, val):
        y_ref[0, rows, cols] = val

    _run(_mlstm_stages(q_ref, k_ref, v_ref, og_ref, sm_ref, bi_ref, bf_ref, nrm_ref, self_ref, seli_ref,
                       cn_ref.at[0], m_ref.at[0], store_y, tt, chunk))


def _mlstm_mixer(z3, sm3, cn0, m0, bi_row, bf_row, norm_row, chunk):
    b, t_len, _ = z3.shape
    tt = min(TIME_TILE, t_len)
    blk = lambda w, off: pl.BlockSpec((1, tt, w), lambda i, j: (i, j, off // w))
    cn_spec = pl.BlockSpec((1, D_HEADS // 2, 2 * D_DK, 2 * D_DV), lambda i, j: (i, 0, 0, 0))
    m_spec = pl.BlockSpec((1, 8, LANES), lambda i, j: (i, 0, 0))
    sel = _resident((2 * LANES, D_HEADS * LANES))
    return pl.pallas_call(
        functools.partial(_mlstm_kernel, tt=tt, chunk=chunk),
        grid=(b, t_len // tt),
        in_specs=[blk(D_QK, Z_DQ), blk(D_QK, Z_DK), blk(D_V, Z_DV), blk(D_V, Z_DO),
                  pl.BlockSpec((1, tt, LANES), lambda i, j: (i, j, 0)), cn_spec, m_spec,
                  _resident((1, LANES)), _resident((1, LANES)), _resident((1, D_DV)), sel, sel],
        out_specs=[pl.BlockSpec((1, tt, D_V), lambda i, j: (i, j, 0)), cn_spec, m_spec],
        out_shape=[jax.ShapeDtypeStruct((b, t_len, D_V), BF16),
                   jax.ShapeDtypeStruct((b, D_HEADS // 2, 2 * D_DK, 2 * D_DV), F32),
                   jax.ShapeDtypeStruct((b, 8, LANES), F32)],
        compiler_params=pltpu.CompilerParams(dimension_semantics=("parallel", "arbitrary"),
                                             vmem_limit_bytes=VMEM_LIMIT),
        name="mlstm",
    )(z3, z3, z3, z3, sm3, cn0, m0, bi_row, bf_row, norm_row,
      _head_select(S_DF, D_HEADS), _head_select(S_DI, D_HEADS))


def _tail_kernel(x_ref, ya_ref, yb_ref, yc_ref, yd_ref, gmix_ref, wgate_ref, bgate_ref, wbr_ref, wout_ref,
                 gffn_ref, wfg_ref, wfu_ref, wfd_ref, gfin_ref, o_ref, *, final):
    ys = (ya_ref, yb_ref, yc_ref, yd_ref)
    _run(_tail_stages(x_ref, lambda i: ys[i][...], gmix_ref, wgate_ref, bgate_ref, wbr_ref, wout_ref, gffn_ref,
                      wfg_ref, wfu_ref, wfd_ref, gfin_ref, o_ref, final))


def _tail_weight_specs():
    return [_resident((1, D_MODEL)), _resident((N_BRANCH, D_MODEL, D_MODEL)), _resident((N_BRANCH, 1, D_MODEL)),
            _resident((N_BRANCH, BR_WIDTH, D_MODEL)), _resident((D_MODEL, D_MODEL)), _resident((1, D_MODEL)),
            _resident((D_MODEL, FFN_HIDDEN)), _resident((D_MODEL, FFN_HIDDEN)),
            _resident((FFN_HIDDEN, D_MODEL)), _resident((1, D_MODEL))]


def _tail_weights(lp, gfin):
    return (lp["gmix"], lp["wgate"], lp["bgate"], lp["wbr"], lp["wout"], lp["gffn"], lp["wfg"], lp["wfu"], lp["wfd"],
            gfin)


def _tail(x2d, ys, lp, gfin, final):
    n = x2d.shape[0]
    r = min(ROW_TILE, n)
    row = lambda w: pl.BlockSpec((r, w), lambda i: (i, 0))
    return pl.pallas_call(
        functools.partial(_tail_kernel, final=final),
        grid=(n // r,),
        in_specs=[row(D_MODEL)] + [row(BR_WIDTH)] * N_BRANCH + _tail_weight_specs(),
        out_specs=row(D_MODEL),
        out_shape=jax.ShapeDtypeStruct((n, D_MODEL), F32),
        compiler_params=pltpu.CompilerParams(dimension_semantics=("parallel",), vmem_limit_bytes=VMEM_LIMIT),
        name="tail",
    )(x2d, *ys, *_tail_weights(lp, gfin))


def _mix_tail_kernel(z_ref, sm_ref,
                     abuf_in, cbuf_in, gs_in, cs_in, cn_in, m_in, x_ref,
                     pw_ref, psc_ref, cw_ref, alog_ref, dt_ref, gnrm_ref, selg_ref, selb_ref,
                     wup_ref, bup_ref, cnrm_ref, bi_ref, bf_ref, dnrm_ref, self_ref, seli_ref,
                     gmix_ref, wgate_ref, bgate_ref, wbr_ref, wout_ref, gffn_ref, wfg_ref, wfu_ref, wfd_ref, gfin_ref,
                     o_ref, abuf_out, cbuf_out, gs_out, cs_out, cn_out, m_out,
                     ybuf, pfull, cfull, gs, cs, cn, m, *, tt, chunk, nt, n_tiles, pos0, final):
    s = pl.program_id(0)
    t = jnp.minimum(s, n_tiles - 1) % nt
    slot = s % 2
    piece = lambda off, w: z_ref.at[:, :, off:off + w]
    a_ref = piece(Z_A, A_WIDTH)
    bq_ref, bk_ref, bv_ref, bg_ref = piece(Z_BQ, B_QK), piece(Z_BK, B_QK), piece(Z_BV, B_V), piece(Z_BG, B_V)
    cq_ref, ck_ref, cv_ref, cg_ref = piece(Z_CQ, C_QK), piece(Z_CK, C_QK), piece(Z_CV, C_V), piece(Z_CG, C_V)
    dq_ref, dk_ref, dv_ref, do_ref = piece(Z_DQ, D_QK), piece(Z_DK, D_QK), piece(Z_DV, D_V), piece(Z_DO, D_V)

    @pl.when(s == 0)
    def _():
        ybuf[1] = jnp.zeros(ybuf.shape[1:], BF16)

    @pl.when(t == 0)
    def _():
        pfull[1:POOL_HDR, :] = abuf_in[0]
        cfull[CONV_HDR - 3:CONV_HDR, :] = cbuf_in[0]
        gs[...] = gs_in[0]
        cs[...] = cs_in[0]
        cn[...] = cn_in[0]
        m[...] = m_in[0]

    @pl.when(t > 0)
    def _():
        pfull[1:POOL_HDR, :] = pfull[tt + 1:tt + POOL_HDR, :]
        cfull[CONV_HDR - 3:CONV_HDR, :] = cfull[tt + CONV_HDR - 3:tt + CONV_HDR, :]

    def store_y(branch):
        def store(rows, cols, val):
            ybuf[slot, rows, slice(branch * BR_WIDTH + cols.start, branch * BR_WIDTH + cols.stop)] = val
        return store

    def read_y(i):
        return ybuf[1 - slot, :, i * BR_WIDTH:(i + 1) * BR_WIDTH]

    pos = pos0 + t * tt + lax.broadcasted_iota(jnp.int32, (tt, 1), 0)
    progress = {}
    _run(_gdn_stages(bq_ref, bk_ref, bv_ref, bg_ref, sm_ref, cw_ref, alog_ref, dt_ref, gnrm_ref, selg_ref, selb_ref,
                     gs, cfull, store_y(1), tt, chunk, progress),
         _mlstm_stages(dq_ref, dk_ref, dv_ref, do_ref, sm_ref, bi_ref, bf_ref, dnrm_ref, self_ref, seli_ref,
                       cn, m, store_y(3), tt, chunk),
         _gla_stages(cq_ref, ck_ref, cv_ref, cg_ref, sm_ref, wup_ref, bup_ref, cnrm_ref, cs, store_y(2), tt, chunk),
         _pool_stages(a_ref, pw_ref, psc_ref, pfull, store_y(0), pos, tt, progress),
         lead=_tail_stages(x_ref, read_y, gmix_ref, wgate_ref, bgate_ref, wbr_ref, wout_ref, gffn_ref, wfg_ref,
                           wfu_ref, wfd_ref, gfin_ref, o_ref, final, progress),
         ratio=TAIL_RATIO)

    @pl.when((t == nt - 1) & (s < n_tiles))
    def _():
        abuf_out[0] = pfull[tt + 1:tt + POOL_HDR, :]
        cbuf_out[0] = cfull[tt + CONV_HDR - 3:tt + CONV_HDR, :]
        gs_out[0] = gs[...]
        cs_out[0] = cs[...]
        cn_out[0] = cn[...]
        m_out[0] = m[...]


def _mix_tail(x2d, z3, sm3, states, lp, gfin, pos0, chunk, final):
    b, t_len, _ = z3.shape
    tt = TIME_TILE
    nt = t_len // tt
    n_tiles = b * nt
    a_buf, conv_buf, gdn_s, gla_s, cn0, m0 = states

    def bt(s):
        sm = jnp.minimum(s, n_tiles - 1)
        return sm // nt, sm % nt

    def per_stream(shape):
        nd = len(shape)
        return pl.BlockSpec((1,) + shape, lambda s: (bt(s)[0],) + (0,) * nd)

    x_spec = pl.BlockSpec((tt, D_MODEL), lambda s: (jnp.maximum(s - 1, 0), 0))
    state_shapes = [(POOL_BUF, A_WIDTH), (CONV_W - 1, B_CONV_CH), (B_HEADS, B_DK, B_DV),
                    (C_HEADS // 2, 2 * C_DK, C_DV), (D_HEADS // 2, 2 * D_DK, 2 * D_DV), (8, LANES)]
    selb = _resident((2 * LANES, B_HEADS * LANES))
    seld = _resident((2 * LANES, D_HEADS * LANES))
    row128 = _resident((1, LANES))
    outs = pl.pallas_call(
        functools.partial(_mix_tail_kernel, tt=tt, chunk=chunk, nt=nt, n_tiles=n_tiles, pos0=pos0, final=final),
        grid=(n_tiles + 1,),
        in_specs=[pl.BlockSpec((1, tt, Z_WIDTH), lambda s: (*bt(s), 0)),
                  pl.BlockSpec((1, tt, LANES), lambda s: (*bt(s), 0))]
                 + [per_stream(sh) for sh in state_shapes] + [x_spec]
                 + [_resident((4, A_GROUP, A_GROUP)), _resident((1, A_WIDTH)),
                    _resident((CONV_W, B_CONV_CH)), row128, row128, _resident((1, B_DV)), selb, selb,
                    _resident((3 * LANES, C_QK)), _resident((1, C_QK)), _resident((1, C_DV)),
                    row128, row128, _resident((1, D_DV)), seld, seld]
                 + _tail_weight_specs(),
        out_specs=[x_spec] + [per_stream(sh) for sh in state_shapes],
        out_shape=[jax.ShapeDtypeStruct((n_tiles * tt, D_MODEL), F32)]
                  + [jax.ShapeDtypeStruct((b,) + sh, F32) for sh in state_shapes],
        scratch_shapes=[pltpu.VMEM((2, tt, N_BRANCH * BR_WIDTH), BF16),
                        pltpu.VMEM((POOL_HDR + tt, A_WIDTH), F32), pltpu.VMEM((CONV_HDR + tt, B_CONV_CH), F32)]
                       + [pltpu.VMEM(sh, F32) for sh in state_shapes[2:]],
        compiler_params=pltpu.CompilerParams(dimension_semantics=("arbitrary",), vmem_limit_bytes=VMEM_LIMIT),
        name="mix_tail",
    )(z3, sm3, a_buf, conv_buf, gdn_s, gla_s, cn0, m0, x2d,
      lp["pool_w"], lp["pool_scale"], lp["conv_w"], lp["alog_row"], lp["dt_row"], lp["gdn_norm"],
      _head_select(S_BA, B_HEADS), _head_select(S_BB, B_HEADS), lp["wup"], lp["bup"], lp["gla_norm"],
      lp["bi_row"], lp["bf_row"], lp["mlstm_norm"], _head_select(S_DF, D_HEADS), _head_select(S_DI, D_HEADS),
      *_tail_weights(lp, gfin))
    return outs[0], outs[1:]


def _lane_row(vec, off):
    return jnp.zeros((1, LANES), F32).at[0, off:off + vec.shape[0]].set(vec.astype(F32))


def _layer_params(l, p):
    w_in = p["w_in"][l]
    offs = [0]
    for s in IN_SPLITS:
        offs.append(offs[-1] + s)
    piece = lambda i: w_in[:, offs[i]:offs[i + 1]]
    wide_ids = (0, 1, 2, 3, 4, 7, 8, 9, 10, 12, 13, 14, 15)
    wz = jnp.concatenate([piece(i) for i in wide_ids], axis=1).astype(BF16)
    small = jnp.concatenate([piece(5), piece(6), piece(16), piece(17), piece(11)], axis=1)
    ws = jnp.zeros((D_MODEL, LANES), F32).at[:, :small.shape[1]].set(small).astype(BF16)
    wup = jnp.zeros((LANES, C_QK), F32).at[S_LR:S_LR + GLA_RANK].set(p["gla_w_up"][l].astype(F32))
    wup_hi = wup.astype(BF16)
    wup_lo = (wup - wup_hi.astype(F32)).astype(BF16)
    return dict(
        gmix=p["norm_mix"][l].reshape(1, D_MODEL), wz=wz, ws=ws,
        pool_w=p["pool_w"][l].astype(BF16), pool_scale=p["pool_scale"][l].reshape(1, A_WIDTH),
        conv_w=p["gdn_conv_w"][l],
        alog_row=_lane_row(-jnp.exp(p["gdn_a_log"][l].astype(F32)), S_BA),
        dt_row=_lane_row(p["gdn_dt_bias"][l], S_BA),
        gdn_norm=p["gdn_norm"][l].reshape(1, B_DV),
        wup=jnp.concatenate([wup_hi, wup_hi, wup_lo], axis=0),
        bup=p["gla_b_up"][l].reshape(1, C_QK), gla_norm=p["gla_norm"][l].reshape(1, C_DV),
        bi_row=_lane_row(p["mlstm_b_i"][l], S_DI), bf_row=_lane_row(p["mlstm_b_f"][l], S_DF),
        mlstm_norm=p["mlstm_norm"][l].reshape(1, D_DV),
        wgate=p["w_gate"][l].astype(BF16), bgate=p["b_gate"][l].reshape(4, 1, D_MODEL),
        wbr=p["w_branch"][l].astype(BF16), wout=p["w_out"][l].astype(BF16),
        gffn=p["norm_ffn"][l].reshape(1, D_MODEL),
        wfg=p["w_ffn_gate"][l].astype(BF16), wfu=p["w_ffn_up"][l].astype(BF16), wfd=p["w_ffn_down"][l].astype(BF16),
    )


def _layer(x, st, lp, gfin, pos0, chunk, final):
    b, t_len, _ = x.shape
    a_buf, conv_buf, gdn_s, gla_s, ml_c, ml_n, ml_m = st
    x2d = x.reshape(b * t_len, D_MODEL)
    z, sm = _project(x2d, lp["gmix"], lp["wz"], lp["ws"])
    z3 = z.reshape(b, t_len, Z_WIDTH)
    sm3 = sm.reshape(b, t_len, LANES)
    gla_s = gla_s.reshape(b, C_HEADS // 2, 2 * C_DK, C_DV)
    cn0 = jnp.concatenate([ml_c, jnp.broadcast_to(ml_n[..., None], ml_c.shape)], axis=-1)
    cn0 = cn0.reshape(b, D_HEADS // 2, 2 * D_DK, 2 * D_DV)
    m0 = jnp.zeros((b, 8, LANES), F32).at[:, :D_HEADS, :].set(jnp.broadcast_to(ml_m[..., None], (b, D_HEADS, LANES)))
    if t_len % TIME_TILE == 0:
        x_new, (a_new, conv_new, gdn_new, gla_new, cn_new, m_new) = _mix_tail(
            x2d, z3, sm3, (a_buf, conv_buf, gdn_s, gla_s, cn0, m0), lp, gfin, pos0, chunk, final)
    else:
        y_a, a_new = _pool_mixer(z3, a_buf, lp["pool_w"], lp["pool_scale"], pos0)
        y_b, conv_new, gdn_new = _gdn_mixer(z3, sm3, conv_buf, gdn_s, lp["conv_w"], lp["alog_row"], lp["dt_row"],
                                            lp["gdn_norm"], chunk)
        y_c, gla_new = _gla_mixer(z3, sm3, gla_s, lp["wup"], lp["bup"], lp["gla_norm"], chunk)
        y_d, cn_new, m_new = _mlstm_mixer(z3, sm3, cn0, m0, lp["bi_row"], lp["bf_row"], lp["mlstm_norm"], chunk)
        ys = [y.reshape(b * t_len, BR_WIDTH) for y in (y_a, y_b, y_c, y_d)]
        x_new = _tail(x2d, ys, lp, gfin, final)
    gla_new = gla_new.reshape(b, C_HEADS, C_DK, C_DV)
    cn_new = cn_new.reshape(b, D_HEADS, D_DK, 2 * D_DV)
    new_st = (a_new, conv_new, gdn_new, gla_new, cn_new[..., :D_DV], cn_new[..., D_DV], m_new[:, :D_HEADS, 0])
    return x_new.reshape(b, t_len, D_MODEL), new_st


def kernel(x_prompt, x_sample, state_a_pool, state_b_conv, state_b_S, state_c_S, state_d_C, state_d_n, state_d_m,
           norm_mix, w_in, pool_w, pool_scale, gdn_conv_w, gdn_a_log, gdn_dt_bias, gdn_norm,
           gla_w_up, gla_b_up, gla_norm, mlstm_b_i, mlstm_b_f, mlstm_norm,
           w_branch, w_gate, b_gate, w_out, norm_ffn, w_ffn_gate, w_ffn_up, w_ffn_down, norm_final):
    p = dict(norm_mix=norm_mix, w_in=w_in, pool_w=pool_w, pool_scale=pool_scale, gdn_conv_w=gdn_conv_w,
             gdn_a_log=gdn_a_log, gdn_dt_bias=gdn_dt_bias, gdn_norm=gdn_norm, gla_w_up=gla_w_up, gla_b_up=gla_b_up,
             gla_norm=gla_norm, mlstm_b_i=mlstm_b_i, mlstm_b_f=mlstm_b_f, mlstm_norm=mlstm_norm,
             w_branch=w_branch, w_gate=w_gate, b_gate=b_gate, w_out=w_out, norm_ffn=norm_ffn,
             w_ffn_gate=w_ffn_gate, w_ffn_up=w_ffn_up, w_ffn_down=w_ffn_down)
    bp = x_prompt.shape[0]
    zero_p = (jnp.zeros((bp, POOL_BUF, A_WIDTH), F32), jnp.zeros((bp, CONV_W - 1, B_CONV_CH), F32),
              jnp.zeros((bp, B_HEADS, B_DK, B_DV), F32), jnp.zeros((bp, C_HEADS, C_DK, C_DV), F32),
              jnp.zeros((bp, D_HEADS, D_DK, D_DV), F32), jnp.zeros((bp, D_HEADS, D_DK), F32),
              jnp.zeros((bp, D_HEADS), F32))
    gfin = norm_final.reshape(1, D_MODEL)
    yp, ys = x_prompt, x_sample
    new_p, new_s = [], []
    for l in range(DEPTH):
        lp = _layer_params(l, p)
        final = l == DEPTH - 1
        yp, sp = _layer(yp, zero_p, lp, gfin, 0, CHUNK, final)
        cache_l = (state_a_pool[l], state_b_conv[l], state_b_S[l], state_c_S[l],
                   state_d_C[l], state_d_n[l], state_d_m[l])
        ys, ss = _layer(ys, cache_l, lp, gfin, PAST_LEN, x_sample.shape[1], final)
        new_p.append(sp)
        new_s.append(ss)
    outs_p = [jnp.stack([s[i] for s in new_p]) for i in range(7)]
    outs_s = [jnp.stack([s[i] for s in new_s]) for i in range(7)]
    return (yp, ys, *outs_p, *outs_s)
```

```python
import functools
import math

import jax
import jax.numpy as jnp
from jax import lax
from jax.experimental import pallas as pl
from jax.experimental.pallas import tpu as pltpu

F32 = jnp.float32
BF16 = jnp.bfloat16

D_MODEL = 1024
DEPTH = 2
PAST_LEN = 2048
CHUNK = 64
EPS = 1e-6
POOL_WINDOWS = (2, 4, 8, 16)
A_WIDTH = 512
A_GROUP = 128
POOL_BUF = 15
B_HEADS, B_DK, B_DV = 4, 128, 128
B_QK = B_HEADS * B_DK
B_V = B_HEADS * B_DV
CONV_W = 4
B_CONV_CH = 2 * B_QK + B_V
C_HEADS, C_DK, C_DV = 4, 64, 128
C_QK = C_HEADS * C_DK
C_V = C_HEADS * C_DV
GLA_RANK = 16
GLA_NORMALIZER = 16.0
D_HEADS, D_DK, D_DV = 4, 64, 128
D_QK = D_HEADS * D_DK
D_V = D_HEADS * D_DV
BR_WIDTH = 512
N_BRANCH = 4
FFN_HIDDEN = 2816
IN_SPLITS = (A_WIDTH, B_QK, B_QK, B_V, B_V, B_HEADS, B_HEADS, C_QK, C_QK, C_V, C_V, GLA_RANK,
             D_QK, D_QK, D_V, D_V, D_HEADS, D_HEADS)

LANES = 128
Z_WIDTH = 5632
Z_A, Z_BQ, Z_BK, Z_BV, Z_BG = 0, 512, 1024, 1536, 2048
Z_CQ, Z_CK, Z_CV, Z_CG = 2560, 2816, 3072, 3584
Z_DQ, Z_DK, Z_DV, Z_DO = 4096, 4352, 4608, 5120
S_BA, S_BB, S_DI, S_DF, S_LR = 0, 4, 8, 12, 16

ROW_TILE = 512
TIME_TILE = 256
TAIL_COLS = 512
FFN_CHUNK = 256
TAIL_RATIO = 3
POOL_HDR = 16
CONV_HDR = 8
VMEM_LIMIT = 61 * 1024 * 1024

_TN = (((0,), (0,)), ((), ()))
_NT = (((1,), (1,)), ((), ()))


def _bdot(a, b):
    return jnp.dot(a.astype(BF16), b.astype(BF16), preferred_element_type=F32)


def _bdot_nt(a, b):
    return lax.dot_general(a.astype(BF16), b.astype(BF16), _NT, preferred_element_type=F32)


def _bdot_tn(a, b):
    return lax.dot_general(a.astype(BF16), b.astype(BF16), _TN, preferred_element_type=F32)


def _sigmoid(x):
    return 1.0 / (1.0 + jnp.exp(-x))


def _softplus(x):
    return jnp.maximum(x, 0.0) + jnp.log(1.0 + jnp.exp(-jnp.abs(x)))


def _rms(x, gain):
    return x * lax.rsqrt(jnp.mean(x * x, axis=-1, keepdims=True) + EPS) * gain


def _tri(c, kind):
    r = lax.broadcasted_iota(jnp.int32, (c, c), 0)
    k = lax.broadcasted_iota(jnp.int32, (c, c), 1)
    return {"lower": r >= k, "upper": r <= k, "eye": r == k}[kind]


def _bf16_mask(mask):
    return mask.astype(F32).astype(BF16)


def _split_f32(x):
    hi = x.astype(BF16).astype(F32)
    return hi, x - hi


def _hi_lo_rows(x):
    hi, lo = _split_f32(x)
    return jnp.concatenate([hi.astype(BF16), lo.astype(BF16)], axis=0)


def _hi_lo_lanes(x):
    hi, lo = _split_f32(x)
    return jnp.concatenate([hi, lo], axis=1).astype(BF16)


def _resident(shape):
    nd = len(shape)
    return pl.BlockSpec(shape, lambda *_: (0,) * nd, pipeline_mode=pl.Buffered(1))


def _run(*gens, lead=None, ratio=1):
    def step(g):
        try:
            next(g)
            return True
        except StopIteration:
            return False

    gens = list(gens)
    issued = 0
    if lead is not None and not step(lead):
        lead = None
    while gens:
        for g in list(gens):
            if not step(g):
                gens.remove(g)
                continue
            issued += 1
            if lead is not None and issued % ratio == 0 and not step(lead):
                lead = None
    while lead is not None and step(lead):
        pass


def _head_select(off, n_heads):
    r = jnp.arange(2 * LANES)[:, None] % LANES
    c = jnp.arange(n_heads * LANES)[None, :] // LANES
    return (r == off + c).astype(BF16)


def _conv_silu(full_ref, cw_ref, r0, rows, c0, anchor=None):
    base = r0 + CONV_HDR - (CONV_W - 1)
    w0 = cw_ref[0:1, c0:c0 + LANES] if anchor is None else cw_ref[0:1, c0:c0 + LANES] + anchor
    acc = full_ref[base:base + rows, c0:c0 + LANES] * w0
    for j in range(1, CONV_W):
        acc = acc + full_ref[base + j:base + j + rows, c0:c0 + LANES] * cw_ref[j:j + 1, c0:c0 + LANES]
    return acc * _sigmoid(acc)


def _proj_kernel(x_ref, g_ref, wz_ref, ws_ref, z_ref, s_ref):
    hb = _rms(x_ref[...], g_ref[...]).astype(BF16)
    step = 512
    for c0 in range(0, Z_WIDTH, step):
        zc = jnp.dot(hb, wz_ref[:, c0:c0 + step], preferred_element_type=F32)
        if c0 in (Z_BG, Z_CG):
            zc = zc * _sigmoid(zc)
        elif c0 == Z_DO:
            zc = _sigmoid(zc)
        z_ref[:, c0:c0 + step] = zc.astype(BF16)
    s_ref[...] = jnp.dot(hb, ws_ref[...], preferred_element_type=F32)


def _project(x2d, gain, wz, ws):
    n = x2d.shape[0]
    r = min(ROW_TILE, n)
    return pl.pallas_call(
        _proj_kernel,
        grid=(n // r,),
        in_specs=[pl.BlockSpec((r, D_MODEL), lambda i: (i, 0)),
                  _resident((1, D_MODEL)), _resident((D_MODEL, Z_WIDTH)), _resident((D_MODEL, LANES))],
        out_specs=[pl.BlockSpec((r, Z_WIDTH), lambda i: (i, 0)), pl.BlockSpec((r, LANES), lambda i: (i, 0))],
        out_shape=[jax.ShapeDtypeStruct((n, Z_WIDTH), BF16), jax.ShapeDtypeStruct((n, LANES), F32)],
        compiler_params=pltpu.CompilerParams(dimension_semantics=("parallel",), vmem_limit_bytes=VMEM_LIMIT),
        name="proj",
    )(x2d, gain, wz, ws)


def _zero_anchor(progress):
    row = progress.get("row") if progress else None
    return None if row is None else row - row


def _pool_stages(u_ref, pw_ref, sc_ref, full_ref, store_y, pos, tt, progress=None):
    hdr = POOL_HDR
    full_ref[hdr:hdr + tt, :] = u_ref[0].astype(F32)
    yield
    for gi, w in enumerate(POOL_WINDOWS):
        cols = slice(gi * A_GROUP, (gi + 1) * A_GROUP)
        anchor = _zero_anchor(progress)
        u = full_ref[hdr:hdr + tt, cols] if anchor is None else full_ref[hdr:hdr + tt, cols] + anchor
        acc = u
        for j in range(1, w):
            acc = acc + full_ref[hdr - j:hdr - j + tt, cols]
        cnt = jnp.minimum(pos + 1, w).astype(F32)
        diff = acc / cnt - u
        y = _bdot(diff, pw_ref[gi]) * sc_ref[:, cols]
        store_y(slice(0, tt), cols, y.astype(BF16))
        yield


def _gdn_stages(q_ref, k_ref, v_ref, g_ref, sm_ref, cw_ref, alog_ref, dt_ref, nrm_ref, selg_ref, selb_ref,
                s_ref, full_ref, store_y, tt, chunk, progress=None):
    hdr = CONV_HDR
    full_ref[hdr:hdr + tt, 0:B_QK] = q_ref[0].astype(F32)
    full_ref[hdr:hdr + tt, B_QK:2 * B_QK] = k_ref[0].astype(F32)
    full_ref[hdr:hdr + tt, 2 * B_QK:] = v_ref[0].astype(F32)
    yield
    rep = LANES // chunk
    row = lax.broadcasted_iota(jnp.int32, (chunk, LANES), 0)
    col = lax.broadcasted_iota(jnp.int32, (chunk, LANES), 1) & (chunk - 1)
    causal = row >= col
    strict = row > col
    eye = (row == col).astype(F32)
    tril2 = jnp.concatenate([_bf16_mask(causal[:, :chunk])] * 2, axis=1)
    triu2 = jnp.concatenate([_bf16_mask(row <= col)] * 2, axis=0)
    ones = jnp.ones((LANES, LANES), BF16)
    heads = range(B_HEADS)
    chunks = range(tt // chunk)
    ch = [(c, h) for c in chunks for h in heads]
    rows = [slice(c * chunk, (c + 1) * chunk) for c in chunks]
    n_steps = int(math.log2(chunk)) - 1

    q, k, v = {}, {}, {}
    for c in chunks:
        anchor = _zero_anchor(progress)
        for h in heads:
            q[c, h] = _conv_silu(full_ref, cw_ref, c * chunk, chunk, h * B_DK, anchor)
            k[c, h] = _conv_silu(full_ref, cw_ref, c * chunk, chunk, B_QK + h * B_DK, anchor)
            v[c, h] = _conv_silu(full_ref, cw_ref, c * chunk, chunk, 2 * B_QK + h * B_DV, anchor)
        yield
    for c in chunks:
        sq = jnp.concatenate([q[c, h] * q[c, h] for h in heads] + [k[c, h] * k[c, h] for h in heads], axis=0)
        ssq = jnp.dot(sq.astype(BF16), ones, preferred_element_type=F32)
        for h in heads:
            q[c, h] = q[c, h] * (lax.rsqrt(ssq[h * chunk:(h + 1) * chunk, :] + EPS) * (B_DK ** -0.5))
            k[c, h] = k[c, h] * lax.rsqrt(ssq[(B_HEADS + h) * chunk:(B_HEADS + h + 1) * chunk, :] + EPS)
    yield
    sm = [sm_ref[0, rows[c], :] for c in chunks]
    log_alpha = [alog_ref[...] * _softplus(sm[c] + dt_ref[...]) for c in chunks]
    beta = [_sigmoid(sm[c]) for c in chunks]
    la2 = [_hi_lo_rows(log_alpha[c]) for c in chunks]
    g_col = [jnp.dot(tril2, la2[c], preferred_element_type=F32) for c in chunks]
    g_row = [lax.dot_general(la2[c], triu2, _TN, preferred_element_type=F32) for c in chunks]
    yield
    g_b = jnp.dot(_hi_lo_lanes(jnp.concatenate(g_col, axis=0)), selg_ref[...], preferred_element_type=F32)
    b_b = jnp.dot(_hi_lo_lanes(jnp.concatenate(beta, axis=0)), selb_ref[...], preferred_element_type=F32)
    gi = {(c, h): g_b[rows[c], h * LANES:(h + 1) * LANES] for c, h in ch}
    bt = {(c, h): b_b[rows[c], h * LANES:(h + 1) * LANES] for c, h in ch}
    gj = {(c, h): g_row[c][S_BA + h:S_BA + h + 1, :] for c, h in ch}
    yield
    decay, kb, eg, low = {}, {}, {}, {}
    for c in chunks:
        anchor = _zero_anchor(progress)
        for i in [(c, h) for h in heads]:
            if anchor is not None:
                gj[i] = gj[i] + anchor
            decay[i] = jnp.exp(jnp.where(causal, gi[i] - gj[i], -jnp.inf))
            kb[i] = k[i] * bt[i]
            eg[i] = jnp.exp(gi[i])
            low[i] = jnp.where(strict, _bdot_nt(kb[i], jnp.concatenate([k[i]] * rep, axis=0)) * decay[i], 0.0)
        yield
    lb = {i: low[i].astype(BF16) for i in ch}
    m = {i: jnp.dot(lb[i][:, :chunk], lb[i], preferred_element_type=F32) for i in ch}
    tinv = {i: eye - low[i] for i in ch}
    yield
    for step in range(n_steps):
        mb = {i: m[i].astype(BF16) for i in ch}
        tb = {i: tinv[i].astype(BF16) for i in ch}
        if step == n_steps - 1:
            tinv = {i: tinv[i] + jnp.dot(mb[i][:, :chunk], tb[i], preferred_element_type=F32) for i in ch}
        else:
            out = {i: jnp.dot(mb[i][:, :chunk], jnp.concatenate([mb[i], tb[i]], axis=1),
                              preferred_element_type=F32) for i in ch}
            m = {i: out[i][:, :LANES] for i in ch}
            tinv = {i: tinv[i] + out[i][:, LANES:] for i in ch}
        yield
    sol, attn, g_last, k_tail, q_dec = {}, {}, {}, {}, {}
    for c in chunks:
        anchor = _zero_anchor(progress)
        for i in [(c, h) for h in heads]:
            sol[i] = _bdot(tinv[i][:, :chunk], jnp.concatenate([v[i] * bt[i], kb[i] * eg[i]], axis=-1))
            attn[i] = jnp.where(causal[:, :chunk], _bdot_nt(q[i], k[i]) * decay[i][:, :chunk], 0.0)
            g_last[i] = gi[i][chunk - 1:chunk, :] if anchor is None else gi[i][chunk - 1:chunk, :] + anchor
            k_tail[i] = k[i] * jnp.exp(g_last[i] - gi[i])
            q_dec[i] = q[i] * eg[i]
        yield
    s = [s_ref[h] for h in heads]
    o = {}
    for c in chunks:
        ws = [_bdot(sol[c, h][:, B_DV:], s[h]) for h in heads]
        qs = [_bdot(q_dec[c, h], s[h]) for h in heads]
        yield
        v_new = [sol[c, h][:, :B_DV] - ws[h] for h in heads]
        for h in heads:
            o[c, h] = qs[h] + _bdot(attn[c, h], v_new[h])
        s = [s[h] * jnp.exp(g_last[c, h]) + _bdot_tn(k_tail[c, h], v_new[h]) for h in heads]
        yield
    for h in heads:
        s_ref[h] = s[h]
    for c in chunks:
        oo = jnp.concatenate([o[c, h] * o[c, h] for h in heads], axis=0)
        ms = jnp.dot(oo.astype(BF16), ones, preferred_element_type=F32) * (1.0 / B_DV)
        for h in heads:
            cols = slice(h * B_DV, (h + 1) * B_DV)
            gate = g_ref[0, rows[c], cols].astype(F32)
            y = o[c, h] * lax.rsqrt(ms[h * chunk:(h + 1) * chunk, :] + EPS) * nrm_ref[...] * gate
            store_y(rows[c], cols, y.astype(BF16))
        yield


def _gla_stages(q_ref, k_ref, v_ref, g_ref, sm_ref, wup_ref, bup_ref, nrm_ref, s_ref, store_y, tt, chunk):
    causal = _tri(chunk, "lower")
    tril2 = jnp.concatenate([_bf16_mask(causal)] * 2, axis=1)
    ones = jnp.ones((2 * chunk, LANES), BF16)
    heads = range(C_HEADS)
    pairs = range(C_HEADS // 2)
    chunks = range(tt // chunk)
    rows = [slice(c * chunk, (c + 1) * chunk) for c in chunks]
    vs = [slice(h * C_DV, (h + 1) * C_DV) for h in heads]
    sm_sp = [_split_f32(sm_ref[0, rows[c], :]) for c in chunks]
    lr = [jnp.dot(jnp.concatenate([sm_sp[c][0], sm_sp[c][1], sm_sp[c][0]], axis=1).astype(BF16), wup_ref[...],
                  preferred_element_type=F32) + bup_ref[...] for c in chunks]
    log_a = [-_softplus(-lr[c]) / GLA_NORMALIZER for c in chunks]
    yield
    la2 = [_hi_lo_rows(log_a[c]) for c in chunks]
    bcum = [jnp.dot(tril2, la2[c], preferred_element_type=F32) for c in chunks]
    d_col = [jnp.exp(lax.dot_general(la2[c], ones, _TN, preferred_element_type=F32)) for c in chunks]
    yield
    mid = [bcum[c][chunk // 2:chunk // 2 + 1, :] for c in chunks]
    last = [bcum[c][chunk - 1:chunk, :] for c in chunks]
    qf = [q_ref[0, rows[c], :].astype(F32) * (C_DK ** -0.5) for c in chunks]
    kf = [k_ref[0, rows[c], :].astype(F32) for c in chunks]
    q_in = [qf[c] * jnp.exp(bcum[c] - mid[c]) for c in chunks]
    k_in = [kf[c] * jnp.exp(mid[c] - bcum[c]) for c in chunks]
    q_x = [qf[c] * jnp.exp(bcum[c]) for c in chunks]
    k_t = [kf[c] * jnp.exp(last[c] - bcum[c]) for c in chunks]
    yield
    lane = lax.broadcasted_iota(jnp.int32, (chunk, LANES), 1)
    half = [(lane < C_DK) if h % 2 == 0 else (lane >= C_DK) for h in heads]
    grp = [slice((h // 2) * LANES, (h // 2 + 1) * LANES) for h in heads]
    v = [[v_ref[0, rows[c], vs[h]] for h in heads] for c in chunks]
    attn = [[jnp.where(causal, _bdot_nt(jnp.where(half[h], q_in[c][:, grp[h]], 0.0), k_in[c][:, grp[h]]), 0.0)
             for h in heads] for c in chunks]
    yield
    o_in = [[_bdot(attn[c][h], v[c][h]) for h in heads] for c in chunks]
    upd = [[_bdot_tn(jnp.where(half[h], k_t[c][:, grp[h]], 0.0), v[c][h]) for h in heads] for c in chunks]
    yield
    s = [[s_ref[p] for p in pairs]]
    for c in chunks:
        s.append([s[c][p] * d_col[c][p * LANES:(p + 1) * LANES, :] + upd[c][2 * p] + upd[c][2 * p + 1]
                  for p in pairs])
    for p in pairs:
        s_ref[p] = s[-1][p]
    o = [[_bdot(jnp.where(half[h], q_x[c][:, grp[h]], 0.0), s[c][h // 2]) + o_in[c][h] for h in heads]
         for c in chunks]
    yield
    inv_dv = jnp.full((LANES, LANES), 1.0 / C_DV, BF16)
    for c in chunks:
        oo = jnp.concatenate([o[c][h] * o[c][h] for h in heads], axis=0)
        ms = jnp.dot(oo.astype(BF16), inv_dv, preferred_element_type=F32)
        for h in heads:
            gate = g_ref[0, rows[c], vs[h]].astype(F32)
            y = o[c][h] * lax.rsqrt(ms[h * chunk:(h + 1) * chunk, :] + EPS) * nrm_ref[...] * gate
            store_y(rows[c], vs[h], y.astype(BF16))
        yield


def _mlstm_stages(q_ref, k_ref, v_ref, og_ref, sm_ref, bi_ref, bf_ref, nrm_ref, self_ref, seli_ref,
                  cn_ref, m_ref, store_y, tt, chunk, progress=None):
    causal = _tri(chunk, "lower")
    tril2 = jnp.concatenate([_bf16_mask(causal)] * 2, axis=1)
    triu2 = jnp.concatenate([_bf16_mask(_tri(chunk, "upper"))] * 2, axis=0)
    eye2 = jnp.concatenate([_bf16_mask(_tri(chunk, "eye"))] * 2, axis=0)
    ones = jnp.ones((chunk, LANES), BF16)
    heads = range(D_HEADS)
    pairs = range(D_HEADS // 2)
    chunks = range(tt // chunk)
    n_chunks = len(chunks)
    ch = [(c, h) for c in chunks for h in heads]
    rows = [slice(c * chunk, (c + 1) * chunk) for c in chunks]
    vs = [slice(h * D_DV, (h + 1) * D_DV) for h in heads]
    sm = [sm_ref[0, rows[c], :] for c in chunks]
    i_pre = [sm[c] + bi_ref[...] for c in chunks]
    log_f = [-_softplus(-(sm[c] + bf_ref[...])) for c in chunks]
    lf2 = [_hi_lo_rows(log_f[c]) for c in chunks]
    ip2 = [_hi_lo_rows(i_pre[c]) for c in chunks]
    b_col = [jnp.dot(tril2, lf2[c], preferred_element_type=F32) for c in chunks]
    b_row = [lax.dot_general(lf2[c], triu2, _TN, preferred_element_type=F32) for c in chunks]
    i_row = [lax.dot_general(ip2[c], eye2, _TN, preferred_element_type=F32) for c in chunks]
    yield
    b_b = jnp.dot(_hi_lo_lanes(jnp.concatenate(b_col, axis=0)), self_ref[...], preferred_element_type=F32)
    i_b = jnp.dot(_hi_lo_lanes(jnp.concatenate(i_pre, axis=0)), seli_ref[...], preferred_element_type=F32)
    bi = {(c, h): b_b[rows[c], h * LANES:(h + 1) * LANES] for c, h in ch}
    ii = {(c, h): i_b[rows[c], h * LANES:(h + 1) * LANES] for c, h in ch}
    yield
    lane = lax.broadcasted_iota(jnp.int32, (chunk, LANES), 1)
    half = [(lane < D_DK) if h % 2 == 0 else (lane >= D_DK) for h in heads]
    grp = [slice((h // 2) * LANES, (h // 2 + 1) * LANES) for h in heads]
    qf = [q_ref[0, rows[c], :].astype(F32) * (D_DK ** -0.5) for c in chunks]
    kf = [k_ref[0, rows[c], :].astype(F32) for c in chunks]
    q, v1, log_w, lw_max, qk = {}, {}, {}, {}, {}
    for c in chunks:
        anchor = _zero_anchor(progress)
        i_row_c = i_row[c] if anchor is None else i_row[c] + anchor[:, :chunk]
        for h in heads:
            q[c, h] = jnp.where(half[h], qf[c][:, grp[h]], 0.0)
            v1[c, h] = jnp.concatenate([v_ref[0, rows[c], vs[h]], ones], axis=-1)
            log_w[c, h] = jnp.where(causal, bi[c, h][:, :chunk] - b_row[c][S_DF + h:S_DF + h + 1, :]
                                    + i_row_c[S_DI + h:S_DI + h + 1, :], -jnp.inf)
            lw_max[c, h] = jnp.max(log_w[c, h], axis=-1, keepdims=True)
            qk[c, h] = _bdot_nt(q[c, h], kf[c][:, grp[h]])
        yield
    b_last = {i: bi[i][chunk - 1:chunk, :] for i in ch}
    m_s = {(0, h): m_ref[h:h + 1, :] for h in heads}
    for c, h in ch:
        m_s[c + 1, h] = jnp.maximum(b_last[c, h] + m_s[c, h], lw_max[c, h][chunk - 1:chunk, :])
    for h in heads:
        m_ref[h:h + 1, :] = m_s[n_chunks, h]
    m_t, w_inter, pv, upd = {}, {}, {}, {}
    for c in chunks:
        for h in heads:
            log_inter = bi[c, h] + m_s[c, h]
            m_t[c, h] = jnp.maximum(log_inter, lw_max[c, h])
            w_inter[c, h] = jnp.exp(log_inter - m_t[c, h])
            p_w = jnp.exp(log_w[c, h] - m_t[c, h][:, :chunk]) * qk[c, h]
            pv[c, h] = _bdot(p_w, v1[c, h])
            w_k = jnp.exp(b_last[c, h] - bi[c, h] + ii[c, h] - m_s[c + 1, h])
            upd[c, h] = _bdot_tn(jnp.where(half[h], kf[c][:, grp[h]] * w_k, 0.0), v1[c, h])
        yield
    top = lax.broadcasted_iota(jnp.int32, (2 * D_DK, LANES), 0) < D_DK
    cn = {(0, p): cn_ref[p] for p in pairs}
    for c in chunks:
        for p in pairs:
            w_a, w_b = (jnp.exp(b_last[c, h] + m_s[c, h] - m_s[c + 1, h]) for h in (2 * p, 2 * p + 1))
            w_state = jnp.where(top, w_a, w_b)
            cn[c + 1, p] = jnp.concatenate([w_state, w_state], axis=1) * cn[c, p] + upd[c, 2 * p] + upd[c, 2 * p + 1]
    for p in pairs:
        cn_ref[p] = cn[n_chunks, p]
    qcn = {(c, h): _bdot(q[c, h], cn[c, h // 2]) for c, h in ch}
    yield
    inv_dv = jnp.full((LANES, LANES), 1.0 / D_DV, BF16)
    for c in chunks:
        hh = []
        for h in heads:
            num = w_inter[c, h] * qcn[c, h][:, :D_DV] + pv[c, h][:, :D_DV]
            den = w_inter[c, h] * qcn[c, h][:, D_DV:] + pv[c, h][:, D_DV:]
            hh.append(num / jnp.maximum(jnp.abs(den), jnp.exp(-m_t[c, h])))
        ms = jnp.dot(jnp.concatenate([x * x for x in hh], axis=0).astype(BF16), inv_dv, preferred_element_type=F32)
        for h in heads:
            gate = og_ref[0, rows[c], vs[h]].astype(F32)
            y = gate * (hh[h] * lax.rsqrt(ms[h * chunk:(h + 1) * chunk, :] + EPS) * nrm_ref[...])
            store_y(rows[c], vs[h], y.astype(BF16))
        yield


def _tail_stages(x_ref, read_y, gmix_ref, wgate_ref, bgate_ref, wbr_ref, wout_ref, gffn_ref, wfg_ref, wfu_ref,
                 wfd_ref, gfin_ref, o_ref, final, progress=None):
    x = x_ref[...]
    hb = _rms(x, gmix_ref[...]).astype(BF16)
    yield
    merged = []
    for c0 in range(0, D_MODEL, TAIL_COLS):
        cols = slice(c0, c0 + TAIL_COLS)
        acc = None
        for i in range(N_BRANCH):
            gate = _sigmoid(jnp.dot(hb, wgate_ref[i, :, cols], preferred_element_type=F32) + bgate_ref[i, :, cols])
            term = gate * jnp.dot(read_y(i), wbr_ref[i, :, cols], preferred_element_type=F32)
            acc = term if acc is None else acc + term
            if progress is not None:
                progress["row"] = term[0:1, 0:LANES]
            yield
        merged.append(acc.astype(BF16))
    mb = jnp.concatenate(merged, axis=1)
    x1 = []
    for c0 in range(0, D_MODEL, TAIL_COLS):
        cols = slice(c0, c0 + TAIL_COLS)
        x1.append(x[:, cols] + jnp.dot(mb, wout_ref[:, cols], preferred_element_type=F32))
        if progress is not None:
            progress["row"] = x1[-1][0:1, 0:LANES]
        yield
    x = jnp.concatenate(x1, axis=1)
    h2 = _rms(x, gffn_ref[...]).astype(BF16)

    def hidden(c):
        a = jnp.dot(h2, wfg_ref[:, c], preferred_element_type=F32)
        u = jnp.dot(h2, wfu_ref[:, c], preferred_element_type=F32)
        return (a * _sigmoid(a) * u).astype(BF16)

    bounds = list(range(0, FFN_HIDDEN, FFN_CHUNK)) + [FFN_HIDDEN]
    pieces = [slice(lo, hi) for lo, hi in zip(bounds[:-1], bounds[1:])]
    f = hidden(pieces[0])
    yield
    for n, piece in enumerate(pieces):
        f_next = hidden(pieces[n + 1]) if n + 1 < len(pieces) else None
        x = x + jnp.dot(f, wfd_ref[piece, :], preferred_element_type=F32)
        f = f_next
        if progress is not None:
            progress["row"] = x[0:1, 0:LANES]
        yield
    if final:
        x = _rms(x, gfin_ref[...])
    o_ref[...] = x


def _pool_kernel(u_ref, buf_ref, pw_ref, sc_ref, y_ref, nbuf_ref, full_ref, *, tt, pos0):
    t = pl.program_id(1)
    hdr = POOL_HDR

    @pl.when(t == 0)
    def _():
        full_ref[1:hdr, :] = buf_ref[0]

    @pl.when(t > 0)
    def _():
        full_ref[1:hdr, :] = full_ref[tt + 1:tt + hdr, :]

    def store_y(rows, cols, val):
        y_ref[0, rows, cols] = val

    pos = pos0 + t * tt + lax.broadcasted_iota(jnp.int32, (tt, 1), 0)
    _run(_pool_stages(u_ref, pw_ref, sc_ref, full_ref, store_y, pos, tt))
    nbuf_ref[0] = full_ref[tt + 1:tt + hdr, :]


def _pool_mixer(z3, buf, pool_w, scale, pos0):
    b, t_len, _ = z3.shape
    tt = min(TIME_TILE, t_len)
    return pl.pallas_call(
        functools.partial(_pool_kernel, tt=tt, pos0=pos0),
        grid=(b, t_len // tt),
        in_specs=[pl.BlockSpec((1, tt, A_WIDTH), lambda i, j: (i, j, Z_A // A_WIDTH)),
                  pl.BlockSpec((1, POOL_BUF, A_WIDTH), lambda i, j: (i, 0, 0)),
                  _resident((4, A_GROUP, A_GROUP)), _resident((1, A_WIDTH))],
        out_specs=[pl.BlockSpec((1, tt, A_WIDTH), lambda i, j: (i, j, 0)),
                   pl.BlockSpec((1, POOL_BUF, A_WIDTH), lambda i, j: (i, 0, 0))],
        out_shape=[jax.ShapeDtypeStruct((b, t_len, A_WIDTH), BF16),
                   jax.ShapeDtypeStruct((b, POOL_BUF, A_WIDTH), F32)],
        scratch_shapes=[pltpu.VMEM((POOL_HDR + tt, A_WIDTH), F32)],
        compiler_params=pltpu.CompilerParams(dimension_semantics=("parallel", "arbitrary"),
                                             vmem_limit_bytes=VMEM_LIMIT),
        name="pool",
    )(z3, buf, pool_w, scale)


def _gdn_kernel(q_ref, k_ref, v_ref, g_ref, sm_ref, cbuf_ref, s_in_ref, cw_ref, alog_ref, dt_ref, nrm_ref,
                selg_ref, selb_ref, y_ref, ncbuf_ref, s_ref, full_ref, *, tt, chunk):
    t = pl.program_id(1)
    hdr = CONV_HDR

    @pl.when(t == 0)
    def _():
        full_ref[hdr - 3:hdr, :] = cbuf_ref[0]
        s_ref[0] = s_in_ref[0]

    @pl.when(t > 0)
    def _():
        full_ref[hdr - 3:hdr, :] = full_ref[tt + hdr - 3:tt + hdr, :]

    def store_y(rows, cols, val):
        y_ref[0, rows, cols] = val

    _run(_gdn_stages(q_ref, k_ref, v_ref, g_ref, sm_ref, cw_ref, alog_ref, dt_ref, nrm_ref, selg_ref, selb_ref,
                     s_ref.at[0], full_ref, store_y, tt, chunk))
    ncbuf_ref[0] = full_ref[tt + hdr - 3:tt + hdr, :]


def _gdn_mixer(z3, sm3, cbuf, s0, conv_w, alog_row, dt_row, norm_row, chunk):
    b, t_len, _ = z3.shape
    tt = min(TIME_TILE, t_len)
    wide = lambda off: pl.BlockSpec((1, tt, 512), lambda i, j: (i, j, off // 512))
    sel = _resident((2 * LANES, B_HEADS * LANES))
    return pl.pallas_call(
        functools.partial(_gdn_kernel, tt=tt, chunk=chunk),
        grid=(b, t_len // tt),
        in_specs=[wide(Z_BQ), wide(Z_BK), wide(Z_BV), wide(Z_BG),
                  pl.BlockSpec((1, tt, LANES), lambda i, j: (i, j, 0)),
                  pl.BlockSpec((1, CONV_W - 1, B_CONV_CH), lambda i, j: (i, 0, 0)),
                  pl.BlockSpec((1, B_HEADS, B_DK, B_DV), lambda i, j: (i, 0, 0, 0)),
                  _resident((CONV_W, B_CONV_CH)), _resident((1, LANES)), _resident((1, LANES)),
                  _resident((1, B_DV)), sel, sel],
        out_specs=[pl.BlockSpec((1, tt, B_V), lambda i, j: (i, j, 0)),
                   pl.BlockSpec((1, CONV_W - 1, B_CONV_CH), lambda i, j: (i, 0, 0)),
                   pl.BlockSpec((1, B_HEADS, B_DK, B_DV), lambda i, j: (i, 0, 0, 0))],
        out_shape=[jax.ShapeDtypeStruct((b, t_len, B_V), BF16),
                   jax.ShapeDtypeStruct((b, CONV_W - 1, B_CONV_CH), F32),
                   jax.ShapeDtypeStruct((b, B_HEADS, B_DK, B_DV), F32)],
        scratch_shapes=[pltpu.VMEM((CONV_HDR + tt, B_CONV_CH), F32)],
        compiler_params=pltpu.CompilerParams(dimension_semantics=("parallel", "arbitrary"),
                                             vmem_limit_bytes=VMEM_LIMIT),
        name="gdn",
    )(z3, z3, z3, z3, sm3, cbuf, s0, conv_w, alog_row, dt_row, norm_row,
      _head_select(S_BA, B_HEADS), _head_select(S_BB, B_HEADS))


def _gla_kernel(q_ref, k_ref, v_ref, g_ref, sm_ref, s_in_ref, wup_ref, bup_ref, nrm_ref,
                y_ref, s_ref, *, tt, chunk):
    @pl.when(pl.program_id(1) == 0)
    def _():
        s_ref[0] = s_in_ref[0]

    def store_y(rows, cols, val):
        y_ref[0, rows, cols] = val

    _run(_gla_stages(q_ref, k_ref, v_ref, g_ref, sm_ref, wup_ref, bup_ref, nrm_ref, s_ref.at[0], store_y, tt, chunk))


def _gla_mixer(z3, sm3, s0, wup_pack, bup_row, norm_row, chunk):
    b, t_len, _ = z3.shape
    tt = min(TIME_TILE, t_len)
    blk = lambda w, off: pl.BlockSpec((1, tt, w), lambda i, j: (i, j, off // w))
    state = pl.BlockSpec((1, C_HEADS // 2, 2 * C_DK, C_DV), lambda i, j: (i, 0, 0, 0))
    return pl.pallas_call(
        functools.partial(_gla_kernel, tt=tt, chunk=chunk),
        grid=(b, t_len // tt),
        in_specs=[blk(C_QK, Z_CQ), blk(C_QK, Z_CK), blk(C_V, Z_CV), blk(C_V, Z_CG),
                  pl.BlockSpec((1, tt, LANES), lambda i, j: (i, j, 0)), state,
                  _resident((3 * LANES, C_QK)), _resident((1, C_QK)), _resident((1, C_DV))],
        out_specs=[pl.BlockSpec((1, tt, C_V), lambda i, j: (i, j, 0)), state],
        out_shape=[jax.ShapeDtypeStruct((b, t_len, C_V), BF16),
                   jax.ShapeDtypeStruct((b, C_HEADS // 2, 2 * C_DK, C_DV), F32)],
        compiler_params=pltpu.CompilerParams(dimension_semantics=("parallel", "arbitrary"),
                                             vmem_limit_bytes=VMEM_LIMIT),
        name="gla",
    )(z3, z3, z3, z3, sm3, s0, wup_pack, bup_row, norm_row)


def _mlstm_kernel(q_ref, k_ref, v_ref, og_ref, sm_ref, cn_in_ref, m_in_ref, bi_ref, bf_ref, nrm_ref,
                  self_ref, seli_ref, y_ref, cn_ref, m_ref, *, tt, chunk):
    @pl.when(pl.program_id(1) == 0)
    def _():
        cn_ref[0] = cn_in_ref[0]
        m_ref[0] = m_in_ref[0]

    def store_y(rows, cols, val):
        y_ref[0, rows, cols] = val

    _run(_mlstm_stages(q_ref, k_ref, v_ref, og_ref, sm_ref, bi_ref, bf_ref, nrm_ref, self_ref, seli_ref,
                       cn_ref.at[0], m_ref.at[0], store_y, tt, chunk))


def _mlstm_mixer(z3, sm3, cn0, m0, bi_row, bf_row, norm_row, chunk):
    b, t_len, _ = z3.shape
    tt = min(TIME_TILE, t_len)
    blk = lambda w, off: pl.BlockSpec((1, tt, w), lambda i, j: (i, j, off // w))
    cn_spec = pl.BlockSpec((1, D_HEADS // 2, 2 * D_DK, 2 * D_DV), lambda i, j: (i, 0, 0, 0))
    m_spec = pl.BlockSpec((1, 8, LANES), lambda i, j: (i, 0, 0))
    sel = _resident((2 * LANES, D_HEADS * LANES))
    return pl.pallas_call(
        functools.partial(_mlstm_kernel, tt=tt, chunk=chunk),
        grid=(b, t_len // tt),
        in_specs=[blk(D_QK, Z_DQ), blk(D_QK, Z_DK), blk(D_V, Z_DV), blk(D_V, Z_DO),
                  pl.BlockSpec((1, tt, LANES), lambda i, j: (i, j, 0)), cn_spec, m_spec,
                  _resident((1, LANES)), _resident((1, LANES)), _resident((1, D_DV)), sel, sel],
        out_specs=[pl.BlockSpec((1, tt, D_V), lambda i, j: (i, j, 0)), cn_spec, m_spec],
        out_shape=[jax.ShapeDtypeStruct((b, t_len, D_V), BF16),
                   jax.ShapeDtypeStruct((b, D_HEADS // 2, 2 * D_DK, 2 * D_DV), F32),
                   jax.ShapeDtypeStruct((b, 8, LANES), F32)],
        compiler_params=pltpu.CompilerParams(dimension_semantics=("parallel", "arbitrary"),
                                             vmem_limit_bytes=VMEM_LIMIT),
        name="mlstm",
    )(z3, z3, z3, z3, sm3, cn0, m0, bi_row, bf_row, norm_row,
      _head_select(S_DF, D_HEADS), _head_select(S_DI, D_HEADS))


def _tail_kernel(x_ref, ya_ref, yb_ref, yc_ref, yd_ref, gmix_ref, wgate_ref, bgate_ref, wbr_ref, wout_ref,
                 gffn_ref, wfg_ref, wfu_ref, wfd_ref, gfin_ref, o_ref, *, final):
    ys = (ya_ref, yb_ref, yc_ref, yd_ref)
    _run(_tail_stages(x_ref, lambda i: ys[i][...], gmix_ref, wgate_ref, bgate_ref, wbr_ref, wout_ref, gffn_ref,
                      wfg_ref, wfu_ref, wfd_ref, gfin_ref, o_ref, final))


def _tail_weight_specs():
    return [_resident((1, D_MODEL)), _resident((N_BRANCH, D_MODEL, D_MODEL)), _resident((N_BRANCH, 1, D_MODEL)),
            _resident((N_BRANCH, BR_WIDTH, D_MODEL)), _resident((D_MODEL, D_MODEL)), _resident((1, D_MODEL)),
            _resident((D_MODEL, FFN_HIDDEN)), _resident((D_MODEL, FFN_HIDDEN)),
            _resident((FFN_HIDDEN, D_MODEL)), _resident((1, D_MODEL))]


def _tail_weights(lp, gfin):
    return (lp["gmix"], lp["wgate"], lp["bgate"], lp["wbr"], lp["wout"], lp["gffn"], lp["wfg"], lp["wfu"], lp["wfd"],
            gfin)


def _tail(x2d, ys, lp, gfin, final):
    n = x2d.shape[0]
    r = min(ROW_TILE, n)
    row = lambda w: pl.BlockSpec((r, w), lambda i: (i, 0))
    return pl.pallas_call(
        functools.partial(_tail_kernel, final=final),
        grid=(n // r,),
        in_specs=[row(D_MODEL)] + [row(BR_WIDTH)] * N_BRANCH + _tail_weight_specs(),
        out_specs=row(D_MODEL),
        out_shape=jax.ShapeDtypeStruct((n, D_MODEL), F32),
        compiler_params=pltpu.CompilerParams(dimension_semantics=("parallel",), vmem_limit_bytes=VMEM_LIMIT),
        name="tail",
    )(x2d, *ys, *_tail_weights(lp, gfin))


def _mix_tail_kernel(z_ref, sm_ref,
                     abuf_in, cbuf_in, gs_in, cs_in, cn_in, m_in, x_ref,
                     pw_ref, psc_ref, cw_ref, alog_ref, dt_ref, gnrm_ref, selg_ref, selb_ref,
                     wup_ref, bup_ref, cnrm_ref, bi_ref, bf_ref, dnrm_ref, self_ref, seli_ref,
                     gmix_ref, wgate_ref, bgate_ref, wbr_ref, wout_ref, gffn_ref, wfg_ref, wfu_ref, wfd_ref, gfin_ref,
                     o_ref, abuf_out, cbuf_out, gs_out, cs_out, cn_out, m_out,
                     ybuf, pfull, cfull, gs, cs, cn, m, *, tt, chunk, nt, n_tiles, pos0, final):
    s = pl.program_id(0)
    t = jnp.minimum(s, n_tiles - 1) % nt
    slot = s % 2
    piece = lambda off, w: z_ref.at[:, :, off:off + w]
    a_ref = piece(Z_A, A_WIDTH)
    bq_ref, bk_ref, bv_ref, bg_ref = piece(Z_BQ, B_QK), piece(Z_BK, B_QK), piece(Z_BV, B_V), piece(Z_BG, B_V)
    cq_ref, ck_ref, cv_ref, cg_ref = piece(Z_CQ, C_QK), piece(Z_CK, C_QK), piece(Z_CV, C_V), piece(Z_CG, C_V)
    dq_ref, dk_ref, dv_ref, do_ref = piece(Z_DQ, D_QK), piece(Z_DK, D_QK), piece(Z_DV, D_V), piece(Z_DO, D_V)

    @pl.when(s == 0)
    def _():
        ybuf[1] = jnp.zeros(ybuf.shape[1:], BF16)

    @pl.when(t == 0)
    def _():
        pfull[1:POOL_HDR, :] = abuf_in[0]
        cfull[CONV_HDR - 3:CONV_HDR, :] = cbuf_in[0]
        gs[...] = gs_in[0]
        cs[...] = cs_in[0]
        cn[...] = cn_in[0]
        m[...] = m_in[0]

    @pl.when(t > 0)
    def _():
        pfull[1:POOL_HDR, :] = pfull[tt + 1:tt + POOL_HDR, :]
        cfull[CONV_HDR - 3:CONV_HDR, :] = cfull[tt + CONV_HDR - 3:tt + CONV_HDR, :]

    def store_y(branch):
        def store(rows, cols, val):
            ybuf[slot, rows, slice(branch * BR_WIDTH + cols.start, branch * BR_WIDTH + cols.stop)] = val
        return store

    def read_y(i):
        return ybuf[1 - slot, :, i * BR_WIDTH:(i + 1) * BR_WIDTH]

    pos = pos0 + t * tt + lax.broadcasted_iota(jnp.int32, (tt, 1), 0)
    progress = {}
    _run(_gdn_stages(bq_ref, bk_ref, bv_ref, bg_ref, sm_ref, cw_ref, alog_ref, dt_ref, gnrm_ref, selg_ref, selb_ref,
                     gs, cfull, store_y(1), tt, chunk, progress),
         _mlstm_stages(dq_ref, dk_ref, dv_ref, do_ref, sm_ref, bi_ref, bf_ref, dnrm_ref, self_ref, seli_ref,
                       cn, m, store_y(3), tt, chunk, progress),
         _gla_stages(cq_ref, ck_ref, cv_ref, cg_ref, sm_ref, wup_ref, bup_ref, cnrm_ref, cs, store_y(2), tt, chunk),
         _pool_stages(a_ref, pw_ref, psc_ref, pfull, store_y(0), pos, tt, progress),
         lead=_tail_stages(x_ref, read_y, gmix_ref, wgate_ref, bgate_ref, wbr_ref, wout_ref, gffn_ref, wfg_ref,
                           wfu_ref, wfd_ref, gfin_ref, o_ref, final, progress),
         ratio=TAIL_RATIO)

    @pl.when((t == nt - 1) & (s < n_tiles))
    def _():
        abuf_out[0] = pfull[tt + 1:tt + POOL_HDR, :]
        cbuf_out[0] = cfull[tt + CONV_HDR - 3:tt + CONV_HDR, :]
        gs_out[0] = gs[...]
        cs_out[0] = cs[...]
        cn_out[0] = cn[...]
        m_out[0] = m[...]


def _mix_tail(x2d, z3, sm3, states, lp, gfin, pos0, chunk, final):
    b, t_len, _ = z3.shape
    tt = TIME_TILE
    nt = t_len // tt
    n_tiles = b * nt
    a_buf, conv_buf, gdn_s, gla_s, cn0, m0 = states

    def bt(s):
        sm = jnp.minimum(s, n_tiles - 1)
        return sm // nt, sm % nt

    def per_stream(shape):
        nd = len(shape)
        return pl.BlockSpec((1,) + shape, lambda s: (bt(s)[0],) + (0,) * nd)

    x_spec = pl.BlockSpec((tt, D_MODEL), lambda s: (jnp.maximum(s - 1, 0), 0))
    state_shapes = [(POOL_BUF, A_WIDTH), (CONV_W - 1, B_CONV_CH), (B_HEADS, B_DK, B_DV),
                    (C_HEADS // 2, 2 * C_DK, C_DV), (D_HEADS // 2, 2 * D_DK, 2 * D_DV), (8, LANES)]
    selb = _resident((2 * LANES, B_HEADS * LANES))
    seld = _resident((2 * LANES, D_HEADS * LANES))
    row128 = _resident((1, LANES))
    outs = pl.pallas_call(
        functools.partial(_mix_tail_kernel, tt=tt, chunk=chunk, nt=nt, n_tiles=n_tiles, pos0=pos0, final=final),
        grid=(n_tiles + 1,),
        in_specs=[pl.BlockSpec((1, tt, Z_WIDTH), lambda s: (*bt(s), 0)),
                  pl.BlockSpec((1, tt, LANES), lambda s: (*bt(s), 0))]
                 + [per_stream(sh) for sh in state_shapes] + [x_spec]
                 + [_resident((4, A_GROUP, A_GROUP)), _resident((1, A_WIDTH)),
                    _resident((CONV_W, B_CONV_CH)), row128, row128, _resident((1, B_DV)), selb, selb,
                    _resident((3 * LANES, C_QK)), _resident((1, C_QK)), _resident((1, C_DV)),
                    row128, row128, _resident((1, D_DV)), seld, seld]
                 + _tail_weight_specs(),
        out_specs=[x_spec] + [per_stream(sh) for sh in state_shapes],
        out_shape=[jax.ShapeDtypeStruct((n_tiles * tt, D_MODEL), F32)]
                  + [jax.ShapeDtypeStruct((b,) + sh, F32) for sh in state_shapes],
        scratch_shapes=[pltpu.VMEM((2, tt, N_BRANCH * BR_WIDTH), BF16),
                        pltpu.VMEM((POOL_HDR + tt, A_WIDTH), F32), pltpu.VMEM((CONV_HDR + tt, B_CONV_CH), F32)]
                       + [pltpu.VMEM(sh, F32) for sh in state_shapes[2:]],
        compiler_params=pltpu.CompilerParams(dimension_semantics=("arbitrary",), vmem_limit_bytes=VMEM_LIMIT),
        name="mix_tail",
    )(z3, sm3, a_buf, conv_buf, gdn_s, gla_s, cn0, m0, x2d,
      lp["pool_w"], lp["pool_scale"], lp["conv_w"], lp["alog_row"], lp["dt_row"], lp["gdn_norm"],
      _head_select(S_BA, B_HEADS), _head_select(S_BB, B_HEADS), lp["wup"], lp["bup"], lp["gla_norm"],
      lp["bi_row"], lp["bf_row"], lp["mlstm_norm"], _head_select(S_DF, D_HEADS), _head_select(S_DI, D_HEADS),
      *_tail_weights(lp, gfin))
    return outs[0], outs[1:]


def _lane_row(vec, off):
    return jnp.zeros((1, LANES), F32).at[0, off:off + vec.shape[0]].set(vec.astype(F32))


def _layer_params(l, p):
    w_in = p["w_in"][l]
    offs = [0]
    for s in IN_SPLITS:
        offs.append(offs[-1] + s)
    piece = lambda i: w_in[:, offs[i]:offs[i + 1]]
    wide_ids = (0, 1, 2, 3, 4, 7, 8, 9, 10, 12, 13, 14, 15)
    wz = jnp.concatenate([piece(i) for i in wide_ids], axis=1).astype(BF16)
    small = jnp.concatenate([piece(5), piece(6), piece(16), piece(17), piece(11)], axis=1)
    ws = jnp.zeros((D_MODEL, LANES), F32).at[:, :small.shape[1]].set(small).astype(BF16)
    wup = jnp.zeros((LANES, C_QK), F32).at[S_LR:S_LR + GLA_RANK].set(p["gla_w_up"][l].astype(F32))
    wup_hi = wup.astype(BF16)
    wup_lo = (wup - wup_hi.astype(F32)).astype(BF16)
    return dict(
        gmix=p["norm_mix"][l].reshape(1, D_MODEL), wz=wz, ws=ws,
        pool_w=p["pool_w"][l].astype(BF16), pool_scale=p["pool_scale"][l].reshape(1, A_WIDTH),
        conv_w=p["gdn_conv_w"][l],
        alog_row=_lane_row(-jnp.exp(p["gdn_a_log"][l].astype(F32)), S_BA),
        dt_row=_lane_row(p["gdn_dt_bias"][l], S_BA),
        gdn_norm=p["gdn_norm"][l].reshape(1, B_DV),
        wup=jnp.concatenate([wup_hi, wup_hi, wup_lo], axis=0),
        bup=p["gla_b_up"][l].reshape(1, C_QK), gla_norm=p["gla_norm"][l].reshape(1, C_DV),
        bi_row=_lane_row(p["mlstm_b_i"][l], S_DI), bf_row=_lane_row(p["mlstm_b_f"][l], S_DF),
        mlstm_norm=p["mlstm_norm"][l].reshape(1, D_DV),
        wgate=p["w_gate"][l].astype(BF16), bgate=p["b_gate"][l].reshape(4, 1, D_MODEL),
        wbr=p["w_branch"][l].astype(BF16), wout=p["w_out"][l].astype(BF16),
        gffn=p["norm_ffn"][l].reshape(1, D_MODEL),
        wfg=p["w_ffn_gate"][l].astype(BF16), wfu=p["w_ffn_up"][l].astype(BF16), wfd=p["w_ffn_down"][l].astype(BF16),
    )


def _layer(x, st, lp, gfin, pos0, chunk, final):
    b, t_len, _ = x.shape
    a_buf, conv_buf, gdn_s, gla_s, ml_c, ml_n, ml_m = st
    x2d = x.reshape(b * t_len, D_MODEL)
    z, sm = _project(x2d, lp["gmix"], lp["wz"], lp["ws"])
    z3 = z.reshape(b, t_len, Z_WIDTH)
    sm3 = sm.reshape(b, t_len, LANES)
    gla_s = gla_s.reshape(b, C_HEADS // 2, 2 * C_DK, C_DV)
    cn0 = jnp.concatenate([ml_c, jnp.broadcast_to(ml_n[..., None], ml_c.shape)], axis=-1)
    cn0 = cn0.reshape(b, D_HEADS // 2, 2 * D_DK, 2 * D_DV)
    m0 = jnp.zeros((b, 8, LANES), F32).at[:, :D_HEADS, :].set(jnp.broadcast_to(ml_m[..., None], (b, D_HEADS, LANES)))
    if t_len % TIME_TILE == 0:
        x_new, (a_new, conv_new, gdn_new, gla_new, cn_new, m_new) = _mix_tail(
            x2d, z3, sm3, (a_buf, conv_buf, gdn_s, gla_s, cn0, m0), lp, gfin, pos0, chunk, final)
    else:
        y_a, a_new = _pool_mixer(z3, a_buf, lp["pool_w"], lp["pool_scale"], pos0)
        y_b, conv_new, gdn_new = _gdn_mixer(z3, sm3, conv_buf, gdn_s, lp["conv_w"], lp["alog_row"], lp["dt_row"],
                                            lp["gdn_norm"], chunk)
        y_c, gla_new = _gla_mixer(z3, sm3, gla_s, lp["wup"], lp["bup"], lp["gla_norm"], chunk)
        y_d, cn_new, m_new = _mlstm_mixer(z3, sm3, cn0, m0, lp["bi_row"], lp["bf_row"], lp["mlstm_norm"], chunk)
        ys = [y.reshape(b * t_len, BR_WIDTH) for y in (y_a, y_b, y_c, y_d)]
        x_new = _tail(x2d, ys, lp, gfin, final)
    gla_new = gla_new.reshape(b, C_HEADS, C_DK, C_DV)
    cn_new = cn_new.reshape(b, D_HEADS, D_DK, 2 * D_DV)
    new_st = (a_new, conv_new, gdn_new, gla_new, cn_new[..., :D_DV], cn_new[..., D_DV], m_new[:, :D_HEADS, 0])
    return x_new.reshape(b, t_len, D_MODEL), new_st


def kernel(x_prompt, x_sample, state_a_pool, state_b_conv, state_b_S, state_c_S, state_d_C, state_d_n, state_d_m,
           norm_mix, w_in, pool_w, pool_scale, gdn_conv_w, gdn_a_log, gdn_dt_bias, gdn_norm,
           gla_w_up, gla_b_up, gla_norm, mlstm_b_i, mlstm_b_f, mlstm_norm,
           w_branch, w_gate, b_gate, w_out, norm_ffn, w_ffn_gate, w_ffn_up, w_ffn_down, norm_final):
    p = dict(norm_mix=norm_mix, w_in=w_in, pool_w=pool_w, pool_scale=pool_scale, gdn_conv_w=gdn_conv_w,
             gdn_a_log=gdn_a_log, gdn_dt_bias=gdn_dt_bias, gdn_norm=gdn_norm, gla_w_up=gla_w_up, gla_b_up=gla_b_up,
             gla_norm=gla_norm, mlstm_b_i=mlstm_b_i, mlstm_b_f=mlstm_b_f, mlstm_norm=mlstm_norm,
             w_branch=w_branch, w_gate=w_gate, b_gate=b_gate, w_out=w_out, norm_ffn=norm_ffn,
             w_ffn_gate=w_ffn_gate, w_ffn_up=w_ffn_up, w_ffn_down=w_ffn_down)
    bp = x_prompt.shape[0]
    zero_p = (jnp.zeros((bp, POOL_BUF, A_WIDTH), F32), jnp.zeros((bp, CONV_W - 1, B_CONV_CH), F32),
              jnp.zeros((bp, B_HEADS, B_DK, B_DV), F32), jnp.zeros((bp, C_HEADS, C_DK, C_DV), F32),
              jnp.zeros((bp, D_HEADS, D_DK, D_DV), F32), jnp.zeros((bp, D_HEADS, D_DK), F32),
              jnp.zeros((bp, D_HEADS), F32))
    gfin = norm_final.reshape(1, D_MODEL)
    yp, ys = x_prompt, x_sample
    new_p, new_s = [], []
    for l in range(DEPTH):
        lp = _layer_params(l, p)
        final = l == DEPTH - 1
        yp, sp = _layer(yp, zero_p, lp, gfin, 0, CHUNK, final)
        cache_l = (state_a_pool[l], state_b_conv[l], state_b_S[l], state_c_S[l],
                   state_d_C[l], state_d_n[l], state_d_m[l])
        ys, ss = _layer(ys, cache_l, lp, gfin, PAST_LEN, x_sample.shape[1], final)
        new_p.append(sp)
        new_s.append(ss)
    outs_p = [jnp.stack([s[i] for s in new_p]) for i in range(7)]
    outs_s = [jnp.stack([s[i] for s in new_s]) for i in range(7)]
    return (yp, ys, *outs_p, *outs_s)
```

```python
import functools
import math

import jax
import jax.numpy as jnp
from jax import lax
from jax.experimental import pallas as pl
from jax.experimental.pallas import tpu as pltpu

F32 = jnp.float32
BF16 = jnp.bfloat16

D_MODEL = 1024
DEPTH = 2
PAST_LEN = 2048
CHUNK = 64
EPS = 1e-6
POOL_WINDOWS = (2, 4, 8, 16)
A_WIDTH = 512
A_GROUP = 128
POOL_BUF = 15
B_HEADS, B_DK, B_DV = 4, 128, 128
B_QK = B_HEADS * B_DK
B_V = B_HEADS * B_DV
CONV_W = 4
B_CONV_CH = 2 * B_QK + B_V
C_HEADS, C_DK, C_DV = 4, 64, 128
C_QK = C_HEADS * C_DK
C_V = C_HEADS * C_DV
GLA_RANK = 16
GLA_NORMALIZER = 16.0
D_HEADS, D_DK, D_DV = 4, 64, 128
D_QK = D_HEADS * D_DK
D_V = D_HEADS * D_DV
BR_WIDTH = 512
N_BRANCH = 4
FFN_HIDDEN = 2816
IN_SPLITS = (A_WIDTH, B_QK, B_QK, B_V, B_V, B_HEADS, B_HEADS, C_QK, C_QK, C_V, C_V, GLA_RANK,
             D_QK, D_QK, D_V, D_V, D_HEADS, D_HEADS)

LANES = 128
Z_WIDTH = 5632
Z_A, Z_BQ, Z_BK, Z_BV, Z_BG = 0, 512, 1024, 1536, 2048
Z_CQ, Z_CK, Z_CV, Z_CG = 2560, 2816, 3072, 3584
Z_DQ, Z_DK, Z_DV, Z_DO = 4096, 4352, 4608, 5120
S_BA, S_BB, S_DI, S_DF, S_LR = 0, 4, 8, 12, 16

ROW_TILE = 512
TIME_TILE = 256
TAIL_COLS = 512
FFN_CHUNK = 256
TAIL_RATIO = 2
POOL_HDR = 16
CONV_HDR = 8
VMEM_LIMIT = 61 * 1024 * 1024

_TN = (((0,), (0,)), ((), ()))
_NT = (((1,), (1,)), ((), ()))


def _bdot(a, b):
    return jnp.dot(a.astype(BF16), b.astype(BF16), preferred_element_type=F32)


def _bdot_nt(a, b):
    return lax.dot_general(a.astype(BF16), b.astype(BF16), _NT, preferred_element_type=F32)


def _bdot_tn(a, b):
    return lax.dot_general(a.astype(BF16), b.astype(BF16), _TN, preferred_element_type=F32)


def _sigmoid(x):
    return 1.0 / (1.0 + jnp.exp(-x))


def _softplus(x):
    return jnp.maximum(x, 0.0) + jnp.log(1.0 + jnp.exp(-jnp.abs(x)))


def _rms(x, gain):
    return x * lax.rsqrt(jnp.mean(x * x, axis=-1, keepdims=True) + EPS) * gain


def _tri(c, kind):
    r = lax.broadcasted_iota(jnp.int32, (c, c), 0)
    k = lax.broadcasted_iota(jnp.int32, (c, c), 1)
    return {"lower": r >= k, "upper": r <= k, "eye": r == k}[kind]


def _bf16_mask(mask):
    return mask.astype(F32).astype(BF16)


def _split_f32(x):
    hi = x.astype(BF16).astype(F32)
    return hi, x - hi


def _hi_lo_rows(x):
    hi, lo = _split_f32(x)
    return jnp.concatenate([hi.astype(BF16), lo.astype(BF16)], axis=0)


def _hi_lo_lanes(x):
    hi, lo = _split_f32(x)
    return jnp.concatenate([hi, lo], axis=1).astype(BF16)


def _resident(shape):
    nd = len(shape)
    return pl.BlockSpec(shape, lambda *_: (0,) * nd, pipeline_mode=pl.Buffered(1))


def _run(*gens, lead=None, ratio=1):
    def step(g):
        try:
            next(g)
            return True
        except StopIteration:
            return False

    gens = list(gens)
    issued = 0
    if lead is not None and not step(lead):
        lead = None
    while gens:
        for g in list(gens):
            if not step(g):
                gens.remove(g)
                continue
            issued += 1
            if lead is not None and issued % ratio == 0 and not step(lead):
                lead = None
    while lead is not None and step(lead):
        pass


def _head_select(off, n_heads):
    r = jnp.arange(2 * LANES)[:, None] % LANES
    c = jnp.arange(n_heads * LANES)[None, :] // LANES
    return (r == off + c).astype(BF16)


def _conv_silu(full_ref, cw_ref, r0, rows, c0, anchor=None):
    base = r0 + CONV_HDR - (CONV_W - 1)
    w0 = cw_ref[0:1, c0:c0 + LANES] if anchor is None else cw_ref[0:1, c0:c0 + LANES] + anchor
    acc = full_ref[base:base + rows, c0:c0 + LANES] * w0
    for j in range(1, CONV_W):
        acc = acc + full_ref[base + j:base + j + rows, c0:c0 + LANES] * cw_ref[j:j + 1, c0:c0 + LANES]
    return acc * _sigmoid(acc)


def _proj_kernel(x_ref, g_ref, wz_ref, ws_ref, z_ref, s_ref):
    hb = _rms(x_ref[...], g_ref[...]).astype(BF16)
    step = 512
    for c0 in range(0, Z_WIDTH, step):
        zc = jnp.dot(hb, wz_ref[:, c0:c0 + step], preferred_element_type=F32)
        if c0 in (Z_BG, Z_CG):
            zc = zc * _sigmoid(zc)
        elif c0 == Z_DO:
            zc = _sigmoid(zc)
        z_ref[:, c0:c0 + step] = zc.astype(BF16)
    s_ref[...] = jnp.dot(hb, ws_ref[...], preferred_element_type=F32)


def _project(x2d, gain, wz, ws):
    n = x2d.shape[0]
    r = min(ROW_TILE, n)
    return pl.pallas_call(
        _proj_kernel,
        grid=(n // r,),
        in_specs=[pl.BlockSpec((r, D_MODEL), lambda i: (i, 0)),
                  _resident((1, D_MODEL)), _resident((D_MODEL, Z_WIDTH)), _resident((D_MODEL, LANES))],
        out_specs=[pl.BlockSpec((r, Z_WIDTH), lambda i: (i, 0)), pl.BlockSpec((r, LANES), lambda i: (i, 0))],
        out_shape=[jax.ShapeDtypeStruct((n, Z_WIDTH), BF16), jax.ShapeDtypeStruct((n, LANES), F32)],
        compiler_params=pltpu.CompilerParams(dimension_semantics=("parallel",), vmem_limit_bytes=VMEM_LIMIT),
        name="proj",
    )(x2d, gain, wz, ws)


def _zero_anchor(progress):
    row = progress.get("row") if progress else None
    return None if row is None else row - row


def _pool_stages(u_ref, pw_ref, sc_ref, full_ref, store_y, pos, tt, progress=None):
    hdr = POOL_HDR
    full_ref[hdr:hdr + tt, :] = u_ref[0].astype(F32)
    yield
    for gi, w in enumerate(POOL_WINDOWS):
        cols = slice(gi * A_GROUP, (gi + 1) * A_GROUP)
        anchor = _zero_anchor(progress)
        u = full_ref[hdr:hdr + tt, cols] if anchor is None else full_ref[hdr:hdr + tt, cols] + anchor
        acc = u
        for j in range(1, w):
            acc = acc + full_ref[hdr - j:hdr - j + tt, cols]
        cnt = jnp.minimum(pos + 1, w).astype(F32)
        diff = acc / cnt - u
        y = _bdot(diff, pw_ref[gi]) * sc_ref[:, cols]
        store_y(slice(0, tt), cols, y.astype(BF16))
        yield


def _gdn_stages(q_ref, k_ref, v_ref, g_ref, sm_ref, cw_ref, alog_ref, dt_ref, nrm_ref, selg_ref, selb_ref,
                s_ref, full_ref, store_y, tt, chunk, progress=None):
    hdr = CONV_HDR
    full_ref[hdr:hdr + tt, 0:B_QK] = q_ref[0].astype(F32)
    full_ref[hdr:hdr + tt, B_QK:2 * B_QK] = k_ref[0].astype(F32)
    full_ref[hdr:hdr + tt, 2 * B_QK:] = v_ref[0].astype(F32)
    yield
    rep = LANES // chunk
    row = lax.broadcasted_iota(jnp.int32, (chunk, LANES), 0)
    col = lax.broadcasted_iota(jnp.int32, (chunk, LANES), 1) & (chunk - 1)
    causal = row >= col
    strict = row > col
    eye = (row == col).astype(F32)
    tril2 = jnp.concatenate([_bf16_mask(causal[:, :chunk])] * 2, axis=1)
    triu2 = jnp.concatenate([_bf16_mask(row <= col)] * 2, axis=0)
    ones = jnp.ones((LANES, LANES), BF16)
    heads = range(B_HEADS)
    chunks = range(tt // chunk)
    ch = [(c, h) for c in chunks for h in heads]
    rows = [slice(c * chunk, (c + 1) * chunk) for c in chunks]
    n_steps = int(math.log2(chunk)) - 1

    q, k, v = {}, {}, {}
    for c in chunks:
        anchor = _zero_anchor(progress)
        for h in heads:
            q[c, h] = _conv_silu(full_ref, cw_ref, c * chunk, chunk, h * B_DK, anchor)
            k[c, h] = _conv_silu(full_ref, cw_ref, c * chunk, chunk, B_QK + h * B_DK, anchor)
            v[c, h] = _conv_silu(full_ref, cw_ref, c * chunk, chunk, 2 * B_QK + h * B_DV, anchor)
        yield
    for c in chunks:
        sq = jnp.concatenate([q[c, h] * q[c, h] for h in heads] + [k[c, h] * k[c, h] for h in heads], axis=0)
        ssq = jnp.dot(sq.astype(BF16), ones, preferred_element_type=F32)
        for h in heads:
            q[c, h] = q[c, h] * (lax.rsqrt(ssq[h * chunk:(h + 1) * chunk, :] + EPS) * (B_DK ** -0.5))
            k[c, h] = k[c, h] * lax.rsqrt(ssq[(B_HEADS + h) * chunk:(B_HEADS + h + 1) * chunk, :] + EPS)
    yield
    sm = [sm_ref[0, rows[c], :] for c in chunks]
    log_alpha = [alog_ref[...] * _softplus(sm[c] + dt_ref[...]) for c in chunks]
    beta = [_sigmoid(sm[c]) for c in chunks]
    la2 = [_hi_lo_rows(log_alpha[c]) for c in chunks]
    g_col = [jnp.dot(tril2, la2[c], preferred_element_type=F32) for c in chunks]
    g_row = [lax.dot_general(la2[c], triu2, _TN, preferred_element_type=F32) for c in chunks]
    yield
    g_b = jnp.dot(_hi_lo_lanes(jnp.concatenate(g_col, axis=0)), selg_ref[...], preferred_element_type=F32)
    b_b = jnp.dot(_hi_lo_lanes(jnp.concatenate(beta, axis=0)), selb_ref[...], preferred_element_type=F32)
    gi = {(c, h): g_b[rows[c], h * LANES:(h + 1) * LANES] for c, h in ch}
    bt = {(c, h): b_b[rows[c], h * LANES:(h + 1) * LANES] for c, h in ch}
    gj = {(c, h): g_row[c][S_BA + h:S_BA + h + 1, :] for c, h in ch}
    yield
    decay, kb, eg, low = {}, {}, {}, {}
    for c in chunks:
        anchor = _zero_anchor(progress)
        for i in [(c, h) for h in heads]:
            if anchor is not None:
                gj[i] = gj[i] + anchor
            decay[i] = jnp.exp(jnp.where(causal, gi[i] - gj[i], -jnp.inf))
            kb[i] = k[i] * bt[i]
            eg[i] = jnp.exp(gi[i])
            low[i] = jnp.where(strict, _bdot_nt(kb[i], jnp.concatenate([k[i]] * rep, axis=0)) * decay[i], 0.0)
        yield
    lb = {i: low[i].astype(BF16) for i in ch}
    m = {i: jnp.dot(lb[i][:, :chunk], lb[i], preferred_element_type=F32) for i in ch}
    tinv = {i: eye - low[i] for i in ch}
    yield
    for step in range(n_steps):
        mb = {i: m[i].astype(BF16) for i in ch}
        tb = {i: tinv[i].astype(BF16) for i in ch}
        if step == n_steps - 1:
            tinv = {i: tinv[i] + jnp.dot(mb[i][:, :chunk], tb[i], preferred_element_type=F32) for i in ch}
        else:
            out = {i: jnp.dot(mb[i][:, :chunk], jnp.concatenate([mb[i], tb[i]], axis=1),
                              preferred_element_type=F32) for i in ch}
            m = {i: out[i][:, :LANES] for i in ch}
            tinv = {i: tinv[i] + out[i][:, LANES:] for i in ch}
        yield
    sol, attn, g_last, k_tail, q_dec = {}, {}, {}, {}, {}
    for c in chunks:
        for i in [(c, h) for h in heads]:
            sol[i] = _bdot(tinv[i][:, :chunk], jnp.concatenate([v[i] * bt[i], kb[i] * eg[i]], axis=-1))
            attn[i] = jnp.where(causal[:, :chunk], _bdot_nt(q[i], k[i]) * decay[i][:, :chunk], 0.0)
            g_last[i] = gi[i][chunk - 1:chunk, :]
            k_tail[i] = k[i] * jnp.exp(g_last[i] - gi[i])
            q_dec[i] = q[i] * eg[i]
        yield
    s = [s_ref[h] for h in heads]
    o = {}
    for c in chunks:
        ws = [_bdot(sol[c, h][:, B_DV:], s[h]) for h in heads]
        qs = [_bdot(q_dec[c, h], s[h]) for h in heads]
        yield
        v_new = [sol[c, h][:, :B_DV] - ws[h] for h in heads]
        for h in heads:
            o[c, h] = qs[h] + _bdot(attn[c, h], v_new[h])
        s = [s[h] * jnp.exp(g_last[c, h]) + _bdot_tn(k_tail[c, h], v_new[h]) for h in heads]
        yield
    for h in heads:
        s_ref[h] = s[h]
    for c in chunks:
        oo = jnp.concatenate([o[c, h] * o[c, h] for h in heads], axis=0)
        ms = jnp.dot(oo.astype(BF16), ones, preferred_element_type=F32) * (1.0 / B_DV)
        for h in heads:
            cols = slice(h * B_DV, (h + 1) * B_DV)
            gate = g_ref[0, rows[c], cols].astype(F32)
            y = o[c, h] * lax.rsqrt(ms[h * chunk:(h + 1) * chunk, :] + EPS) * nrm_ref[...] * gate
            store_y(rows[c], cols, y.astype(BF16))
        yield


def _gla_stages(q_ref, k_ref, v_ref, g_ref, sm_ref, wup_ref, bup_ref, nrm_ref, s_ref, store_y, tt, chunk):
    causal = _tri(chunk, "lower")
    tril2 = jnp.concatenate([_bf16_mask(causal)] * 2, axis=1)
    ones = jnp.ones((2 * chunk, LANES), BF16)
    heads = range(C_HEADS)
    pairs = range(C_HEADS // 2)
    chunks = range(tt // chunk)
    rows = [slice(c * chunk, (c + 1) * chunk) for c in chunks]
    vs = [slice(h * C_DV, (h + 1) * C_DV) for h in heads]
    sm_sp = [_split_f32(sm_ref[0, rows[c], :]) for c in chunks]
    lr = [jnp.dot(jnp.concatenate([sm_sp[c][0], sm_sp[c][1], sm_sp[c][0]], axis=1).astype(BF16), wup_ref[...],
                  preferred_element_type=F32) + bup_ref[...] for c in chunks]
    log_a = [-_softplus(-lr[c]) / GLA_NORMALIZER for c in chunks]
    yield
    la2 = [_hi_lo_rows(log_a[c]) for c in chunks]
    bcum = [jnp.dot(tril2, la2[c], preferred_element_type=F32) for c in chunks]
    d_col = [jnp.exp(lax.dot_general(la2[c], ones, _TN, preferred_element_type=F32)) for c in chunks]
    yield
    mid = [bcum[c][chunk // 2:chunk // 2 + 1, :] for c in chunks]
    last = [bcum[c][chunk - 1:chunk, :] for c in chunks]
    qf = [q_ref[0, rows[c], :].astype(F32) * (C_DK ** -0.5) for c in chunks]
    kf = [k_ref[0, rows[c], :].astype(F32) for c in chunks]
    q_in = [qf[c] * jnp.exp(bcum[c] - mid[c]) for c in chunks]
    k_in = [kf[c] * jnp.exp(mid[c] - bcum[c]) for c in chunks]
    q_x = [qf[c] * jnp.exp(bcum[c]) for c in chunks]
    k_t = [kf[c] * jnp.exp(last[c] - bcum[c]) for c in chunks]
    yield
    lane = lax.broadcasted_iota(jnp.int32, (chunk, LANES), 1)
    half = [(lane < C_DK) if h % 2 == 0 else (lane >= C_DK) for h in heads]
    grp = [slice((h // 2) * LANES, (h // 2 + 1) * LANES) for h in heads]
    v = [[v_ref[0, rows[c], vs[h]] for h in heads] for c in chunks]
    attn = [[jnp.where(causal, _bdot_nt(jnp.where(half[h], q_in[c][:, grp[h]], 0.0), k_in[c][:, grp[h]]), 0.0)
             for h in heads] for c in chunks]
    yield
    o_in = [[_bdot(attn[c][h], v[c][h]) for h in heads] for c in chunks]
    upd = [[_bdot_tn(jnp.where(half[h], k_t[c][:, grp[h]], 0.0), v[c][h]) for h in heads] for c in chunks]
    yield
    s = [[s_ref[p] for p in pairs]]
    for c in chunks:
        s.append([s[c][p] * d_col[c][p * LANES:(p + 1) * LANES, :] + upd[c][2 * p] + upd[c][2 * p + 1]
                  for p in pairs])
    for p in pairs:
        s_ref[p] = s[-1][p]
    o = [[_bdot(jnp.where(half[h], q_x[c][:, grp[h]], 0.0), s[c][h // 2]) + o_in[c][h] for h in heads]
         for c in chunks]
    yield
    inv_dv = jnp.full((LANES, LANES), 1.0 / C_DV, BF16)
    for c in chunks:
        oo = jnp.concatenate([o[c][h] * o[c][h] for h in heads], axis=0)
        ms = jnp.dot(oo.astype(BF16), inv_dv, preferred_element_type=F32)
        for h in heads:
            gate = g_ref[0, rows[c], vs[h]].astype(F32)
            y = o[c][h] * lax.rsqrt(ms[h * chunk:(h + 1) * chunk, :] + EPS) * nrm_ref[...] * gate
            store_y(rows[c], vs[h], y.astype(BF16))
        yield


def _mlstm_stages(q_ref, k_ref, v_ref, og_ref, sm_ref, bi_ref, bf_ref, nrm_ref, self_ref, seli_ref,
                  cn_ref, m_ref, store_y, tt, chunk):
    causal = _tri(chunk, "lower")
    tril2 = jnp.concatenate([_bf16_mask(causal)] * 2, axis=1)
    triu2 = jnp.concatenate([_bf16_mask(_tri(chunk, "upper"))] * 2, axis=0)
    eye2 = jnp.concatenate([_bf16_mask(_tri(chunk, "eye"))] * 2, axis=0)
    ones = jnp.ones((chunk, LANES), BF16)
    heads = range(D_HEADS)
    pairs = range(D_HEADS // 2)
    chunks = range(tt // chunk)
    n_chunks = len(chunks)
    ch = [(c, h) for c in chunks for h in heads]
    rows = [slice(c * chunk, (c + 1) * chunk) for c in chunks]
    vs = [slice(h * D_DV, (h + 1) * D_DV) for h in heads]
    sm = [sm_ref[0, rows[c], :] for c in chunks]
    i_pre = [sm[c] + bi_ref[...] for c in chunks]
    log_f = [-_softplus(-(sm[c] + bf_ref[...])) for c in chunks]
    lf2 = [_hi_lo_rows(log_f[c]) for c in chunks]
    ip2 = [_hi_lo_rows(i_pre[c]) for c in chunks]
    b_col = [jnp.dot(tril2, lf2[c], preferred_element_type=F32) for c in chunks]
    b_row = [lax.dot_general(lf2[c], triu2, _TN, preferred_element_type=F32) for c in chunks]
    i_row = [lax.dot_general(ip2[c], eye2, _TN, preferred_element_type=F32) for c in chunks]
    yield
    b_b = jnp.dot(_hi_lo_lanes(jnp.concatenate(b_col, axis=0)), self_ref[...], preferred_element_type=F32)
    i_b = jnp.dot(_hi_lo_lanes(jnp.concatenate(i_pre, axis=0)), seli_ref[...], preferred_element_type=F32)
    bi = {(c, h): b_b[rows[c], h * LANES:(h + 1) * LANES] for c, h in ch}
    ii = {(c, h): i_b[rows[c], h * LANES:(h + 1) * LANES] for c, h in ch}
    yield
    lane = lax.broadcasted_iota(jnp.int32, (chunk, LANES), 1)
    half = [(lane < D_DK) if h % 2 == 0 else (lane >= D_DK) for h in heads]
    grp = [slice((h // 2) * LANES, (h // 2 + 1) * LANES) for h in heads]
    qf = [q_ref[0, rows[c], :].astype(F32) * (D_DK ** -0.5) for c in chunks]
    kf = [k_ref[0, rows[c], :].astype(F32) for c in chunks]
    q, v1, log_w, lw_max, qk = {}, {}, {}, {}, {}
    for c in chunks:
        for h in heads:
            q[c, h] = jnp.where(half[h], qf[c][:, grp[h]], 0.0)
            v1[c, h] = jnp.concatenate([v_ref[0, rows[c], vs[h]], ones], axis=-1)
            log_w[c, h] = jnp.where(causal, bi[c, h][:, :chunk] - b_row[c][S_DF + h:S_DF + h + 1, :]
                                    + i_row[c][S_DI + h:S_DI + h + 1, :], -jnp.inf)
            lw_max[c, h] = jnp.max(log_w[c, h], axis=-1, keepdims=True)
            qk[c, h] = _bdot_nt(q[c, h], kf[c][:, grp[h]])
        yield
    b_last = {i: bi[i][chunk - 1:chunk, :] for i in ch}
    m_s = {(0, h): m_ref[h:h + 1, :] for h in heads}
    for c, h in ch:
        m_s[c + 1, h] = jnp.maximum(b_last[c, h] + m_s[c, h], lw_max[c, h][chunk - 1:chunk, :])
    for h in heads:
        m_ref[h:h + 1, :] = m_s[n_chunks, h]
    m_t, w_inter, pv, upd = {}, {}, {}, {}
    for c in chunks:
        for h in heads:
            log_inter = bi[c, h] + m_s[c, h]
            m_t[c, h] = jnp.maximum(log_inter, lw_max[c, h])
            w_inter[c, h] = jnp.exp(log_inter - m_t[c, h])
            p_w = jnp.exp(log_w[c, h] - m_t[c, h][:, :chunk]) * qk[c, h]
            pv[c, h] = _bdot(p_w, v1[c, h])
            w_k = jnp.exp(b_last[c, h] - bi[c, h] + ii[c, h] - m_s[c + 1, h])
            upd[c, h] = _bdot_tn(jnp.where(half[h], kf[c][:, grp[h]] * w_k, 0.0), v1[c, h])
        yield
    top = lax.broadcasted_iota(jnp.int32, (2 * D_DK, LANES), 0) < D_DK
    cn = {(0, p): cn_ref[p] for p in pairs}
    for c in chunks:
        for p in pairs:
            w_a, w_b = (jnp.exp(b_last[c, h] + m_s[c, h] - m_s[c + 1, h]) for h in (2 * p, 2 * p + 1))
            w_state = jnp.where(top, w_a, w_b)
            cn[c + 1, p] = jnp.concatenate([w_state, w_state], axis=1) * cn[c, p] + upd[c, 2 * p] + upd[c, 2 * p + 1]
    for p in pairs:
        cn_ref[p] = cn[n_chunks, p]
    qcn = {(c, h): _bdot(q[c, h], cn[c, h // 2]) for c, h in ch}
    yield
    inv_dv = jnp.full((LANES, LANES), 1.0 / D_DV, BF16)
    for c in chunks:
        hh = []
        for h in heads:
            num = w_inter[c, h] * qcn[c, h][:, :D_DV] + pv[c, h][:, :D_DV]
            den = w_inter[c, h] * qcn[c, h][:, D_DV:] + pv[c, h][:, D_DV:]
            hh.append(num / jnp.maximum(jnp.abs(den), jnp.exp(-m_t[c, h])))
        ms = jnp.dot(jnp.concatenate([x * x for x in hh], axis=0).astype(BF16), inv_dv, preferred_element_type=F32)
        for h in heads:
            gate = og_ref[0, rows[c], vs[h]].astype(F32)
            y = gate * (hh[h] * lax.rsqrt(ms[h * chunk:(h + 1) * chunk, :] + EPS) * nrm_ref[...])
            store_y(rows[c], vs[h], y.astype(BF16))
        yield


def _tail_stages(x_ref, read_y, gmix_ref, wgate_ref, bgate_ref, wbr_ref, wout_ref, gffn_ref, wfg_ref, wfu_ref,
                 wfd_ref, gfin_ref, o_ref, final, progress=None):
    x = x_ref[...]
    hb = _rms(x, gmix_ref[...]).astype(BF16)
    yield
    merged = []
    for c0 in range(0, D_MODEL, TAIL_COLS):
        cols = slice(c0, c0 + TAIL_COLS)
        acc = None
        for i in range(N_BRANCH):
            gate = _sigmoid(jnp.dot(hb, wgate_ref[i, :, cols], preferred_element_type=F32) + bgate_ref[i, :, cols])
            term = gate * jnp.dot(read_y(i), wbr_ref[i, :, cols], preferred_element_type=F32)
            acc = term if acc is None else acc + term
            if progress is not None:
                progress["row"] = term[0:1, 0:LANES]
            yield
        merged.append(acc.astype(BF16))
    mb = jnp.concatenate(merged, axis=1)
    x1 = []
    for c0 in range(0, D_MODEL, TAIL_COLS):
        cols = slice(c0, c0 + TAIL_COLS)
        x1.append(x[:, cols] + jnp.dot(mb, wout_ref[:, cols], preferred_element_type=F32))
        if progress is not None:
            progress["row"] = x1[-1][0:1, 0:LANES]
        yield
    x = jnp.concatenate(x1, axis=1)
    h2 = _rms(x, gffn_ref[...]).astype(BF16)

    def hidden(c):
        a = jnp.dot(h2, wfg_ref[:, c], preferred_element_type=F32)
        u = jnp.dot(h2, wfu_ref[:, c], preferred_element_type=F32)
        return (a * _sigmoid(a) * u).astype(BF16)

    bounds = list(range(0, FFN_HIDDEN, FFN_CHUNK)) + [FFN_HIDDEN]
    pieces = [slice(lo, hi) for lo, hi in zip(bounds[:-1], bounds[1:])]
    f = hidden(pieces[0])
    yield
    for n, piece in enumerate(pieces):
        f_next = hidden(pieces[n + 1]) if n + 1 < len(pieces) else None
        x = x + jnp.dot(f, wfd_ref[piece, :], preferred_element_type=F32)
        f = f_next
        if progress is not None:
            progress["row"] = x[0:1, 0:LANES]
        yield
    if final:
        x = _rms(x, gfin_ref[...])
    o_ref[...] = x


def _pool_kernel(u_ref, buf_ref, pw_ref, sc_ref, y_ref, nbuf_ref, full_ref, *, tt, pos0):
    t = pl.program_id(1)
    hdr = POOL_HDR

    @pl.when(t == 0)
    def _():
        full_ref[1:hdr, :] = buf_ref[0]

    @pl.when(t > 0)
    def _():
        full_ref[1:hdr, :] = full_ref[tt + 1:tt + hdr, :]

    def store_y(rows, cols, val):
        y_ref[0, rows, cols] = val

    pos = pos0 + t * tt + lax.broadcasted_iota(jnp.int32, (tt, 1), 0)
    _run(_pool_stages(u_ref, pw_ref, sc_ref, full_ref, store_y, pos, tt))
    nbuf_ref[0] = full_ref[tt + 1:tt + hdr, :]


def _pool_mixer(z3, buf, pool_w, scale, pos0):
    b, t_len, _ = z3.shape
    tt = min(TIME_TILE, t_len)
    return pl.pallas_call(
        functools.partial(_pool_kernel, tt=tt, pos0=pos0),
        grid=(b, t_len // tt),
        in_specs=[pl.BlockSpec((1, tt, A_WIDTH), lambda i, j: (i, j, Z_A // A_WIDTH)),
                  pl.BlockSpec((1, POOL_BUF, A_WIDTH), lambda i, j: (i, 0, 0)),
                  _resident((4, A_GROUP, A_GROUP)), _resident((1, A_WIDTH))],
        out_specs=[pl.BlockSpec((1, tt, A_WIDTH), lambda i, j: (i, j, 0)),
                   pl.BlockSpec((1, POOL_BUF, A_WIDTH), lambda i, j: (i, 0, 0))],
        out_shape=[jax.ShapeDtypeStruct((b, t_len, A_WIDTH), BF16),
                   jax.ShapeDtypeStruct((b, POOL_BUF, A_WIDTH), F32)],
        scratch_shapes=[pltpu.VMEM((POOL_HDR + tt, A_WIDTH), F32)],
        compiler_params=pltpu.CompilerParams(dimension_semantics=("parallel", "arbitrary"),
                                             vmem_limit_bytes=VMEM_LIMIT),
        name="pool",
    )(z3, buf, pool_w, scale)


def _gdn_kernel(q_ref, k_ref, v_ref, g_ref, sm_ref, cbuf_ref, s_in_ref, cw_ref, alog_ref, dt_ref, nrm_ref,
                selg_ref, selb_ref, y_ref, ncbuf_ref, s_ref, full_ref, *, tt, chunk):
    t = pl.program_id(1)
    hdr = CONV_HDR

    @pl.when(t == 0)
    def _():
        full_ref[hdr - 3:hdr, :] = cbuf_ref[0]
        s_ref[0] = s_in_ref[0]

    @pl.when(t > 0)
    def _():
        full_ref[hdr - 3:hdr, :] = full_ref[tt + hdr - 3:tt + hdr, :]

    def store_y(rows, cols, val):
        y_ref[0, rows, cols] = val

    _run(_gdn_stages(q_ref, k_ref, v_ref, g_ref, sm_ref, cw_ref, alog_ref, dt_ref, nrm_ref, selg_ref, selb_ref,
                     s_ref.at[0], full_ref, store_y, tt, chunk))
    ncbuf_ref[0] = full_ref[tt + hdr - 3:tt + hdr, :]


def _gdn_mixer(z3, sm3, cbuf, s0, conv_w, alog_row, dt_row, norm_row, chunk):
    b, t_len, _ = z3.shape
    tt = min(TIME_TILE, t_len)
    wide = lambda off: pl.BlockSpec((1, tt, 512), lambda i, j: (i, j, off // 512))
    sel = _resident((2 * LANES, B_HEADS * LANES))
    return pl.pallas_call(
        functools.partial(_gdn_kernel, tt=tt, chunk=chunk),
        grid=(b, t_len // tt),
        in_specs=[wide(Z_BQ), wide(Z_BK), wide(Z_BV), wide(Z_BG),
                  pl.BlockSpec((1, tt, LANES), lambda i, j: (i, j, 0)),
                  pl.BlockSpec((1, CONV_W - 1, B_CONV_CH), lambda i, j: (i, 0, 0)),
                  pl.BlockSpec((1, B_HEADS, B_DK, B_DV), lambda i, j: (i, 0, 0, 0)),
                  _resident((CONV_W, B_CONV_CH)), _resident((1, LANES)), _resident((1, LANES)),
                  _resident((1, B_DV)), sel, sel],
        out_specs=[pl.BlockSpec((1, tt, B_V), lambda i, j: (i, j, 0)),
                   pl.BlockSpec((1, CONV_W - 1, B_CONV_CH), lambda i, j: (i, 0, 0)),
                   pl.BlockSpec((1, B_HEADS, B_DK, B_DV), lambda i, j: (i, 0, 0, 0))],
        out_shape=[jax.ShapeDtypeStruct((b, t_len, B_V), BF16),
                   jax.ShapeDtypeStruct((b, CONV_W - 1, B_CONV_CH), F32),
                   jax.ShapeDtypeStruct((b, B_HEADS, B_DK, B_DV), F32)],
        scratch_shapes=[pltpu.VMEM((CONV_HDR + tt, B_CONV_CH), F32)],
        compiler_params=pltpu.CompilerParams(dimension_semantics=("parallel", "arbitrary"),
                                             vmem_limit_bytes=VMEM_LIMIT),
        name="gdn",
    )(z3, z3, z3, z3, sm3, cbuf, s0, conv_w, alog_row, dt_row, norm_row,
      _head_select(S_BA, B_HEADS), _head_select(S_BB, B_HEADS))


def _gla_kernel(q_ref, k_ref, v_ref, g_ref, sm_ref, s_in_ref, wup_ref, bup_ref, nrm_ref,
                y_ref, s_ref, *, tt, chunk):
    @pl.when(pl.program_id(1) == 0)
    def _():
        s_ref[0] = s_in_ref[0]

    def store_y(rows, cols, val):
        y_ref[0, rows, cols] = val

    _run(_gla_stages(q_ref, k_ref, v_ref, g_ref, sm_ref, wup_ref, bup_ref, nrm_ref, s_ref.at[0], store_y, tt, chunk))


def _gla_mixer(z3, sm3, s0, wup_pack, bup_row, norm_row, chunk):
    b, t_len, _ = z3.shape
    tt = min(TIME_TILE, t_len)
    blk = lambda w, off: pl.BlockSpec((1, tt, w), lambda i, j: (i, j, off // w))
    state = pl.BlockSpec((1, C_HEADS // 2, 2 * C_DK, C_DV), lambda i, j: (i, 0, 0, 0))
    return pl.pallas_call(
        functools.partial(_gla_kernel, tt=tt, chunk=chunk),
        grid=(b, t_len // tt),
        in_specs=[blk(C_QK, Z_CQ), blk(C_QK, Z_CK), blk(C_V, Z_CV), blk(C_V, Z_CG),
                  pl.BlockSpec((1, tt, LANES), lambda i, j: (i, j, 0)), state,
                  _resident((3 * LANES, C_QK)), _resident((1, C_QK)), _resident((1, C_DV))],
        out_specs=[pl.BlockSpec((1, tt, C_V), lambda i, j: (i, j, 0)), state],
        out_shape=[jax.ShapeDtypeStruct((b, t_len, C_V), BF16),
                   jax.ShapeDtypeStruct((b, C_HEADS // 2, 2 * C_DK, C_DV), F32)],
        compiler_params=pltpu.CompilerParams(dimension_semantics=("parallel", "arbitrary"),
                                             vmem_limit_bytes=VMEM_LIMIT),
        name="gla",
    )(z3, z3, z3, z3, sm3, s0, wup_pack, bup_row, norm_row)


def _mlstm_kernel(q_ref, k_ref, v_ref, og_ref, sm_ref, cn_in_ref, m_in_ref, bi_ref, bf_ref, nrm_ref,
                  self_ref, seli_ref, y_ref, cn_ref, m_ref, *, tt, chunk):
    @pl.when(pl.program_id(1) == 0)
    def _():
        cn_ref[0] = cn_in_ref[0]
        m_ref[0] = m_in_ref[0]

    def store_y(rows, cols, val):
        y_ref[0, rows, cols] = val

    _run(_mlstm_stages(q_ref, k_ref, v_ref, og_ref, sm_ref, bi_ref, bf_ref, nrm_ref, self_ref, seli_ref,
                       cn_ref.at[0], m_ref.at[0], store_y, tt, chunk))


def _mlstm_mixer(z3, sm3, cn0, m0, bi_row, bf_row, norm_row, chunk):
    b, t_len, _ = z3.shape
    tt = min(TIME_TILE, t_len)
    blk = lambda w, off: pl.BlockSpec((1, tt, w), lambda i, j: (i, j, off // w))
    cn_spec = pl.BlockSpec((1, D_HEADS // 2, 2 * D_DK, 2 * D_DV), lambda i, j: (i, 0, 0, 0))
    m_spec = pl.BlockSpec((1, 8, LANES), lambda i, j: (i, 0, 0))
    sel = _resident((2 * LANES, D_HEADS * LANES))
    return pl.pallas_call(
        functools.partial(_mlstm_kernel, tt=tt, chunk=chunk),
        grid=(b, t_len // tt),
        in_specs=[blk(D_QK, Z_DQ), blk(D_QK, Z_DK), blk(D_V, Z_DV), blk(D_V, Z_DO),
                  pl.BlockSpec((1, tt, LANES), lambda i, j: (i, j, 0)), cn_spec, m_spec,
                  _resident((1, LANES)), _resident((1, LANES)), _resident((1, D_DV)), sel, sel],
        out_specs=[pl.BlockSpec((1, tt, D_V), lambda i, j: (i, j, 0)), cn_spec, m_spec],
        out_shape=[jax.ShapeDtypeStruct((b, t_len, D_V), BF16),
                   jax.ShapeDtypeStruct((b, D_HEADS // 2, 2 * D_DK, 2 * D_DV), F32),
                   jax.ShapeDtypeStruct((b, 8, LANES), F32)],
        compiler_params=pltpu.CompilerParams(dimension_semantics=("parallel", "arbitrary"),
                                             vmem_limit_bytes=VMEM_LIMIT),
        name="mlstm",
    )(z3, z3, z3, z3, sm3, cn0, m0, bi_row, bf_row, norm_row,
      _head_select(S_DF, D_HEADS), _head_select(S_DI, D_HEADS))


def _tail_kernel(x_ref, ya_ref, yb_ref, yc_ref, yd_ref, gmix_ref, wgate_ref, bgate_ref, wbr_ref, wout_ref,
                 gffn_ref, wfg_ref, wfu_ref, wfd_ref, gfin_ref, o_ref, *, final):
    ys = (ya_ref, yb_ref, yc_ref, yd_ref)
    _run(_tail_stages(x_ref, lambda i: ys[i][...], gmix_ref, wgate_ref, bgate_ref, wbr_ref, wout_ref, gffn_ref,
                      wfg_ref, wfu_ref, wfd_ref, gfin_ref, o_ref, final))


def _tail_weight_specs():
    return [_resident((1, D_MODEL)), _resident((N_BRANCH, D_MODEL, D_MODEL)), _resident((N_BRANCH, 1, D_MODEL)),
            _resident((N_BRANCH, BR_WIDTH, D_MODEL)), _resident((D_MODEL, D_MODEL)), _resident((1, D_MODEL)),
            _resident((D_MODEL, FFN_HIDDEN)), _resident((D_MODEL, FFN_HIDDEN)),
            _resident((FFN_HIDDEN, D_MODEL)), _resident((1, D_MODEL))]


def _tail_weights(lp, gfin):
    return (lp["gmix"], lp["wgate"], lp["bgate"], lp["wbr"], lp["wout"], lp["gffn"], lp["wfg"], lp["wfu"], lp["wfd"],
            gfin)


def _tail(x2d, ys, lp, gfin, final):
    n = x2d.shape[0]
    r = min(ROW_TILE, n)
    row = lambda w: pl.BlockSpec((r, w), lambda i: (i, 0))
    return pl.pallas_call(
        functools.partial(_tail_kernel, final=final),
        grid=(n // r,),
        in_specs=[row(D_MODEL)] + [row(BR_WIDTH)] * N_BRANCH + _tail_weight_specs(),
        out_specs=row(D_MODEL),
        out_shape=jax.ShapeDtypeStruct((n, D_MODEL), F32),
        compiler_params=pltpu.CompilerParams(dimension_semantics=("parallel",), vmem_limit_bytes=VMEM_LIMIT),
        name="tail",
    )(x2d, *ys, *_tail_weights(lp, gfin))


def _mix_tail_kernel(z_ref, sm_ref,
                     abuf_in, cbuf_in, gs_in, cs_in, cn_in, m_in, x_ref,
                     pw_ref, psc_ref, cw_ref, alog_ref, dt_ref, gnrm_ref, selg_ref, selb_ref,
                     wup_ref, bup_ref, cnrm_ref, bi_ref, bf_ref, dnrm_ref, self_ref, seli_ref,
                     gmix_ref, wgate_ref, bgate_ref, wbr_ref, wout_ref, gffn_ref, wfg_ref, wfu_ref, wfd_ref, gfin_ref,
                     o_ref, abuf_out, cbuf_out, gs_out, cs_out, cn_out, m_out,
                     ybuf, pfull, cfull, gs, cs, cn, m, *, tt, chunk, nt, n_tiles, pos0, final):
    s = pl.program_id(0)
    t = jnp.minimum(s, n_tiles - 1) % nt
    slot = s % 2
    piece = lambda off, w: z_ref.at[:, :, off:off + w]
    a_ref = piece(Z_A, A_WIDTH)
    bq_ref, bk_ref, bv_ref, bg_ref = piece(Z_BQ, B_QK), piece(Z_BK, B_QK), piece(Z_BV, B_V), piece(Z_BG, B_V)
    cq_ref, ck_ref, cv_ref, cg_ref = piece(Z_CQ, C_QK), piece(Z_CK, C_QK), piece(Z_CV, C_V), piece(Z_CG, C_V)
    dq_ref, dk_ref, dv_ref, do_ref = piece(Z_DQ, D_QK), piece(Z_DK, D_QK), piece(Z_DV, D_V), piece(Z_DO, D_V)

    @pl.when(s == 0)
    def _():
        ybuf[1] = jnp.zeros(ybuf.shape[1:], BF16)

    @pl.when(t == 0)
    def _():
        pfull[1:POOL_HDR, :] = abuf_in[0]
        cfull[CONV_HDR - 3:CONV_HDR, :] = cbuf_in[0]
        gs[...] = gs_in[0]
        cs[...] = cs_in[0]
        cn[...] = cn_in[0]
        m[...] = m_in[0]

    @pl.when(t > 0)
    def _():
        pfull[1:POOL_HDR, :] = pfull[tt + 1:tt + POOL_HDR, :]
        cfull[CONV_HDR - 3:CONV_HDR, :] = cfull[tt + CONV_HDR - 3:tt + CONV_HDR, :]

    def store_y(branch):
        def store(rows, cols, val):
            ybuf[slot, rows, slice(branch * BR_WIDTH + cols.start, branch * BR_WIDTH + cols.stop)] = val
        return store

    def read_y(i):
        return ybuf[1 - slot, :, i * BR_WIDTH:(i + 1) * BR_WIDTH]

    pos = pos0 + t * tt + lax.broadcasted_iota(jnp.int32, (tt, 1), 0)
    progress = {}
    _run(_gdn_stages(bq_ref, bk_ref, bv_ref, bg_ref, sm_ref, cw_ref, alog_ref, dt_ref, gnrm_ref, selg_ref, selb_ref,
                     gs, cfull, store_y(1), tt, chunk, progress),
         _mlstm_stages(dq_ref, dk_ref, dv_ref, do_ref, sm_ref, bi_ref, bf_ref, dnrm_ref, self_ref, seli_ref,
                       cn, m, store_y(3), tt, chunk),
         _gla_stages(cq_ref, ck_ref, cv_ref, cg_ref, sm_ref, wup_ref, bup_ref, cnrm_ref, cs, store_y(2), tt, chunk),
         _pool_stages(a_ref, pw_ref, psc_ref, pfull, store_y(0), pos, tt, progress),
         lead=_tail_stages(x_ref, read_y, gmix_ref, wgate_ref, bgate_ref, wbr_ref, wout_ref, gffn_ref, wfg_ref,
                           wfu_ref, wfd_ref, gfin_ref, o_ref, final, progress),
         ratio=TAIL_RATIO)

    @pl.when((t == nt - 1) & (s < n_tiles))
    def _():
        abuf_out[0] = pfull[tt + 1:tt + POOL_HDR, :]
        cbuf_out[0] = cfull[tt + CONV_HDR - 3:tt + CONV_HDR, :]
        gs_out[0] = gs[...]
        cs_out[0] = cs[...]
        cn_out[0] = cn[...]
        m_out[0] = m[...]


def _mix_tail(x2d, z3, sm3, states, lp, gfin, pos0, chunk, final):
    b, t_len, _ = z3.shape
    tt = TIME_TILE
    nt = t_len // tt
    n_tiles = b * nt
    a_buf, conv_buf, gdn_s, gla_s, cn0, m0 = states

    def bt(s):
        sm = jnp.minimum(s, n_tiles - 1)
        return sm // nt, sm % nt

    def per_stream(shape):
        nd = len(shape)
        return pl.BlockSpec((1,) + shape, lambda s: (bt(s)[0],) + (0,) * nd)

    x_spec = pl.BlockSpec((tt, D_MODEL), lambda s: (jnp.maximum(s - 1, 0), 0))
    state_shapes = [(POOL_BUF, A_WIDTH), (CONV_W - 1, B_CONV_CH), (B_HEADS, B_DK, B_DV),
                    (C_HEADS // 2, 2 * C_DK, C_DV), (D_HEADS // 2, 2 * D_DK, 2 * D_DV), (8, LANES)]
    selb = _resident((2 * LANES, B_HEADS * LANES))
    seld = _resident((2 * LANES, D_HEADS * LANES))
    row128 = _resident((1, LANES))
    outs = pl.pallas_call(
        functools.partial(_mix_tail_kernel, tt=tt, chunk=chunk, nt=nt, n_tiles=n_tiles, pos0=pos0, final=final),
        grid=(n_tiles + 1,),
        in_specs=[pl.BlockSpec((1, tt, Z_WIDTH), lambda s: (*bt(s), 0)),
                  pl.BlockSpec((1, tt, LANES), lambda s: (*bt(s), 0))]
                 + [per_stream(sh) for sh in state_shapes] + [x_spec]
                 + [_resident((4, A_GROUP, A_GROUP)), _resident((1, A_WIDTH)),
                    _resident((CONV_W, B_CONV_CH)), row128, row128, _resident((1, B_DV)), selb, selb,
                    _resident((3 * LANES, C_QK)), _resident((1, C_QK)), _resident((1, C_DV)),
                    row128, row128, _resident((1, D_DV)), seld, seld]
                 + _tail_weight_specs(),
        out_specs=[x_spec] + [per_stream(sh) for sh in state_shapes],
        out_shape=[jax.ShapeDtypeStruct((n_tiles * tt, D_MODEL), F32)]
                  + [jax.ShapeDtypeStruct((b,) + sh, F32) for sh in state_shapes],
        scratch_shapes=[pltpu.VMEM((2, tt, N_BRANCH * BR_WIDTH), BF16),
                        pltpu.VMEM((POOL_HDR + tt, A_WIDTH), F32), pltpu.VMEM((CONV_HDR + tt, B_CONV_CH), F32)]
                       + [pltpu.VMEM(sh, F32) for sh in state_shapes[2:]],
        compiler_params=pltpu.CompilerParams(dimension_semantics=("arbitrary",), vmem_limit_bytes=VMEM_LIMIT),
        name="mix_tail",
    )(z3, sm3, a_buf, conv_buf, gdn_s, gla_s, cn0, m0, x2d,
      lp["pool_w"], lp["pool_scale"], lp["conv_w"], lp["alog_row"], lp["dt_row"], lp["gdn_norm"],
      _head_select(S_BA, B_HEADS), _head_select(S_BB, B_HEADS), lp["wup"], lp["bup"], lp["gla_norm"],
      lp["bi_row"], lp["bf_row"], lp["mlstm_norm"], _head_select(S_DF, D_HEADS), _head_select(S_DI, D_HEADS),
      *_tail_weights(lp, gfin))
    return outs[0], outs[1:]


def _lane_row(vec, off):
    return jnp.zeros((1, LANES), F32).at[0, off:off + vec.shape[0]].set(vec.astype(F32))


def _layer_params(l, p):
    w_in = p["w_in"][l]
    offs = [0]
    for s in IN_SPLITS:
        offs.append(offs[-1] + s)
    piece = lambda i: w_in[:, offs[i]:offs[i + 1]]
    wide_ids = (0, 1, 2, 3, 4, 7, 8, 9, 10, 12, 13, 14, 15)
    wz = jnp.concatenate([piece(i) for i in wide_ids], axis=1).astype(BF16)
    small = jnp.concatenate([piece(5), piece(6), piece(16), piece(17), piece(11)], axis=1)
    ws = jnp.zeros((D_MODEL, LANES), F32).at[:, :small.shape[1]].set(small).astype(BF16)
    wup = jnp.zeros((LANES, C_QK), F32).at[S_LR:S_LR + GLA_RANK].set(p["gla_w_up"][l].astype(F32))
    wup_hi = wup.astype(BF16)
    wup_lo = (wup - wup_hi.astype(F32)).astype(BF16)
    return dict(
        gmix=p["norm_mix"][l].reshape(1, D_MODEL), wz=wz, ws=ws,
        pool_w=p["pool_w"][l].astype(BF16), pool_scale=p["pool_scale"][l].reshape(1, A_WIDTH),
        conv_w=p["gdn_conv_w"][l],
        alog_row=_lane_row(-jnp.exp(p["gdn_a_log"][l].astype(F32)), S_BA),
        dt_row=_lane_row(p["gdn_dt_bias"][l], S_BA),
        gdn_norm=p["gdn_norm"][l].reshape(1, B_DV),
        wup=jnp.concatenate([wup_hi, wup_hi, wup_lo], axis=0),
        bup=p["gla_b_up"][l].reshape(1, C_QK), gla_norm=p["gla_norm"][l].reshape(1, C_DV),
        bi_row=_lane_row(p["mlstm_b_i"][l], S_DI), bf_row=_lane_row(p["mlstm_b_f"][l], S_DF),
        mlstm_norm=p["mlstm_norm"][l].reshape(1, D_DV),
        wgate=p["w_gate"][l].astype(BF16), bgate=p["b_gate"][l].reshape(4, 1, D_MODEL),
        wbr=p["w_branch"][l].astype(BF16), wout=p["w_out"][l].astype(BF16),
        gffn=p["norm_ffn"][l].reshape(1, D_MODEL),
        wfg=p["w_ffn_gate"][l].astype(BF16), wfu=p["w_ffn_up"][l].astype(BF16), wfd=p["w_ffn_down"][l].astype(BF16),
    )


def _layer(x, st, lp, gfin, pos0, chunk, final):
    b, t_len, _ = x.shape
    a_buf, conv_buf, gdn_s, gla_s, ml_c, ml_n, ml_m = st
    x2d = x.reshape(b * t_len, D_MODEL)
    z, sm = _project(x2d, lp["gmix"], lp["wz"], lp["ws"])
    z3 = z.reshape(b, t_len, Z_WIDTH)
    sm3 = sm.reshape(b, t_len, LANES)
    gla_s = gla_s.reshape(b, C_HEADS // 2, 2 * C_DK, C_DV)
    cn0 = jnp.concatenate([ml_c, jnp.broadcast_to(ml_n[..., None], ml_c.shape)], axis=-1)
    cn0 = cn0.reshape(b, D_HEADS // 2, 2 * D_DK, 2 * D_DV)
    m0 = jnp.zeros((b, 8, LANES), F32).at[:, :D_HEADS, :].set(jnp.broadcast_to(ml_m[..., None], (b, D_HEADS, LANES)))
    if t_len % TIME_TILE == 0:
        x_new, (a_new, conv_new, gdn_new, gla_new, cn_new, m_new) = _mix_tail(
            x2d, z3, sm3, (a_buf, conv_buf, gdn_s, gla_s, cn0, m0), lp, gfin, pos0, chunk, final)
    else:
        y_a, a_new = _pool_mixer(z3, a_buf, lp["pool_w"], lp["pool_scale"], pos0)
        y_b, conv_new, gdn_new = _gdn_mixer(z3, sm3, conv_buf, gdn_s, lp["conv_w"], lp["alog_row"], lp["dt_row"],
                                            lp["gdn_norm"], chunk)
        y_c, gla_new = _gla_mixer(z3, sm3, gla_s, lp["wup"], lp["bup"], lp["gla_norm"], chunk)
        y_d, cn_new, m_new = _mlstm_mixer(z3, sm3, cn0, m0, lp["bi_row"], lp["bf_row"], lp["mlstm_norm"], chunk)
        ys = [y.reshape(b * t_len, BR_WIDTH) for y in (y_a, y_b, y_c, y_d)]
        x_new = _tail(x2d, ys, lp, gfin, final)
    gla_new = gla_new.reshape(b, C_HEADS, C_DK, C_DV)
    cn_new = cn_new.reshape(b, D_HEADS, D_DK, 2 * D_DV)
    new_st = (a_new, conv_new, gdn_new, gla_new, cn_new[..., :D_DV], cn_new[..., D_DV], m_new[:, :D_HEADS, 0])
    return x_new.reshape(b, t_len, D_MODEL), new_st


def kernel(x_prompt, x_sample, state_a_pool, state_b_conv, state_b_S, state_c_S, state_d_C, state_d_n, state_d_m,
           norm_mix, w_in, pool_w, pool_scale, gdn_conv_w, gdn_a_log, gdn_dt_bias, gdn_norm,
           gla_w_up, gla_b_up, gla_norm, mlstm_b_i, mlstm_b_f, mlstm_norm,
           w_branch, w_gate, b_gate, w_out, norm_ffn, w_ffn_gate, w_ffn_up, w_ffn_down, norm_final):
    p = dict(norm_mix=norm_mix, w_in=w_in, pool_w=pool_w, pool_scale=pool_scale, gdn_conv_w=gdn_conv_w,
             gdn_a_log=gdn_a_log, gdn_dt_bias=gdn_dt_bias, gdn_norm=gdn_norm, gla_w_up=gla_w_up, gla_b_up=gla_b_up,
             gla_norm=gla_norm, mlstm_b_i=mlstm_b_i, mlstm_b_f=mlstm_b_f, mlstm_norm=mlstm_norm,
             w_branch=w_branch, w_gate=w_gate, b_gate=b_gate, w_out=w_out, norm_ffn=norm_ffn,
             w_ffn_gate=w_ffn_gate, w_ffn_up=w_ffn_up, w_ffn_down=w_ffn_down)
    bp = x_prompt.shape[0]
    zero_p = (jnp.zeros((bp, POOL_BUF, A_WIDTH), F32), jnp.zeros((bp, CONV_W - 1, B_CONV_CH), F32),
              jnp.zeros((bp, B_HEADS, B_DK, B_DV), F32), jnp.zeros((bp, C_HEADS, C_DK, C_DV), F32),
              jnp.zeros((bp, D_HEADS, D_DK, D_DV), F32), jnp.zeros((bp, D_HEADS, D_DK), F32),
              jnp.zeros((bp, D_HEADS), F32))
    gfin = norm_final.reshape(1, D_MODEL)
    yp, ys = x_prompt, x_sample
    new_p, new_s = [], []
    for l in range(DEPTH):
        lp = _layer_params(l, p)
        final = l == DEPTH - 1
        yp, sp = _layer(yp, zero_p, lp, gfin, 0, CHUNK, final)
        cache_l = (state_a_pool[l], state_b_conv[l], state_b_S[l], state_c_S[l],
                   state_d_C[l], state_d_n[l], state_d_m[l])
        ys, ss = _layer(ys, cache_l, lp, gfin, PAST_LEN, x_sample.shape[1], final)
        new_p.append(sp)
        new_s.append(ss)
    outs_p = [jnp.stack([s[i] for s in new_p]) for i in range(7)]
    outs_s = [jnp.stack([s[i] for s in new_s]) for i in range(7)]
    return (yp, ys, *outs_p, *outs_s)
```
